```python
import functools
import jax, jax.numpy as jnp
from jax import lax
import numpy as np

D_MODEL = 2048
BATCH = 4
SEQ = 2048
DEPTH = 1
DEC_BATCH = 16
DEC_SEQ = 16
PAST_LEN = 4096

CHUNK = 64
LEFT_CHUNKS = 8
PAST_BAND = LEFT_CHUNKS * CHUNK
BAND = PAST_BAND + CHUNK
HEAD_DIM = 64
ATT_HEADS = D_MODEL // 128
ATT_WIDTH = ATT_HEADS * HEAD_DIM
RWKV_HEADS = D_MODEL // 128
RWKV_WIDTH = RWKV_HEADS * HEAD_DIM
DECAY_LORA = 96
ICLR_LORA = 96
GATE_LORA = 256
SHIFT_WIDTH = 3 * RWKV_WIDTH + DECAY_LORA + ICLR_LORA + GATE_LORA
PROJ_WIDTH = 3 * ATT_WIDTH + SHIFT_WIDTH + 2 * D_MODEL
SPLIT_IN = (ATT_WIDTH, 2 * ATT_WIDTH, 3 * ATT_WIDTH, 3 * ATT_WIDTH + SHIFT_WIDTH,
            3 * ATT_WIDTH + SHIFT_WIDTH + D_MODEL)
SPLIT_RWKV = (RWKV_WIDTH, 2 * RWKV_WIDTH, 3 * RWKV_WIDTH, 3 * RWKV_WIDTH + DECAY_LORA,
              3 * RWKV_WIDTH + DECAY_LORA + ICLR_LORA)
D_FF = 5632
MAX_REL = 128
RMS_EPS = 1e-6
GN_EPS = 64e-5
NEG_INF = -1e30

kernel_name = 'streaming_band_attn_rwkv7_macaron'


def rmsnorm(x, g):
    xf = x.astype(jnp.float32)
    y = xf * lax.rsqrt(jnp.mean(xf * xf, axis=-1, keepdims=True) + RMS_EPS)
    return (y * g.astype(jnp.float32)).astype(x.dtype)


def swiglu(x, w_in, w_down):
    gate, up = jnp.split(x @ w_in, 2, axis=-1)
    return (jax.nn.silu(gate) * up) @ w_down


def macaron_ffn(x, g_pre, g_post, w_in, w_down):
    return x + 0.5 * rmsnorm(swiglu(rmsnorm(x, g_pre), w_in, w_down), g_post)


def rel_bias_lookup(rel_bias, rel):
    return rel_bias[:, jnp.clip(rel, -MAX_REL, MAX_REL) + MAX_REL]


def softmax_attention(q, k, v, bias, mask):
    s = jnp.einsum('...qhd,...khd->...hqk', q, k).astype(jnp.float32) * (HEAD_DIM ** -0.5)
    s = s + bias.astype(jnp.float32)
    if mask is not None:
        s = jnp.where(mask, s, NEG_INF)
    p = jax.nn.softmax(s, axis=-1).astype(v.dtype)
    return jnp.einsum('...hqk,...khd->...qhd', p, v)


def band_attention_prompt(q, k, v, rel_bias):
    B, S, H, Dh = q.shape
    n_chunks = S // CHUNK
    qc = q.reshape(B, n_chunks, CHUNK, H, Dh)
    pad = ((0, 0), (PAST_BAND, 0), (0, 0), (0, 0))
    kc = jnp.pad(k, pad).reshape(B, n_chunks + LEFT_CHUNKS, CHUNK, H, Dh)
    vc = jnp.pad(v, pad).reshape(B, n_chunks + LEFT_CHUNKS, CHUNK, H, Dh)
    band_idx = jnp.arange(n_chunks)[:, None] + jnp.arange(LEFT_CHUNKS + 1)[None, :]
    kb = kc[:, band_idx].reshape(B, n_chunks, BAND, H, Dh)
    vb = vc[:, band_idx].reshape(B, n_chunks, BAND, H, Dh)
    rel = jnp.arange(CHUNK)[:, None] + PAST_BAND - jnp.arange(BAND)[None, :]
    bias = rel_bias_lookup(rel_bias, rel)
    key_pos = (jnp.arange(n_chunks)[:, None] - LEFT_CHUNKS) * CHUNK + jnp.arange(BAND)[None, :]
    mask = (key_pos >= 0)[:, None, None, :]
    return softmax_attention(qc, kb, vb, bias, mask).reshape(B, S, H, Dh)


def band_attention_sample(q, k, v, cache_k, cache_v, rel_bias):
    T = q.shape[1]
    n_past = cache_k.shape[1]
    kk = jnp.concatenate([cache_k.astype(k.dtype), k], axis=1)
    vv = jnp.concatenate([cache_v.astype(v.dtype), v], axis=1)
    rel = jnp.arange(T)[:, None] + n_past - jnp.arange(n_past + T)[None, :]
    bias = rel_bias_lookup(rel_bias, rel)
    return softmax_attention(q, kk, vv, bias, None)


def rwkv7_time_mix(p, prev_row, wkv0, mu, w0, w_up, a0, a_up, g_up, k_k, k_a, r_k, ln_w, ln_b):
    B, T, _ = p.shape
    dt = p.dtype
    f32 = jnp.float32
    prev = jnp.concatenate([prev_row.astype(dt), p[:, :-1]], axis=1)
    xm = p + (prev - p) * mu
    r, k, v, wd, ad, gd = jnp.split(xm, SPLIT_RWKV, axis=-1)
    w_log = -jax.nn.softplus(-(w0 + jnp.tanh(wd) @ w_up).astype(f32)) - 0.5
    decay = jnp.exp(-jnp.exp(w_log))
    a = jax.nn.sigmoid((a0 + ad @ a_up).astype(f32))
    g = jax.nn.sigmoid(gd) @ g_up

    def heads(t):
        return t.astype(f32).reshape(B, T, RWKV_HEADS, HEAD_DIM)

    kk = heads(k * k_k)
    kk = kk / jnp.maximum(jnp.sqrt(jnp.sum(kk * kk, axis=-1, keepdims=True)), 1e-12)
    k = k.astype(f32) * (1.0 + (a - 1.0) * k_a)
    r_h, k_h, v_h, a_h, w_h = heads(r), heads(k), heads(v), heads(a), heads(decay)

    def step(S, inp):
        r_t, w_t, k_t, v_t, kk_t, a_t = inp
        sa = jnp.einsum('bhvk,bhk->bhv', S, -kk_t)
        S = (S * w_t[:, :, None, :] + sa[:, :, :, None] * (kk_t * a_t)[:, :, None, :]
             + v_t[:, :, :, None] * k_t[:, :, None, :])
        return S, jnp.einsum('bhvk,bhk->bhv', S, r_t)

    xs = tuple(jnp.moveaxis(t, 1, 0) for t in (r_h, w_h, k_h, v_h, kk, a_h))
    S_final, ys = lax.scan(step, wkv0.astype(f32), xs)
    y = jnp.moveaxis(ys, 0, 1)
    mean = jnp.mean(y, axis=-1, keepdims=True)
    var = jnp.mean(jnp.square(y - mean), axis=-1, keepdims=True)
    y = ((y - mean) * lax.rsqrt(var + GN_EPS)).reshape(B, T, RWKV_WIDTH) * ln_w + ln_b
    bonus = jnp.sum(r_h * k_h * r_k, axis=-1, keepdims=True) * v_h
    y = y + bonus.reshape(B, T, RWKV_WIDTH)
    return (y * g).astype(dt), S_final.astype(wkv0.dtype), p[:, -1:]


def trunk_layer(x, attend, prev_row, wkv0, norm_ffn1_pre, norm_ffn1_post, w_ffn1_in, w_ffn1_down,
                norm_mix_pre, norm_mix_post, w_in, w_att_out, rwkv_mu, rwkv_w0, rwkv_w_up, rwkv_a0,
                rwkv_a_up, rwkv_g_up, rwkv_k_k, rwkv_k_a, rwkv_r_k, rwkv_ln_w, rwkv_ln_b, w_rwkv_out,
                w_out, norm_ffn2_pre, norm_ffn2_post, w_ffn2_in, w_ffn2_down):
    B, T, _ = x.shape
    x = macaron_ffn(x, norm_ffn1_pre, norm_ffn1_post, w_ffn1_in, w_ffn1_down)
    h = rmsnorm(x, norm_mix_pre)
    q, k, v, pb, gate_a, gate_b = jnp.split(h @ w_in, SPLIT_IN, axis=-1)
    q = q.reshape(B, T, ATT_HEADS, HEAD_DIM)
    k = k.reshape(B, T, ATT_HEADS, HEAD_DIM)
    v = v.reshape(B, T, ATT_HEADS, HEAD_DIM)
    att = attend(q, k, v).reshape(B, T, ATT_WIDTH) @ w_att_out
    rw, wkv, shift_row = rwkv7_time_mix(pb, prev_row, wkv0, rwkv_mu, rwkv_w0, rwkv_w_up, rwkv_a0,
                                        rwkv_a_up, rwkv_g_up, rwkv_k_k, rwkv_k_a, rwkv_r_k,
                                        rwkv_ln_w, rwkv_ln_b)
    rw = rw @ w_rwkv_out
    mixed = (jax.nn.sigmoid(gate_a) * att + jax.nn.sigmoid(gate_b) * rw) @ w_out
    x = x + rmsnorm(mixed, norm_mix_post)
    x = macaron_ffn(x, norm_ffn2_pre, norm_ffn2_post, w_ffn2_in, w_ffn2_down)
    return x, k, v, wkv, shift_row


def setup_inputs(seed: int = 0) -> dict:
    key = jax.random.key(seed)
    ks = jax.random.split(key, 32)
    f32 = jnp.float32
    L = DEPTH
    att_past = min(PAST_BAND, PAST_LEN)

    def nrm(i, shape, scale=1.0):
        return scale * jax.random.normal(ks[i], shape, f32)

    def gain(i, shape, center=1.0):
        return center + 0.05 * jax.random.normal(ks[i], shape, f32)

    return {
        'x_prompt': nrm(0, (BATCH, SEQ, D_MODEL)),
        'x_sample': nrm(1, (DEC_BATCH, DEC_SEQ, D_MODEL)),
        'cache_k': nrm(2, (L, DEC_BATCH, att_past, ATT_HEADS, HEAD_DIM)),
        'cache_v': nrm(3, (L, DEC_BATCH, att_past, ATT_HEADS, HEAD_DIM)),
        'state_wkv': nrm(4, (L, DEC_BATCH, RWKV_HEADS, HEAD_DIM, HEAD_DIM), 0.5),
        'state_shift': nrm(5, (L, DEC_BATCH, 1, SHIFT_WIDTH)),
        'norm_ffn1_pre': gain(6, (L, D_MODEL)),
        'norm_ffn1_post': gain(7, (L, D_MODEL)),
        'w_ffn1_in': nrm(8, (L, D_MODEL, 2 * D_FF), D_MODEL ** -0.5),
        'w_ffn1_down': nrm(9, (L, D_FF, D_MODEL), D_FF ** -0.5),
        'norm_mix_pre': gain(10, (L, D_MODEL)),
        'norm_mix_post': gain(11, (L, D_MODEL)),
        'w_in': nrm(12, (L, D_MODEL, PROJ_WIDTH), D_MODEL ** -0.5),
        'rel_bias': nrm(13, (L, ATT_HEADS, 2 * MAX_REL + 1), 0.5),
        'w_att_out': nrm(14, (L, ATT_WIDTH, D_MODEL), ATT_WIDTH ** -0.5),
        'rwkv_mu': jax.random.uniform(ks[15], (L, SHIFT_WIDTH), f32, 0.0, 1.0),
        'rwkv_w0': jax.random.uniform(ks[16], (L, RWKV_WIDTH), f32, -6.0, -1.0),
        'rwkv_w_up': nrm(17, (L, DECAY_LORA, RWKV_WIDTH), 0.5 * DECAY_LORA ** -0.5),
        'rwkv_a0': nrm(18, (L, RWKV_WIDTH), 0.1),
        'rwkv_a_up': nrm(19, (L, ICLR_LORA, RWKV_WIDTH), 0.5 * ICLR_LORA ** -0.5),
        'rwkv_g_up': nrm(20, (L, GATE_LORA, RWKV_WIDTH), GATE_LORA ** -0.5),
        'rwkv_k_k': gain(21, (L, RWKV_WIDTH), 0.85),
        'rwkv_k_a': gain(22, (L, RWKV_WIDTH)),
        'rwkv_r_k': nrm(23, (L, RWKV_HEADS, HEAD_DIM), 0.1),
        'rwkv_ln_w': gain(24, (L, RWKV_WIDTH)),
        'rwkv_ln_b': nrm(25, (L, RWKV_WIDTH), 0.02),
        'w_rwkv_out': nrm(26, (L, RWKV_WIDTH, D_MODEL), RWKV_WIDTH ** -0.5),
        'w_out': nrm(27, (L, D_MODEL, D_MODEL), D_MODEL ** -0.5),
        'norm_ffn2_pre': gain(28, (L, D_MODEL)),
        'norm_ffn2_post': gain(29, (L, D_MODEL)),
        'w_ffn2_in': nrm(30, (L, D_MODEL, 2 * D_FF), D_MODEL ** -0.5),
        'w_ffn2_down': nrm(31, (L, D_FF, D_MODEL), D_FF ** -0.5),
    }


def reference(x_prompt, x_sample, cache_k, cache_v, state_wkv, state_shift,
              norm_ffn1_pre, norm_ffn1_post, w_ffn1_in, w_ffn1_down,
              norm_mix_pre, norm_mix_post, w_in, rel_bias, w_att_out,
              rwkv_mu, rwkv_w0, rwkv_w_up, rwkv_a0, rwkv_a_up, rwkv_g_up,
              rwkv_k_k, rwkv_k_a, rwkv_r_k, rwkv_ln_w, rwkv_ln_b, w_rwkv_out, w_out,
              norm_ffn2_pre, norm_ffn2_post, w_ffn2_in, w_ffn2_down):
    B = x_prompt.shape[0]
    keep_rows = min(PAST_BAND, x_prompt.shape[1])
    y_p, y_s = x_prompt, x_sample
    kp_l, vp_l, wp_l, sp_l = [], [], [], []
    ks_l, vs_l, ws_l, ss_l = [], [], [], []
    for l in range(DEPTH):
        layer = functools.partial(
            trunk_layer,
            norm_ffn1_pre=norm_ffn1_pre[l], norm_ffn1_post=norm_ffn1_post[l],
            w_ffn1_in=w_ffn1_in[l], w_ffn1_down=w_ffn1_down[l],
            norm_mix_pre=norm_mix_pre[l], norm_mix_post=norm_mix_post[l],
            w_in=w_in[l], w_att_out=w_att_out[l],
            rwkv_mu=rwkv_mu[l], rwkv_w0=rwkv_w0[l], rwkv_w_up=rwkv_w_up[l], rwkv_a0=rwkv_a0[l],
            rwkv_a_up=rwkv_a_up[l], rwkv_g_up=rwkv_g_up[l], rwkv_k_k=rwkv_k_k[l],
            rwkv_k_a=rwkv_k_a[l], rwkv_r_k=rwkv_r_k[l], rwkv_ln_w=rwkv_ln_w[l],
            rwkv_ln_b=rwkv_ln_b[l], w_rwkv_out=w_rwkv_out[l], w_out=w_out[l],
            norm_ffn2_pre=norm_ffn2_pre[l], norm_ffn2_post=norm_ffn2_post[l],
            w_ffn2_in=w_ffn2_in[l], w_ffn2_down=w_ffn2_down[l])
        attend_p = functools.partial(band_attention_prompt, rel_bias=rel_bias[l])
        shift0 = jnp.zeros((B, 1, SHIFT_WIDTH), x_prompt.dtype)
        wkv0 = jnp.zeros((B, RWKV_HEADS, HEAD_DIM, HEAD_DIM), state_wkv.dtype)
        y_p, k_p, v_p, wkv_p, sh_p = layer(y_p, attend_p, shift0, wkv0)
        kp_l.append(k_p[:, -keep_rows:])
        vp_l.append(v_p[:, -keep_rows:])
        wp_l.append(wkv_p)
        sp_l.append(sh_p)
        attend_s = functools.partial(band_attention_sample, cache_k=cache_k[l], cache_v=cache_v[l],
                                     rel_bias=rel_bias[l])
        y_s, k_s, v_s, wkv_s, sh_s = layer(y_s, attend_s, state_shift[l], state_wkv[l])
        ks_l.append(k_s)
        vs_l.append(v_s)
        ws_l.append(wkv_s)
        ss_l.append(sh_s)
    new_k_prompt = jnp.stack(kp_l)
    new_v_prompt = jnp.stack(vp_l)
    new_wkv_prompt = jnp.stack(wp_l)
    new_shift_prompt = jnp.stack(sp_l)
    new_k_sample = jnp.stack(ks_l)
    new_v_sample = jnp.stack(vs_l)
    new_wkv_sample = jnp.stack(ws_l)
    new_shift_sample = jnp.stack(ss_l)
    return (y_p, y_s, new_k_prompt, new_v_prompt, new_wkv_prompt, new_shift_prompt,
            new_k_sample, new_v_sample, new_wkv_sample, new_shift_sample)
```

```python
import functools

import numpy as np
import jax
import jax.numpy as jnp
from jax import lax
from jax.experimental import pallas as pl
from jax.experimental.pallas import tpu as pltpu

F32 = jnp.float32
BF16 = jnp.bfloat16

D_MODEL = 2048
CHUNK = 64
LEFT_CHUNKS = 8
PAST_BAND = LEFT_CHUNKS * CHUNK
BAND = PAST_BAND + CHUNK
HEAD_DIM = 64
HEADS = 16
WIDTH = HEADS * HEAD_DIM
DECAY_LORA = 96
ICLR_LORA = 96
GATE_LORA = 256
LORA_PAD = 128
LORA_COLS = 2 * LORA_PAD + GATE_LORA
D_FF = 5632
MAX_REL = 128
RMS_EPS = 1e-6
GN_EPS = 64e-5
NEG_INF = -1e30

COL_Q, COL_K, COL_V = 0, WIDTH, 2 * WIDTH
COL_RKV = 3 * WIDTH
COL_GA = 6 * WIDTH
COL_GB = COL_GA + D_MODEL
COL_LORA = COL_GB + D_MODEL
PROJ_COLS = COL_LORA + LORA_COLS

VMEM_LIMIT = 56 * 1024 * 1024

FFN_TM = 768
FFN_TF = 512
PROJ_TM = 768
PROJ_TN = 1536
MERGE_TM = 384
PREP_TM = 256
SCAN_TB = 64
LANE_GROUP_BATCH = 4
KLO = HEAD_DIM // 2


def _sigmoid(x):
    return 1.0 / (1.0 + jnp.exp(-x))


def _split_bf16(x):
    hi = x.astype(BF16)
    lo = (x - hi.astype(F32)).astype(BF16)
    return hi, lo


def _dot(a, b):
    return jnp.dot(a, b, preferred_element_type=F32)


def _dot_f32(a, b_hi, b_lo):
    a_hi, a_lo = _split_bf16(a)
    return _dot(a_hi, b_hi) + _dot(a_lo, b_hi) + _dot(a_hi, b_lo)


def _segsum(x, ones_bd):
    hi, lo = _split_bf16(x)
    outs = []
    for j in range(x.shape[-1] // 128):
        sl = slice(j * 128, (j + 1) * 128)
        outs.append(_dot(hi[:, sl], ones_bd) + _dot(lo[:, sl], ones_bd))
    return jnp.concatenate(outs, axis=-1)


def _rms_scale(x):
    return lax.rsqrt(jnp.mean(x * x, axis=-1, keepdims=True) + RMS_EPS)


def _ffn_body(x_ref, gpre_ref, gpost_ref, wg_ref, wu_ref, wd_ref, o_ref, h_ref, *, n_f):
    f = pl.program_id(1)

    @pl.when(f == 0)
    def _():
        x = x_ref[...]
        h_ref[...] = (x * _rms_scale(x) * gpre_ref[...]).astype(BF16)

    h = h_ref[...]
    g = _dot(h, wg_ref[...])
    u = _dot(h, wu_ref[...])
    act = (g * _sigmoid(g) * u).astype(BF16)
    part = _dot(act, wd_ref[...])

    @pl.when(f == 0)
    def _():
        o_ref[...] = part

    @pl.when(f > 0)
    def _():
        o_ref[...] += part

    @pl.when(f == n_f - 1)
    def _():
        y = o_ref[...]
        o_ref[...] = x_ref[...] + 0.5 * (y * _rms_scale(y) * gpost_ref[...])


def _ffn(x, g_pre, g_post, w_in, w_down):
    m = x.shape[0]
    n_m, n_f = m // FFN_TM, D_FF // FFN_TF
    return pl.pallas_call(
        functools.partial(_ffn_body, n_f=n_f),
        grid=(n_m, n_f),
        in_specs=[
            pl.BlockSpec((FFN_TM, D_MODEL), lambda i, f: (i, 0)),
            pl.BlockSpec((1, D_MODEL), lambda i, f: (0, 0)),
            pl.BlockSpec((1, D_MODEL), lambda i, f: (0, 0)),
            pl.BlockSpec((D_MODEL, FFN_TF), lambda i, f: (0, f)),
            pl.BlockSpec((D_MODEL, FFN_TF), lambda i, f: (0, f + n_f)),
            pl.BlockSpec((FFN_TF, D_MODEL), lambda i, f: (f, 0)),
        ],
        out_specs=pl.BlockSpec((FFN_TM, D_MODEL), lambda i, f: (i, 0)),
        out_shape=jax.ShapeDtypeStruct((m, D_MODEL), F32),
        scratch_shapes=[pltpu.VMEM((FFN_TM, D_MODEL), BF16)],
        compiler_params=pltpu.CompilerParams(
            dimension_semantics=("parallel", "arbitrary"), vmem_limit_bytes=VMEM_LIMIT),
        name="ffn",
    )(x, g_pre, g_post, w_in, w_in, w_down)


def _proj_body(x_ref, g_ref, w_ref, o_ref, h_ref):
    @pl.when(pl.program_id(1) == 0)
    def _():
        x = x_ref[...]
        h_ref[...] = (x * _rms_scale(x) * g_ref[...]).astype(BF16)

    o_ref[...] = _dot(h_ref[...], w_ref[...])


def _proj(x, g, w):
    m = x.shape[0]
    return pl.pallas_call(
        _proj_body,
        grid=(m // PROJ_TM, PROJ_COLS // PROJ_TN),
        in_specs=[
            pl.BlockSpec((PROJ_TM, D_MODEL), lambda i, n: (i, 0)),
            pl.BlockSpec((1, D_MODEL), lambda i, n: (0, 0)),
            pl.BlockSpec((D_MODEL, PROJ_TN), lambda i, n: (0, n)),
        ],
        out_specs=pl.BlockSpec((PROJ_TM, PROJ_TN), lambda i, n: (i, n)),
        out_shape=jax.ShapeDtypeStruct((m, PROJ_COLS), F32),
        scratch_shapes=[pltpu.VMEM((PROJ_TM, D_MODEL), BF16)],
        compiler_params=pltpu.CompilerParams(
            dimension_semantics=("parallel", "arbitrary"), vmem_limit_bytes=VMEM_LIMIT),
        name="proj",
    )(x, g, w)


def _softmax_pv(parts):
    m = functools.reduce(jnp.maximum, [jnp.max(s, axis=-1, keepdims=True) for s, _ in parts])
    num, den = 0.0, 0.0
    for s, v in parts:
        p = jnp.exp(s - m)
        den = den + jnp.sum(p, axis=-1, keepdims=True)
        num = num + _dot(p.astype(BF16), v)
    return num / den


def _qk(q, k):
    return lax.dot_general(q, k, (((1,), (1,)), ((), ())), preferred_element_type=F32)


def _attn_prompt_body(q_ref, k_ref, v_ref, bias_ref, o_ref, kbf_ref, vbf_ref):
    c = pl.program_id(1)

    @pl.when(c == 0)
    def _():
        kbf_ref[...] = k_ref[...].astype(BF16)
        vbf_ref[...] = v_ref[...].astype(BF16)

    start = pl.multiple_of(jnp.maximum(c - LEFT_CHUNKS, 0) * CHUNK, CHUNK)
    q = (q_ref[...] * (HEAD_DIM ** -0.5)).astype(BF16)
    outs = []
    for h in range(HEADS):
        sl = slice(h * HEAD_DIM, (h + 1) * HEAD_DIM)
        kh = kbf_ref[pl.ds(start, BAND), sl]
        vh = vbf_ref[pl.ds(start, BAND), sl]
        s = _qk(q[:, sl], kh) + bias_ref[0, h]
        outs.append(_softmax_pv([(s, vh)]))
    o_ref[...] = jnp.concatenate(outs, axis=-1).astype(BF16)


def _attn_prompt(proj, bias_tabs, batch, seq):
    n_c = seq // CHUNK
    return pl.pallas_call(
        _attn_prompt_body,
        grid=(batch, n_c),
        in_specs=[
            pl.BlockSpec((CHUNK, WIDTH), lambda b, c: (b * n_c + c, COL_Q // WIDTH)),
            pl.BlockSpec((seq, WIDTH), lambda b, c: (b, COL_K // WIDTH)),
            pl.BlockSpec((seq, WIDTH), lambda b, c: (b, COL_V // WIDTH)),
            pl.BlockSpec((1, HEADS, CHUNK, BAND),
                         lambda b, c: (jnp.maximum(LEFT_CHUNKS - c, 0), 0, 0, 0)),
        ],
        out_specs=pl.BlockSpec((CHUNK, WIDTH), lambda b, c: (b * n_c + c, 0)),
        out_shape=jax.ShapeDtypeStruct((batch * seq, WIDTH), BF16),
        scratch_shapes=[pltpu.VMEM((seq, WIDTH), BF16), pltpu.VMEM((seq, WIDTH), BF16)],
        compiler_params=pltpu.CompilerParams(
            dimension_semantics=("parallel", "arbitrary"), vmem_limit_bytes=VMEM_LIMIT),
        name="attn_prompt",
    )(proj, proj, proj, bias_tabs)


def _attn_sample_body(q_ref, kn_ref, vn_ref, ck_ref, cv_ref, bias_c_ref, bias_n_ref, o_ref):
    q = (q_ref[...] * (HEAD_DIM ** -0.5)).astype(BF16)
    kn = kn_ref[...].astype(BF16)
    vn = vn_ref[...].astype(BF16)
    outs = []
    for h in range(HEADS):
        sl = slice(h * HEAD_DIM, (h + 1) * HEAD_DIM)
        qh = q[:, sl]
        ck = ck_ref[0, :, sl].astype(BF16)
        cv = cv_ref[0, :, sl].astype(BF16)
        s_c = _qk(qh, ck) + bias_c_ref[h]
        s_n = _qk(qh, kn[:, sl]) + bias_n_ref[h]
        outs.append(_softmax_pv([(s_c, cv), (s_n, vn[:, sl])]))
    o_ref[...] = jnp.concatenate(outs, axis=-1).astype(BF16)


def _attn_sample(proj, row0, cache_k, cache_v, bias_c, bias_n, batch, t):
    n_past = cache_k.shape[1]
    blk0 = row0 // t
    return pl.pallas_call(
        _attn_sample_body,
        grid=(batch,),
        in_specs=[
            pl.BlockSpec((t, WIDTH), lambda b: (blk0 + b, COL_Q // WIDTH)),
            pl.BlockSpec((t, WIDTH), lambda b: (blk0 + b, COL_K // WIDTH)),
            pl.BlockSpec((t, WIDTH), lambda b: (blk0 + b, COL_V // WIDTH)),
            pl.BlockSpec((1, n_past, WIDTH), lambda b: (b, 0, 0)),
            pl.BlockSpec((1, n_past, WIDTH), lambda b: (b, 0, 0)),
            pl.BlockSpec((HEADS, t, n_past), lambda b: (0, 0, 0)),
            pl.BlockSpec((HEADS, t, t), lambda b: (0, 0, 0)),
        ],
        out_specs=pl.BlockSpec((t, WIDTH), lambda b: (b, 0)),
        out_shape=jax.ShapeDtypeStruct((batch * t, WIDTH), BF16),
        compiler_params=pltpu.CompilerParams(
            dimension_semantics=("parallel",), vmem_limit_bytes=VMEM_LIMIT),
        name="attn_sample",
    )(proj, proj, proj, cache_k, cache_v, bias_c, bias_n)


def _prep_body(p_ref, pl_ref, bp_ref, bl_ref, mu_ref, mul_ref, w0_ref, a0_ref, kk_ref, ka_ref, rk_ref,
               lnb_ref, wup_hi, wup_lo, aup_hi, aup_lo, gup_hi, gup_lo, bd_ref,
               r_o, w_o, k_o, v_o, kk_o, kka_o, g_o, cb_o):
    def shifted(p, first_row, mu):
        row = lax.broadcasted_iota(jnp.int32, p.shape, 0)
        prev = jnp.where(row == 0, first_row, pltpu.roll(p, 1, axis=0))
        return p + (prev - p) * mu

    xm = shifted(p_ref[...], bp_ref[0], mu_ref[...])
    xl = shifted(pl_ref[...], bl_ref[0], mul_ref[...])
    r = xm[:, 0:WIDTH]
    k = xm[:, WIDTH:2 * WIDTH]
    v = xm[:, 2 * WIDTH:3 * WIDTH]
    wd = xl[:, 0:LORA_PAD]
    ad = xl[:, LORA_PAD:2 * LORA_PAD]
    gd = xl[:, 2 * LORA_PAD:]

    z = -(w0_ref[...] + _dot_f32(jnp.tanh(wd), wup_hi[...], wup_lo[...]))
    softplus = jnp.maximum(z, 0.0) + jnp.log(1.0 + jnp.exp(-jnp.abs(z)))
    decay = jnp.exp(-jnp.exp(-softplus - 0.5))
    a = _sigmoid(a0_ref[...] + _dot_f32(ad, aup_hi[...], aup_lo[...]))
    g = _dot_f32(_sigmoid(gd), gup_hi[...], gup_lo[...])

    bd = bd_ref[...]
    kk = k * kk_ref[...]
    kk = kk / jnp.maximum(jnp.sqrt(_segsum(kk * kk, bd)), 1e-12)
    k2 = k * (1.0 + (a - 1.0) * ka_ref[...])
    bonus = _segsum(r * k2 * rk_ref[...], bd) * v

    r_o[...] = r
    w_o[...] = decay
    k_o[...] = k2
    v_o[...] = v
    kk_o[...] = kk
    kka_o[...] = kk * a
    g_o[...] = g
    cb_o[...] = lnb_ref[...] + bonus


def _prep(proj, row0, n_rows, tm, first_rkv, first_lora, params):
    n_t = n_rows // tm
    blk0 = row0 // tm
    vec = lambda w: pl.BlockSpec((1, w), lambda i: (0, 0))
    mat = lambda r: pl.BlockSpec((r, WIDTH), lambda i: (0, 0))
    out = pl.BlockSpec((tm, WIDTH), lambda i: (i, 0))
    return pl.pallas_call(
        _prep_body,
        grid=(n_t,),
        in_specs=[
            pl.BlockSpec((tm, 3 * WIDTH), lambda i: (blk0 + i, COL_RKV // (3 * WIDTH))),
            pl.BlockSpec((tm, LORA_COLS), lambda i: (blk0 + i, COL_LORA // LORA_COLS)),
            pl.BlockSpec((1, 1, 3 * WIDTH), lambda i: (i, 0, 0)),
            pl.BlockSpec((1, 1, LORA_COLS), lambda i: (i, 0, 0)),
            vec(3 * WIDTH), vec(LORA_COLS), vec(WIDTH), vec(WIDTH), vec(WIDTH), vec(WIDTH), vec(WIDTH),
            vec(WIDTH), mat(LORA_PAD), mat(LORA_PAD), mat(LORA_PAD), mat(LORA_PAD),
            mat(GATE_LORA), mat(GATE_LORA),
            pl.BlockSpec((128, 128), lambda i: (0, 0)),
        ],
        out_specs=[out] * 8,
        out_shape=[jax.ShapeDtypeStruct((n_rows, WIDTH), F32)] * 8,
        compiler_params=pltpu.CompilerParams(
            dimension_semantics=("parallel",), vmem_limit_bytes=VMEM_LIMIT),
        name="rwkv_prep",
    )(proj, proj, first_rkv, first_lora, *params)


def _scan_body(r_ref, w_ref, k_ref, kk_ref, kka_ref, v_ref, s0_ref, y_ref, st_ref, s_ref, *, tb):
    @pl.when(pl.program_id(1) == 0)
    def _():
        s_ref[...] = s0_ref[0]

    def both_halves(x):
        return x + pltpu.roll(x, HEAD_DIM, axis=1)

    def step(t, carry):
        sa = jnp.zeros((HEAD_DIM, 128), F32)
        for j in range(KLO):
            sa = sa + s_ref[j] * kk_ref[0, t, pl.ds(j, 1), :]
        sa = -both_halves(sa)
        v = v_ref[0, t]
        y = jnp.zeros((HEAD_DIM, 128), F32)
        for j in range(KLO):
            row = pl.ds(j, 1)
            s_new = (s_ref[j] * w_ref[0, t, row, :] + sa * kka_ref[0, t, row, :]
                     + v * k_ref[0, t, row, :])
            s_ref[j] = s_new
            y = y + s_new * r_ref[0, t, row, :]
        y_ref[0, t] = both_halves(y)
        return carry

    lax.fori_loop(0, tb, step, 0)

    @pl.when(pl.program_id(1) == pl.num_programs(1) - 1)
    def _():
        st_ref[0] = s_ref[...]


def _scan(rows, v, s0, tb):
    g, t = v.shape[0], v.shape[1]
    row_spec = pl.BlockSpec((1, tb, KLO, 128), lambda i, j: (i, j, 0, 0))
    val_spec = pl.BlockSpec((1, tb, HEAD_DIM, 128), lambda i, j: (i, j, 0, 0))
    st_spec = pl.BlockSpec((1, KLO, HEAD_DIM, 128), lambda i, j: (i, 0, 0, 0))
    return pl.pallas_call(
        functools.partial(_scan_body, tb=tb),
        grid=(g, t // tb),
        in_specs=[row_spec] * 5 + [val_spec, st_spec],
        out_specs=[val_spec, st_spec],
        out_shape=[jax.ShapeDtypeStruct((g, t, HEAD_DIM, 128), F32),
                   jax.ShapeDtypeStruct((g, KLO, HEAD_DIM, 128), F32)],
        scratch_shapes=[pltpu.VMEM((KLO, HEAD_DIM, 128), F32)],
        compiler_params=pltpu.CompilerParams(
            dimension_semantics=("parallel", "arbitrary"), vmem_limit_bytes=VMEM_LIMIT),
        name="rwkv_scan",
    )(*rows, v, s0)


def _rows_to_scan(x, n_batch, t):
    g = n_batch // LANE_GROUP_BATCH
    x = x.reshape(g, LANE_GROUP_BATCH, t, HEADS, 2, KLO)
    return x.transpose(0, 2, 5, 4, 1, 3).reshape(g, t, KLO, 128)


def _vals_to_scan(x, n_batch, t):
    g = n_batch // LANE_GROUP_BATCH
    x = x.reshape(g, LANE_GROUP_BATCH, t, HEADS, HEAD_DIM).transpose(0, 2, 4, 1, 3)
    x = x.reshape(g, t, HEAD_DIM, 64)
    return jnp.concatenate([x, x], axis=-1)


def _vals_from_scan(y, n_batch, t):
    g = n_batch // LANE_GROUP_BATCH
    y = y[..., :64].reshape(g, t, HEAD_DIM, LANE_GROUP_BATCH, HEADS).transpose(0, 3, 1, 4, 2)
    return y.reshape(n_batch * t, WIDTH)


def _state_to_scan(s, n_batch):
    g = n_batch // LANE_GROUP_BATCH
    s = s.reshape(g, LANE_GROUP_BATCH, HEADS, HEAD_DIM, 2, KLO)
    return s.transpose(0, 5, 3, 4, 1, 2).reshape(g, KLO, HEAD_DIM, 128)


def _state_from_scan(s, n_batch):
    g = n_batch // LANE_GROUP_BATCH
    s = s.reshape(g, KLO, HEAD_DIM, 2, LANE_GROUP_BATCH, HEADS)
    return s.transpose(0, 4, 5, 2, 3, 1).reshape(n_batch, HEADS, HEAD_DIM, HEAD_DIM)


def _merge_body(att_ref, y_ref, g_ref, cb_ref, ga_ref, gb_ref, x_ref, lnw_ref, gpost_ref, bd_ref,
                wa_ref, wr_ref, wo_ref, o_ref):
    bd = bd_ref[...]
    y = y_ref[...]
    d = y - _segsum(y, bd) * (1.0 / HEAD_DIM)
    var = _segsum(d * d, bd) * (1.0 / HEAD_DIM)
    rw = ((d * lax.rsqrt(var + GN_EPS) * lnw_ref[...] + cb_ref[...]) * g_ref[...]).astype(BF16)
    rw_o = _dot(rw, wr_ref[...])
    att_o = _dot(att_ref[...], wa_ref[...])
    mixed = (_sigmoid(ga_ref[...]) * att_o + _sigmoid(gb_ref[...]) * rw_o).astype(BF16)
    z = _dot(mixed, wo_ref[...])
    o_ref[...] = x_ref[...] + z * _rms_scale(z) * gpost_ref[...]


def _merge(att, y, g, cb, proj, x, ln_w, g_post, ones_bd, w_att, w_rwkv, w_out):
    m = x.shape[0]
    tm = MERGE_TM
    tok = lambda w: pl.BlockSpec((tm, w), lambda i: (i, 0))
    const = lambda a: pl.BlockSpec(a.shape, lambda i: (0,) * a.ndim)
    return pl.pallas_call(
        _merge_body,
        grid=(m // tm,),
        in_specs=[
            tok(WIDTH), tok(WIDTH), tok(WIDTH), tok(WIDTH),
            pl.BlockSpec((tm, D_MODEL), lambda i: (i, COL_GA // D_MODEL)),
            pl.BlockSpec((tm, D_MODEL), lambda i: (i, COL_GB // D_MODEL)),
            tok(D_MODEL), const(ln_w), const(g_post), const(ones_bd),
            const(w_att), const(w_rwkv), const(w_out),
        ],
        out_specs=tok(D_MODEL),
        out_shape=jax.ShapeDtypeStruct((m, D_MODEL), F32),
        compiler_params=pltpu.CompilerParams(
            dimension_semantics=("parallel",), vmem_limit_bytes=VMEM_LIMIT),
        name="merge",
    )(att, y, g, cb, proj, proj, x, ln_w, g_post, ones_bd, w_att, w_rwkv, w_out)


def _pad_cols(x, width):
    return jnp.pad(x, ((0, 0), (0, width - x.shape[1])))


def _pad_rows(x, height):
    return jnp.pad(x, ((0, height - x.shape[0]), (0, 0)))


def _reorder_shift_cols(x):
    o = 3 * WIDTH
    wd = x[..., o:o + DECAY_LORA]
    ad = x[..., o + DECAY_LORA:o + DECAY_LORA + ICLR_LORA]
    gd = x[..., o + DECAY_LORA + ICLR_LORA:]
    pad = [(0, 0)] * (x.ndim - 1)
    lora = jnp.concatenate([jnp.pad(wd, pad + [(0, LORA_PAD - DECAY_LORA)]),
                            jnp.pad(ad, pad + [(0, LORA_PAD - ICLR_LORA)]), gd], axis=-1)
    return x[..., :o], lora


def _shift_row_from_proj(rows):
    lo = rows[:, COL_LORA:]
    return jnp.concatenate([rows[:, COL_RKV:COL_RKV + 3 * WIDTH], lo[:, :DECAY_LORA],
                            lo[:, LORA_PAD:LORA_PAD + ICLR_LORA], lo[:, 2 * LORA_PAD:]], axis=-1)


def _bias_tables(rel_bias):
    i = np.arange(CHUNK)[None, :, None]
    j = np.arange(BAND)[None, None, :]
    s = np.arange(LEFT_CHUNKS + 1)[:, None, None]
    idx = np.clip(i + PAST_BAND - j - CHUNK * s, -MAX_REL, MAX_REL) + MAX_REL
    valid = np.broadcast_to(j < BAND - CHUNK * s, idx.shape)
    tab = jnp.take(rel_bias, jnp.asarray(idx.reshape(-1), jnp.int32), axis=1)
    tab = tab.reshape(HEADS, LEFT_CHUNKS + 1, CHUNK, BAND).transpose(1, 0, 2, 3)
    return jnp.where(jnp.asarray(valid)[:, None], tab, NEG_INF)


def kernel(x_prompt, x_sample, cache_k, cache_v, state_wkv, state_shift, norm_ffn1_pre, norm_ffn1_post, w_ffn1_in, w_ffn1_down, norm_mix_pre, norm_mix_post, w_in, rel_bias, w_att_out, rwkv_mu, rwkv_w0, rwkv_w_up, rwkv_a0, rwkv_a_up, rwkv_g_up, rwkv_k_k, rwkv_k_a, rwkv_r_k, rwkv_ln_w, rwkv_ln_b, w_rwkv_out, w_out, norm_ffn2_pre, norm_ffn2_post, w_ffn2_in, w_ffn2_down):
    depth = w_in.shape[0]
    batch, seq, _ = x_prompt.shape
    dec_batch, dec_seq, _ = x_sample.shape
    n_past = cache_k.shape[2]
    m_p, m_s = batch * seq, dec_batch * dec_seq
    keep = min(PAST_BAND, seq)
    assert n_past == PAST_BAND and seq % CHUNK == 0 and seq // CHUNK > LEFT_CHUNKS
    assert (m_p + m_s) % FFN_TM == 0 and (m_p + m_s) % MERGE_TM == 0 and seq % PREP_TM == 0
    assert batch == LANE_GROUP_BATCH and dec_batch % LANE_GROUP_BATCH == 0

    x = jnp.concatenate([x_prompt.reshape(m_p, D_MODEL), x_sample.reshape(m_s, D_MODEL)], axis=0)
    ones_bd = jnp.asarray(np.kron(np.eye(2), np.ones((HEAD_DIM, HEAD_DIM))), BF16)
    row = lambda v: v.reshape(1, -1)

    outs = {k: [] for k in ("kp", "vp", "wp", "sp", "ks", "vs", "ws", "ss")}
    for l in range(depth):
        w = w_in[l]
        o_pb = 3 * WIDTH
        o_ga = o_pb + 3 * WIDTH + DECAY_LORA + ICLR_LORA + GATE_LORA
        w_lora = jnp.concatenate([
            _pad_cols(w[:, o_pb + 3 * WIDTH:o_pb + 3 * WIDTH + DECAY_LORA], LORA_PAD),
            _pad_cols(w[:, o_pb + 3 * WIDTH + DECAY_LORA:o_pb + 3 * WIDTH + DECAY_LORA + ICLR_LORA], LORA_PAD),
            w[:, o_pb + 3 * WIDTH + DECAY_LORA + ICLR_LORA:o_ga]], axis=1)
        w_proj = jnp.concatenate([w[:, :o_pb + 3 * WIDTH], w[:, o_ga:], w_lora], axis=1).astype(BF16)
        mu_rkv, mu_lora = _reorder_shift_cols(rwkv_mu[l][None])
        prep_params = (
            mu_rkv, mu_lora, row(rwkv_w0[l]), row(rwkv_a0[l]), row(rwkv_k_k[l]), row(rwkv_k_a[l]),
            row(rwkv_r_k[l]), row(rwkv_ln_b[l]),
            *_split_bf16(_pad_rows(rwkv_w_up[l], LORA_PAD)),
            *_split_bf16(_pad_rows(rwkv_a_up[l], LORA_PAD)),
            *_split_bf16(rwkv_g_up[l]), ones_bd)

        x = _ffn(x, row(norm_ffn1_pre[l]), row(norm_ffn1_post[l]),
                 w_ffn1_in[l].astype(BF16), w_ffn1_down[l].astype(BF16))
        proj = _proj(x, row(norm_mix_pre[l]), w_proj)

        tabs = _bias_tables(rel_bias[l])
        att_p = _attn_prompt(proj, tabs, batch, seq)
        att_s = _attn_sample(proj, m_p, cache_k[l].reshape(dec_batch, n_past, WIDTH),
                             cache_v[l].reshape(dec_batch, n_past, WIDTH),
                             tabs[0, :, :dec_seq, :n_past], tabs[0, :, :dec_seq, n_past:n_past + dec_seq],
                             dec_batch, dec_seq)

        last_of_tile = proj[PREP_TM - 1:m_p:PREP_TM]
        starts_seq = (jnp.arange(m_p // PREP_TM) % (seq // PREP_TM) == 0)[:, None]
        first = jnp.where(starts_seq, 0.0, jnp.roll(last_of_tile, 1, axis=0))
        prep_p = _prep(proj, 0, m_p, PREP_TM, first[:, None, COL_RKV:COL_RKV + 3 * WIDTH],
                       first[:, None, COL_LORA:], prep_params)
        sh_rkv, sh_lora = _reorder_shift_cols(state_shift[l])
        prep_s = _prep(proj, m_p, m_s, dec_seq, sh_rkv, sh_lora, prep_params)

        def run_scan(prep, n_batch, t, s0, tb):
            r, wdec, k2, v, kk, kka = prep[:6]
            rows = [_rows_to_scan(a, n_batch, t) for a in (r, wdec, k2, kk, kka)]
            y, s_t = _scan(rows, _vals_to_scan(v, n_batch, t), _state_to_scan(s0, n_batch), tb)
            return _vals_from_scan(y, n_batch, t), _state_from_scan(s_t, n_batch)

        y_p, wkv_p = run_scan(prep_p, batch, seq, jnp.zeros((batch, HEADS, HEAD_DIM, HEAD_DIM), F32), SCAN_TB)
        y_s, wkv_s = run_scan(prep_s, dec_batch, dec_seq, state_wkv[l].astype(F32), dec_seq)

        cat = lambda a, b: jnp.concatenate([a, b], axis=0)
        x = _merge(cat(att_p, att_s), cat(y_p, y_s), cat(prep_p[6], prep_s[6]), cat(prep_p[7], prep_s[7]),
                   proj, x, row(rwkv_ln_w[l]), row(norm_mix_post[l]), ones_bd,
                   w_att_out[l].astype(BF16), w_rwkv_out[l].astype(BF16), w_out[l].astype(BF16))
        x = _ffn(x, row(norm_ffn2_pre[l]), row(norm_ffn2_post[l]),
                 w_ffn2_in[l].astype(BF16), w_ffn2_down[l].astype(BF16))

        proj_p = proj[:m_p].reshape(batch, seq, PROJ_COLS)
        proj_s = proj[m_p:].reshape(dec_batch, dec_seq, PROJ_COLS)
        heads = lambda a: a.reshape(a.shape[0], a.shape[1], HEADS, HEAD_DIM)
        outs["kp"].append(heads(proj_p[:, seq - keep:, COL_K:COL_K + WIDTH]))
        outs["vp"].append(heads(proj_p[:, seq - keep:, COL_V:COL_V + WIDTH]))
        outs["wp"].append(wkv_p.astype(state_wkv.dtype))
        outs["sp"].append(_shift_row_from_proj(proj_p[:, -1])[:, None])
        outs["ks"].append(heads(proj_s[:, :, COL_K:COL_K + WIDTH]))
        outs["vs"].append(heads(proj_s[:, :, COL_V:COL_V + WIDTH]))
        outs["ws"].append(wkv_s.astype(state_wkv.dtype))
        outs["ss"].append(_shift_row_from_proj(proj_s[:, -1])[:, None])

    y_prompt = x[:m_p].reshape(batch, seq, D_MODEL)
    y_sample = x[m_p:].reshape(dec_batch, dec_seq, D_MODEL)
    st = lambda k: jnp.stack(outs[k])
    return (y_prompt, y_sample, st("kp"), st("vp"), st("wp"), st("sp"),
            st("ks"), st("vs"), st("ws"), st("ss"))
```

```python
import functools

import numpy as np
import jax
import jax.numpy as jnp
from jax import lax
from jax.experimental import pallas as pl
from jax.experimental.pallas import tpu as pltpu

F32 = jnp.float32
BF16 = jnp.bfloat16

D_MODEL = 2048
CHUNK = 64
LEFT_CHUNKS = 8
PAST_BAND = LEFT_CHUNKS * CHUNK
BAND = PAST_BAND + CHUNK
HEAD_DIM = 64
HEADS = 16
WIDTH = HEADS * HEAD_DIM
DECAY_LORA = 96
ICLR_LORA = 96
GATE_LORA = 256
LORA_PAD = 128
LORA_COLS = 2 * LORA_PAD + GATE_LORA
D_FF = 5632
MAX_REL = 128
RMS_EPS = 1e-6
GN_EPS = 64e-5
NEG_INF = -1e30

COL_Q, COL_K, COL_V = 0, WIDTH, 2 * WIDTH
COL_RKV = 3 * WIDTH
COL_GA = 6 * WIDTH
COL_GB = COL_GA + D_MODEL
COL_LORA = COL_GB + D_MODEL
PROJ_COLS = COL_LORA + LORA_COLS

VMEM_LIMIT = 56 * 1024 * 1024

FFN_TM = 768
FFN_TF = 512
PROJ_TM = 768
PROJ_TN = 1536
MERGE_TM = 384
PREP_TM = 256
SCAN_TB = 64
ATT_Q = 2 * CHUNK
ATT_KEYS = PAST_BAND + ATT_Q
ATT_LEAD = PAST_BAND // ATT_Q
ATT_DIAG = ATT_KEYS + ATT_Q
LANE_GROUP_BATCH = 4
KLO = HEAD_DIM // 2


def _sigmoid(x):
    return 1.0 / (1.0 + jnp.exp(-x))


def _split_bf16(x):
    hi = x.astype(BF16)
    lo = (x - hi.astype(F32)).astype(BF16)
    return hi, lo


def _dot(a, b):
    return jnp.dot(a, b, preferred_element_type=F32)


def _dot_f32(a, b_hi, b_lo):
    a_hi, a_lo = _split_bf16(a)
    return _dot(a_hi, b_hi) + _dot(a_lo, b_hi) + _dot(a_hi, b_lo)


def _segsum(x, ones_bd):
    hi, lo = _split_bf16(x)
    outs = []
    for j in range(x.shape[-1] // 128):
        sl = slice(j * 128, (j + 1) * 128)
        outs.append(_dot(hi[:, sl], ones_bd) + _dot(lo[:, sl], ones_bd))
    return jnp.concatenate(outs, axis=-1)


def _rms_scale(x):
    return lax.rsqrt(jnp.mean(x * x, axis=-1, keepdims=True) + RMS_EPS)


def _ffn_body(x_ref, gpre_ref, gpost_ref, wg_ref, wu_ref, wd_ref, o_ref, h_ref, *, n_f):
    f = pl.program_id(1)

    @pl.when(f == 0)
    def _():
        x = x_ref[...]
        h_ref[...] = (x * _rms_scale(x) * gpre_ref[...]).astype(BF16)

    h = h_ref[...]
    g = _dot(h, wg_ref[...])
    u = _dot(h, wu_ref[...])
    act = (g * _sigmoid(g) * u).astype(BF16)
    part = _dot(act, wd_ref[...])

    @pl.when(f == 0)
    def _():
        o_ref[...] = part

    @pl.when(f > 0)
    def _():
        o_ref[...] += part

    @pl.when(f == n_f - 1)
    def _():
        y = o_ref[...]
        o_ref[...] = x_ref[...] + 0.5 * (y * _rms_scale(y) * gpost_ref[...])


def _ffn(x, g_pre, g_post, w_in, w_down):
    m = x.shape[0]
    n_m, n_f = m // FFN_TM, D_FF // FFN_TF
    return pl.pallas_call(
        functools.partial(_ffn_body, n_f=n_f),
        grid=(n_m, n_f),
        in_specs=[
            pl.BlockSpec((FFN_TM, D_MODEL), lambda i, f: (i, 0)),
            pl.BlockSpec((1, D_MODEL), lambda i, f: (0, 0)),
            pl.BlockSpec((1, D_MODEL), lambda i, f: (0, 0)),
            pl.BlockSpec((D_MODEL, FFN_TF), lambda i, f: (0, f)),
            pl.BlockSpec((D_MODEL, FFN_TF), lambda i, f: (0, f + n_f)),
            pl.BlockSpec((FFN_TF, D_MODEL), lambda i, f: (f, 0)),
        ],
        out_specs=pl.BlockSpec((FFN_TM, D_MODEL), lambda i, f: (i, 0)),
        out_shape=jax.ShapeDtypeStruct((m, D_MODEL), F32),
        scratch_shapes=[pltpu.VMEM((FFN_TM, D_MODEL), BF16)],
        compiler_params=pltpu.CompilerParams(
            dimension_semantics=("parallel", "arbitrary"), vmem_limit_bytes=VMEM_LIMIT),
        name="ffn",
    )(x, g_pre, g_post, w_in, w_in, w_down)


def _proj_body(x_ref, g_ref, w_ref, o_ref, h_ref):
    @pl.when(pl.program_id(1) == 0)
    def _():
        x = x_ref[...]
        h_ref[...] = (x * _rms_scale(x) * g_ref[...]).astype(BF16)

    o_ref[...] = _dot(h_ref[...], w_ref[...])


def _proj(x, g, w):
    m = x.shape[0]
    return pl.pallas_call(
        _proj_body,
        grid=(m // PROJ_TM, PROJ_COLS // PROJ_TN),
        in_specs=[
            pl.BlockSpec((PROJ_TM, D_MODEL), lambda i, n: (i, 0)),
            pl.BlockSpec((1, D_MODEL), lambda i, n: (0, 0)),
            pl.BlockSpec((D_MODEL, PROJ_TN), lambda i, n: (0, n)),
        ],
        out_specs=pl.BlockSpec((PROJ_TM, PROJ_TN), lambda i, n: (i, n)),
        out_shape=jax.ShapeDtypeStruct((m, PROJ_COLS), F32),
        scratch_shapes=[pltpu.VMEM((PROJ_TM, D_MODEL), BF16)],
        compiler_params=pltpu.CompilerParams(
            dimension_semantics=("parallel", "arbitrary"), vmem_limit_bytes=VMEM_LIMIT),
        name="proj",
    )(x, g, w)


def _softmax_pv(parts):
    m = functools.reduce(jnp.maximum, [jnp.max(s, axis=-1, keepdims=True) for s, _ in parts])
    num, den = 0.0, 0.0
    for s, v in parts:
        p = jnp.exp(s - m)
        den = den + jnp.sum(p, axis=-1, keepdims=True)
        num = num + _dot(p.astype(BF16), v)
    return num / den


def _qk(q, k):
    return lax.dot_general(q, k, (((1,), (1,)), ((), ())), preferred_element_type=F32)


def _attn_prompt_body(q_ref, k_ref, v_ref, diag_ref, o_ref, kbf_ref, vbf_ref, bias_ref):
    c = pl.program_id(1)
    shift = jnp.maximum(ATT_LEAD - c, 0)

    @pl.when(c == 0)
    def _():
        kbf_ref[...] = k_ref[...].astype(BF16)
        vbf_ref[...] = v_ref[...].astype(BF16)

    @pl.when(c <= ATT_LEAD)
    def _():
        key = lax.broadcasted_iota(jnp.int32, (ATT_KEYS, ATT_Q), 0) // CHUNK
        qry = lax.broadcasted_iota(jnp.int32, (ATT_KEYS, ATT_Q), 1) // CHUNK
        dist = (LEFT_CHUNKS + qry) - (key + shift * (ATT_Q // CHUNK))
        in_band = (dist >= 0) & (dist <= LEFT_CHUNKS)
        for h in range(HEADS):
            diag = jnp.broadcast_to(diag_ref[shift, pl.ds(h, 1), :], (ATT_Q, ATT_DIAG))
            toep = pltpu.roll(diag, ATT_DIAG - ATT_Q, axis=1, stride=1, stride_axis=0)
            bias_ref[h] = jnp.where(in_band, toep[:, :ATT_KEYS].T, NEG_INF)

    start = pl.multiple_of(jnp.maximum(c - ATT_LEAD, 0) * ATT_Q, ATT_Q)
    lane = lax.broadcasted_iota(jnp.int32, (ATT_Q, 128), 1)
    for pair in range(HEADS // 2):
        sl = slice(pair * 128, (pair + 1) * 128)
        kp = kbf_ref[pl.ds(start, ATT_KEYS), sl]
        vp = vbf_ref[pl.ds(start, ATT_KEYS), sl]
        qp = q_ref[:, sl] * (HEAD_DIM ** -0.5)
        halves = []
        for half in range(2):
            qm = jnp.where((lane >= HEAD_DIM) == bool(half), qp, 0.0).astype(BF16)
            s = _qk(kp, qm) + bias_ref[2 * pair + half]
            p = jnp.exp(s - jnp.max(s, axis=0, keepdims=True))
            p = (p * (1.0 / jnp.sum(p, axis=0, keepdims=True))).astype(BF16)
            halves.append(lax.dot_general(p, vp, (((0,), (0,)), ((), ())),
                                          preferred_element_type=F32))
        o_ref[:, sl] = jnp.where(lane < HEAD_DIM, halves[0], halves[1]).astype(BF16)


def _attn_prompt(proj, diag, batch, seq):
    n_c = seq // ATT_Q
    once = dict(pipeline_mode=pl.Buffered(1))
    return pl.pallas_call(
        _attn_prompt_body,
        grid=(batch, n_c),
        in_specs=[
            pl.BlockSpec((ATT_Q, WIDTH), lambda b, c: (b * n_c + c, COL_Q // WIDTH)),
            pl.BlockSpec((seq, WIDTH), lambda b, c: (b, COL_K // WIDTH), **once),
            pl.BlockSpec((seq, WIDTH), lambda b, c: (b, COL_V // WIDTH), **once),
            pl.BlockSpec(diag.shape, lambda b, c: (0, 0, 0)),
        ],
        out_specs=pl.BlockSpec((ATT_Q, WIDTH), lambda b, c: (b * n_c + c, 0)),
        out_shape=jax.ShapeDtypeStruct((batch * seq, WIDTH), BF16),
        scratch_shapes=[pltpu.VMEM((seq, WIDTH), BF16), pltpu.VMEM((seq, WIDTH), BF16),
                        pltpu.VMEM((HEADS, ATT_KEYS, ATT_Q), F32)],
        compiler_params=pltpu.CompilerParams(
            dimension_semantics=("arbitrary", "arbitrary"), vmem_limit_bytes=VMEM_LIMIT),
        name="attn_prompt",
    )(proj, proj, proj, diag)


def _attn_sample_body(q_ref, kn_ref, vn_ref, ck_ref, cv_ref, bias_c_ref, bias_n_ref, o_ref):
    q = (q_ref[...] * (HEAD_DIM ** -0.5)).astype(BF16)
    kn = kn_ref[...].astype(BF16)
    vn = vn_ref[...].astype(BF16)
    outs = []
    for h in range(HEADS):
        sl = slice(h * HEAD_DIM, (h + 1) * HEAD_DIM)
        qh = q[:, sl]
        ck = ck_ref[0, :, sl].astype(BF16)
        cv = cv_ref[0, :, sl].astype(BF16)
        s_c = _qk(qh, ck) + bias_c_ref[h]
        s_n = _qk(qh, kn[:, sl]) + bias_n_ref[h]
        outs.append(_softmax_pv([(s_c, cv), (s_n, vn[:, sl])]))
    o_ref[...] = jnp.concatenate(outs, axis=-1).astype(BF16)


def _attn_sample(proj, row0, cache_k, cache_v, bias_c, bias_n, batch, t):
    n_past = cache_k.shape[1]
    blk0 = row0 // t
    return pl.pallas_call(
        _attn_sample_body,
        grid=(batch,),
        in_specs=[
            pl.BlockSpec((t, WIDTH), lambda b: (blk0 + b, COL_Q // WIDTH)),
            pl.BlockSpec((t, WIDTH), lambda b: (blk0 + b, COL_K // WIDTH)),
            pl.BlockSpec((t, WIDTH), lambda b: (blk0 + b, COL_V // WIDTH)),
            pl.BlockSpec((1, n_past, WIDTH), lambda b: (b, 0, 0)),
            pl.BlockSpec((1, n_past, WIDTH), lambda b: (b, 0, 0)),
            pl.BlockSpec((HEADS, t, n_past), lambda b: (0, 0, 0)),
            pl.BlockSpec((HEADS, t, t), lambda b: (0, 0, 0)),
        ],
        out_specs=pl.BlockSpec((t, WIDTH), lambda b: (b, 0)),
        out_shape=jax.ShapeDtypeStruct((batch * t, WIDTH), BF16),
        compiler_params=pltpu.CompilerParams(
            dimension_semantics=("parallel",), vmem_limit_bytes=VMEM_LIMIT),
        name="attn_sample",
    )(proj, proj, proj, cache_k, cache_v, bias_c, bias_n)


def _prep_body(p_ref, pl_ref, bp_ref, bl_ref, mu_ref, mul_ref, w0_ref, a0_ref, kk_ref, ka_ref, rk_ref,
               lnb_ref, wup_hi, wup_lo, aup_hi, aup_lo, gup_hi, gup_lo, bd_ref,
               r_o, w_o, k_o, v_o, kk_o, kka_o, g_o, cb_o):
    def shifted(p, first_row, mu):
        row = lax.broadcasted_iota(jnp.int32, p.shape, 0)
        prev = jnp.where(row == 0, first_row, pltpu.roll(p, 1, axis=0))
        return p + (prev - p) * mu

    xm = shifted(p_ref[...], bp_ref[0], mu_ref[...])
    xl = shifted(pl_ref[...], bl_ref[0], mul_ref[...])
    r = xm[:, 0:WIDTH]
    k = xm[:, WIDTH:2 * WIDTH]
    v = xm[:, 2 * WIDTH:3 * WIDTH]
    wd = xl[:, 0:LORA_PAD]
    ad = xl[:, LORA_PAD:2 * LORA_PAD]
    gd = xl[:, 2 * LORA_PAD:]

    z = -(w0_ref[...] + _dot_f32(jnp.tanh(wd), wup_hi[...], wup_lo[...]))
    softplus = jnp.maximum(z, 0.0) + jnp.log(1.0 + jnp.exp(-jnp.abs(z)))
    decay = jnp.exp(-jnp.exp(-softplus - 0.5))
    a = _sigmoid(a0_ref[...] + _dot_f32(ad, aup_hi[...], aup_lo[...]))
    g = _dot_f32(_sigmoid(gd), gup_hi[...], gup_lo[...])

    bd = bd_ref[...]
    kk = k * kk_ref[...]
    kk = kk / jnp.maximum(jnp.sqrt(_segsum(kk * kk, bd)), 1e-12)
    k2 = k * (1.0 + (a - 1.0) * ka_ref[...])
    bonus = _segsum(r * k2 * rk_ref[...], bd) * v

    r_o[...] = r
    w_o[...] = decay
    k_o[...] = k2
    v_o[...] = v
    kk_o[...] = kk
    kka_o[...] = kk * a
    g_o[...] = g
    cb_o[...] = lnb_ref[...] + bonus


def _prep(proj, row0, n_rows, tm, first_rkv, first_lora, params):
    n_t = n_rows // tm
    blk0 = row0 // tm
    vec = lambda w: pl.BlockSpec((1, w), lambda i: (0, 0))
    mat = lambda r: pl.BlockSpec((r, WIDTH), lambda i: (0, 0))
    out = pl.BlockSpec((tm, WIDTH), lambda i: (i, 0))
    return pl.pallas_call(
        _prep_body,
        grid=(n_t,),
        in_specs=[
            pl.BlockSpec((tm, 3 * WIDTH), lambda i: (blk0 + i, COL_RKV // (3 * WIDTH))),
            pl.BlockSpec((tm, LORA_COLS), lambda i: (blk0 + i, COL_LORA // LORA_COLS)),
            pl.BlockSpec((1, 1, 3 * WIDTH), lambda i: (i, 0, 0)),
            pl.BlockSpec((1, 1, LORA_COLS), lambda i: (i, 0, 0)),
            vec(3 * WIDTH), vec(LORA_COLS), vec(WIDTH), vec(WIDTH), vec(WIDTH), vec(WIDTH), vec(WIDTH),
            vec(WIDTH), mat(LORA_PAD), mat(LORA_PAD), mat(LORA_PAD), mat(LORA_PAD),
            mat(GATE_LORA), mat(GATE_LORA),
            pl.BlockSpec((128, 128), lambda i: (0, 0)),
        ],
        out_specs=[out] * 8,
        out_shape=[jax.ShapeDtypeStruct((n_rows, WIDTH), F32)] * 8,
        compiler_params=pltpu.CompilerParams(
            dimension_semantics=("parallel",), vmem_limit_bytes=VMEM_LIMIT),
        name="rwkv_prep",
    )(proj, proj, first_rkv, first_lora, *params)


def _scan_body(r_ref, w_ref, k_ref, kk_ref, kka_ref, v_ref, s0_ref, y_ref, st_ref, s_ref, rows_ref, *, tb):
    operands = (w_ref, kka_ref, k_ref, r_ref, kk_ref)
    W, KKA, K, R, KK = range(5)

    @pl.when(pl.program_id(1) == 0)
    def _():
        s_ref[...] = s0_ref[0]

    lane = lax.broadcasted_iota(jnp.int32, (tb * KLO, 128), 1)
    for a, ref in enumerate(operands):
        x = ref[0].reshape(tb * KLO, 128)
        swapped = pltpu.roll(x, HEAD_DIM, axis=1)
        rows_ref[a, :, 0:KLO, :] = jnp.where(lane < HEAD_DIM, x, swapped).reshape(tb, KLO, 128)
        rows_ref[a, :, KLO:, :] = jnp.where(lane < HEAD_DIM, swapped, x).reshape(tb, KLO, 128)

    def row(a, t, k):
        return jnp.broadcast_to(rows_ref[a, t, pl.ds(k, 1), :], (KLO, 128))

    def minus_s_dot(t):
        acc = [jnp.zeros((KLO, 128), F32), jnp.zeros((KLO, 128), F32)]
        for k in range(HEAD_DIM):
            acc[k % 2] = acc[k % 2] - s_ref[k] * row(KK, t, k)
        return acc[0] + acc[1]

    def step(t, sa):
        v = v_ref[0, t]
        t_next = jnp.minimum(t + 1, tb - 1)
        y = [jnp.zeros((KLO, 128), F32), jnp.zeros((KLO, 128), F32)]
        sa_next = [jnp.zeros((KLO, 128), F32), jnp.zeros((KLO, 128), F32)]
        for k in range(HEAD_DIM):
            s_new = s_ref[k] * row(W, t, k) + sa * row(KKA, t, k) + v * row(K, t, k)
            s_ref[k] = s_new
            y[k % 2] = y[k % 2] + s_new * row(R, t, k)
            sa_next[k % 2] = sa_next[k % 2] - s_new * row(KK, t_next, k)
        y_ref[0, t] = y[0] + y[1]
        return sa_next[0] + sa_next[1]

    lax.fori_loop(0, tb, step, minus_s_dot(0))

    @pl.when(pl.program_id(1) == pl.num_programs(1) - 1)
    def _():
        st_ref[0] = s_ref[...]


def _scan(rows, v, s0, tb):
    g, t = v.shape[0], v.shape[1]
    row_spec = pl.BlockSpec((1, tb, KLO, 128), lambda i, j: (i, j, 0, 0))
    st_spec = pl.BlockSpec((1, HEAD_DIM, KLO, 128), lambda i, j: (i, 0, 0, 0))
    return pl.pallas_call(
        functools.partial(_scan_body, tb=tb),
        grid=(g, t // tb),
        in_specs=[row_spec] * 6 + [st_spec],
        out_specs=[row_spec, st_spec],
        out_shape=[jax.ShapeDtypeStruct((g, t, KLO, 128), F32),
                   jax.ShapeDtypeStruct((g, HEAD_DIM, KLO, 128), F32)],
        scratch_shapes=[pltpu.VMEM((HEAD_DIM, KLO, 128), F32),
                        pltpu.VMEM((5, tb, HEAD_DIM, 128), F32)],
        compiler_params=pltpu.CompilerParams(
            dimension_semantics=("parallel", "arbitrary"), vmem_limit_bytes=VMEM_LIMIT),
        name="rwkv_scan",
    )(*rows, v, s0)


def _rows_to_scan(x, n_batch, t):
    g = n_batch // LANE_GROUP_BATCH
    x = x.reshape(g, LANE_GROUP_BATCH, t, HEADS, 2, KLO)
    return x.transpose(0, 2, 5, 4, 1, 3).reshape(g, t, KLO, 128)


def _rows_from_scan(y, n_batch, t):
    g = n_batch // LANE_GROUP_BATCH
    y = y.reshape(g, t, KLO, 2, LANE_GROUP_BATCH, HEADS).transpose(0, 4, 1, 5, 3, 2)
    return y.reshape(n_batch * t, WIDTH)


def _state_to_scan(s, n_batch):
    g = n_batch // LANE_GROUP_BATCH
    s = s.reshape(g, LANE_GROUP_BATCH, HEADS, 2, KLO, HEAD_DIM)
    return s.transpose(0, 5, 4, 3, 1, 2).reshape(g, HEAD_DIM, KLO, 128)


def _state_from_scan(s, n_batch):
    g = n_batch // LANE_GROUP_BATCH
    s = s.reshape(g, HEAD_DIM, KLO, 2, LANE_GROUP_BATCH, HEADS)
    return s.transpose(0, 4, 5, 3, 2, 1).reshape(n_batch, HEADS, HEAD_DIM, HEAD_DIM)


def _merge_body(att_ref, y_ref, g_ref, cb_ref, ga_ref, gb_ref, x_ref, lnw_ref, gpost_ref, bd_ref,
                wa_ref, wr_ref, wo_ref, o_ref):
    bd = bd_ref[...]
    y = y_ref[...]
    d = y - _segsum(y, bd) * (1.0 / HEAD_DIM)
    var = _segsum(d * d, bd) * (1.0 / HEAD_DIM)
    rw = ((d * lax.rsqrt(var + GN_EPS) * lnw_ref[...] + cb_ref[...]) * g_ref[...]).astype(BF16)
    rw_o = _dot(rw, wr_ref[...])
    att_o = _dot(att_ref[...], wa_ref[...])
    mixed = (_sigmoid(ga_ref[...]) * att_o + _sigmoid(gb_ref[...]) * rw_o).astype(BF16)
    z = _dot(mixed, wo_ref[...])
    o_ref[...] = x_ref[...] + z * _rms_scale(z) * gpost_ref[...]


def _merge(att, y, g, cb, proj, x, ln_w, g_post, ones_bd, w_att, w_rwkv, w_out):
    m = x.shape[0]
    tm = MERGE_TM
    tok = lambda w: pl.BlockSpec((tm, w), lambda i: (i, 0))
    const = lambda a: pl.BlockSpec(a.shape, lambda i: (0,) * a.ndim)
    return pl.pallas_call(
        _merge_body,
        grid=(m // tm,),
        in_specs=[
            tok(WIDTH), tok(WIDTH), tok(WIDTH), tok(WIDTH),
            pl.BlockSpec((tm, D_MODEL), lambda i: (i, COL_GA // D_MODEL)),
            pl.BlockSpec((tm, D_MODEL), lambda i: (i, COL_GB // D_MODEL)),
            tok(D_MODEL), const(ln_w), const(g_post), const(ones_bd),
            const(w_att), const(w_rwkv), const(w_out),
        ],
        out_specs=tok(D_MODEL),
        out_shape=jax.ShapeDtypeStruct((m, D_MODEL), F32),
        compiler_params=pltpu.CompilerParams(
            dimension_semantics=("parallel",), vmem_limit_bytes=VMEM_LIMIT),
        name="merge",
    )(att, y, g, cb, proj, proj, x, ln_w, g_post, ones_bd, w_att, w_rwkv, w_out)


def _pad_cols(x, width):
    return jnp.pad(x, ((0, 0), (0, width - x.shape[1])))


def _pad_rows(x, height):
    return jnp.pad(x, ((0, height - x.shape[0]), (0, 0)))


def _reorder_shift_cols(x):
    o = 3 * WIDTH
    wd = x[..., o:o + DECAY_LORA]
    ad = x[..., o + DECAY_LORA:o + DECAY_LORA + ICLR_LORA]
    gd = x[..., o + DECAY_LORA + ICLR_LORA:]
    pad = [(0, 0)] * (x.ndim - 1)
    lora = jnp.concatenate([jnp.pad(wd, pad + [(0, LORA_PAD - DECAY_LORA)]),
                            jnp.pad(ad, pad + [(0, LORA_PAD - ICLR_LORA)]), gd], axis=-1)
    return x[..., :o], lora


def _shift_row_from_proj(rows):
    lo = rows[:, COL_LORA:]
    return jnp.concatenate([rows[:, COL_RKV:COL_RKV + 3 * WIDTH], lo[:, :DECAY_LORA],
                            lo[:, LORA_PAD:LORA_PAD + ICLR_LORA], lo[:, 2 * LORA_PAD:]], axis=-1)


def _bias_diagonals(rel_bias):
    u = np.arange(ATT_DIAG)[None, :]
    s = np.arange(ATT_LEAD + 1)[:, None]
    idx = np.clip(ATT_KEYS - u - ATT_Q * s, -MAX_REL, MAX_REL) + MAX_REL
    tab = jnp.take(rel_bias, jnp.asarray(idx.reshape(-1), jnp.int32), axis=1)
    return tab.reshape(HEADS, ATT_LEAD + 1, ATT_DIAG).transpose(1, 0, 2)


def _bias_sample(rel_bias, t, n_past):
    rel = np.arange(t)[:, None] + n_past - np.arange(n_past + t)[None, :]
    idx = np.clip(rel, -MAX_REL, MAX_REL) + MAX_REL
    tab = jnp.take(rel_bias, jnp.asarray(idx.reshape(-1), jnp.int32), axis=1).reshape(HEADS, t, n_past + t)
    return tab[:, :, :n_past], tab[:, :, n_past:]


def kernel(x_prompt, x_sample, cache_k, cache_v, state_wkv, state_shift, norm_ffn1_pre, norm_ffn1_post, w_ffn1_in, w_ffn1_down, norm_mix_pre, norm_mix_post, w_in, rel_bias, w_att_out, rwkv_mu, rwkv_w0, rwkv_w_up, rwkv_a0, rwkv_a_up, rwkv_g_up, rwkv_k_k, rwkv_k_a, rwkv_r_k, rwkv_ln_w, rwkv_ln_b, w_rwkv_out, w_out, norm_ffn2_pre, norm_ffn2_post, w_ffn2_in, w_ffn2_down):
    depth = w_in.shape[0]
    batch, seq, _ = x_prompt.shape
    dec_batch, dec_seq, _ = x_sample.shape
    n_past = cache_k.shape[2]
    m_p, m_s = batch * seq, dec_batch * dec_seq
    keep = min(PAST_BAND, seq)
    assert n_past == PAST_BAND and seq % CHUNK == 0 and seq // CHUNK > LEFT_CHUNKS
    assert (m_p + m_s) % FFN_TM == 0 and (m_p + m_s) % MERGE_TM == 0 and seq % PREP_TM == 0
    assert batch == LANE_GROUP_BATCH and dec_batch % LANE_GROUP_BATCH == 0

    x = jnp.concatenate([x_prompt.reshape(m_p, D_MODEL), x_sample.reshape(m_s, D_MODEL)], axis=0)
    ones_bd = jnp.asarray(np.kron(np.eye(2), np.ones((HEAD_DIM, HEAD_DIM))), BF16)
    row = lambda v: v.reshape(1, -1)

    outs = {k: [] for k in ("kp", "vp", "wp", "sp", "ks", "vs", "ws", "ss")}
    for l in range(depth):
        w = w_in[l]
        o_pb = 3 * WIDTH
        o_ga = o_pb + 3 * WIDTH + DECAY_LORA + ICLR_LORA + GATE_LORA
        w_lora = jnp.concatenate([
            _pad_cols(w[:, o_pb + 3 * WIDTH:o_pb + 3 * WIDTH + DECAY_LORA], LORA_PAD),
            _pad_cols(w[:, o_pb + 3 * WIDTH + DECAY_LORA:o_pb + 3 * WIDTH + DECAY_LORA + ICLR_LORA], LORA_PAD),
            w[:, o_pb + 3 * WIDTH + DECAY_LORA + ICLR_LORA:o_ga]], axis=1)
        w_proj = jnp.concatenate([w[:, :o_pb + 3 * WIDTH], w[:, o_ga:], w_lora], axis=1).astype(BF16)
        mu_rkv, mu_lora = _reorder_shift_cols(rwkv_mu[l][None])
        prep_params = (
            mu_rkv, mu_lora, row(rwkv_w0[l]), row(rwkv_a0[l]), row(rwkv_k_k[l]), row(rwkv_k_a[l]),
            row(rwkv_r_k[l]), row(rwkv_ln_b[l]),
            *_split_bf16(_pad_rows(rwkv_w_up[l], LORA_PAD)),
            *_split_bf16(_pad_rows(rwkv_a_up[l], LORA_PAD)),
            *_split_bf16(rwkv_g_up[l]), ones_bd)

        x = _ffn(x, row(norm_ffn1_pre[l]), row(norm_ffn1_post[l]),
                 w_ffn1_in[l].astype(BF16), w_ffn1_down[l].astype(BF16))
        proj = _proj(x, row(norm_mix_pre[l]), w_proj)

        att_p = _attn_prompt(proj, _bias_diagonals(rel_bias[l]), batch, seq)
        att_s = _attn_sample(proj, m_p, cache_k[l].reshape(dec_batch, n_past, WIDTH),
                             cache_v[l].reshape(dec_batch, n_past, WIDTH),
                             *_bias_sample(rel_bias[l], dec_seq, n_past), dec_batch, dec_seq)

        starts_seq = (jnp.arange(m_p // PREP_TM) % (seq // PREP_TM) == 0)[:, None]

        def first_rows(col, width):
            last_of_tile = proj[PREP_TM - 1:m_p:PREP_TM, col:col + width]
            return jnp.where(starts_seq, 0.0, jnp.roll(last_of_tile, 1, axis=0))[:, None]

        prep_p = _prep(proj, 0, m_p, PREP_TM, first_rows(COL_RKV, 3 * WIDTH),
                       first_rows(COL_LORA, LORA_COLS), prep_params)
        sh_rkv, sh_lora = _reorder_shift_cols(state_shift[l])
        prep_s = _prep(proj, m_p, m_s, dec_seq, sh_rkv, sh_lora, prep_params)

        def run_scan(prep, n_batch, t, s0, tb):
            r, wdec, k2, v, kk, kka = prep[:6]
            rows = [_rows_to_scan(a, n_batch, t) for a in (r, wdec, k2, kk, kka)]
            y, s_t = _scan(rows, _rows_to_scan(v, n_batch, t), _state_to_scan(s0, n_batch), tb)
            return _rows_from_scan(y, n_batch, t), _state_from_scan(s_t, n_batch)

        y_p, wkv_p = run_scan(prep_p, batch, seq, jnp.zeros((batch, HEADS, HEAD_DIM, HEAD_DIM), F32), SCAN_TB)
        y_s, wkv_s = run_scan(prep_s, dec_batch, dec_seq, state_wkv[l].astype(F32), dec_seq)

        cat = lambda a, b: jnp.concatenate([a, b], axis=0)
        x = _merge(cat(att_p, att_s), cat(y_p, y_s), cat(prep_p[6], prep_s[6]), cat(prep_p[7], prep_s[7]),
                   proj, x, row(rwkv_ln_w[l]), row(norm_mix_post[l]), ones_bd,
                   w_att_out[l].astype(BF16), w_rwkv_out[l].astype(BF16), w_out[l].astype(BF16))
        x = _ffn(x, row(norm_ffn2_pre[l]), row(norm_ffn2_post[l]),
                 w_ffn2_in[l].astype(BF16), w_ffn2_down[l].astype(BF16))

        def tail_rows(col, n_batch, t, n_keep, row0):
            if n_keep == t:
                rows = proj[row0:row0 + n_batch * t, col:col + WIDTH]
            else:
                rows = jnp.stack([proj[row0 + (b + 1) * t - n_keep:row0 + (b + 1) * t, col:col + WIDTH]
                                  for b in range(n_batch)])
            return rows.reshape(n_batch, n_keep, HEADS, HEAD_DIM)

        def last_rows(n_batch, t, row0):
            rows = jnp.concatenate([proj[row0 + (b + 1) * t - 1:row0 + (b + 1) * t] for b in range(n_batch)])
            return _shift_row_from_proj(rows)[:, None]

        outs["kp"].append(tail_rows(COL_K, batch, seq, keep, 0))
        outs["vp"].append(tail_rows(COL_V, batch, seq, keep, 0))
        outs["wp"].append(wkv_p.astype(state_wkv.dtype))
        outs["sp"].append(last_rows(batch, seq, 0))
        outs["ks"].append(tail_rows(COL_K, dec_batch, dec_seq, dec_seq, m_p))
        outs["vs"].append(tail_rows(COL_V, dec_batch, dec_seq, dec_seq, m_p))
        outs["ws"].append(wkv_s.astype(state_wkv.dtype))
        outs["ss"].append(last_rows(dec_batch, dec_seq, m_p))

    y_prompt = x[:m_p].reshape(batch, seq, D_MODEL)
    y_sample = x[m_p:].reshape(dec_batch, dec_seq, D_MODEL)
    st = lambda k: jnp.stack(outs[k])
    return (y_prompt, y_sample, st("kp"), st("vp"), st("wp"), st("sp"),
            st("ks"), st("vs"), st("ws"), st("ss"))
```

```python
import functools

import numpy as np
import jax
import jax.numpy as jnp
from jax import lax
from jax.experimental import pallas as pl
from jax.experimental.pallas import tpu as pltpu

F32 = jnp.float32
BF16 = jnp.bfloat16

D_MODEL = 2048
CHUNK = 64
LEFT_CHUNKS = 8
PAST_BAND = LEFT_CHUNKS * CHUNK
BAND = PAST_BAND + CHUNK
HEAD_DIM = 64
HEADS = 16
WIDTH = HEADS * HEAD_DIM
DECAY_LORA = 96
ICLR_LORA = 96
GATE_LORA = 256
LORA_PAD = 128
LORA_COLS = 2 * LORA_PAD + GATE_LORA
D_FF = 5632
MAX_REL = 128
RMS_EPS = 1e-6
GN_EPS = 64e-5
NEG_INF = -1e30

COL_Q, COL_K, COL_V = 0, WIDTH, 2 * WIDTH
COL_RKV = 3 * WIDTH
COL_GA = 6 * WIDTH
COL_GB = COL_GA + D_MODEL
COL_LORA = COL_GB + D_MODEL
PROJ_COLS = COL_LORA + LORA_COLS

VMEM_LIMIT = 56 * 1024 * 1024

FFN_TM = 512
FFN_TF = 512
PROJ_TM = 1024
PROJ_TN = 1536
MERGE_TM = 256
PREP_TM = 256
SCAN_TB = 64
ATT_Q = 2 * CHUNK
ATT_KEYS = PAST_BAND + ATT_Q
ATT_LEAD = PAST_BAND // ATT_Q
ATT_DIAG = ATT_KEYS + ATT_Q
LANE_GROUP_BATCH = 4
KLO = HEAD_DIM // 2


def _sigmoid(x):
    return 1.0 / (1.0 + jnp.exp(-x))


def _split_bf16(x):
    hi = x.astype(BF16)
    lo = (x - hi.astype(F32)).astype(BF16)
    return hi, lo


def _dot(a, b):
    return jnp.dot(a, b, preferred_element_type=F32)


def _dot_f32(a, b_hi, b_lo):
    a_hi, a_lo = _split_bf16(a)
    return _dot(a_hi, b_hi) + _dot(a_lo, b_hi) + _dot(a_hi, b_lo)


def _segsum(x, ones_bd):
    hi, lo = _split_bf16(x)
    outs = []
    for j in range(x.shape[-1] // 128):
        sl = slice(j * 128, (j + 1) * 128)
        outs.append(_dot(hi[:, sl], ones_bd) + _dot(lo[:, sl], ones_bd))
    return jnp.concatenate(outs, axis=-1)


def _rms_scale(x):
    return lax.rsqrt(jnp.mean(x * x, axis=-1, keepdims=True) + RMS_EPS)


def _ffn_body(x_ref, gpre_ref, gpost_ref, wg_ref, wu_ref, wd_ref, o_ref, h_ref, *, n_f):
    f = pl.program_id(1)

    @pl.when(f == 0)
    def _():
        x = x_ref[...]
        h_ref[...] = (x * _rms_scale(x) * gpre_ref[...]).astype(BF16)
        o_ref[...] = jnp.zeros_like(o_ref)

    h = h_ref[...]
    g = _dot(h, wg_ref[...])
    u = _dot(h, wu_ref[...])
    act = (g * _sigmoid(g) * u).astype(BF16)
    o_ref[...] += _dot(act, wd_ref[...])

    @pl.when(f == n_f - 1)
    def _():
        y = o_ref[...]
        o_ref[...] = x_ref[...] + 0.5 * (y * _rms_scale(y) * gpost_ref[...])


def _ffn(x, g_pre, g_post, w_in, w_down):
    m = x.shape[0]
    tm = min(FFN_TM, m)
    n_m, n_f = m // tm, D_FF // FFN_TF
    return pl.pallas_call(
        functools.partial(_ffn_body, n_f=n_f),
        grid=(n_m, n_f),
        in_specs=[
            pl.BlockSpec((tm, D_MODEL), lambda i, f: (i, 0)),
            pl.BlockSpec((1, D_MODEL), lambda i, f: (0, 0)),
            pl.BlockSpec((1, D_MODEL), lambda i, f: (0, 0)),
            pl.BlockSpec((D_MODEL, FFN_TF), lambda i, f: (0, f)),
            pl.BlockSpec((D_MODEL, FFN_TF), lambda i, f: (0, f + n_f)),
            pl.BlockSpec((FFN_TF, D_MODEL), lambda i, f: (f, 0)),
        ],
        out_specs=pl.BlockSpec((tm, D_MODEL), lambda i, f: (i, 0)),
        out_shape=jax.ShapeDtypeStruct((m, D_MODEL), F32),
        scratch_shapes=[pltpu.VMEM((tm, D_MODEL), BF16)],
        compiler_params=pltpu.CompilerParams(
            dimension_semantics=("parallel", "arbitrary"), vmem_limit_bytes=VMEM_LIMIT),
        name="ffn",
    )(x, g_pre, g_post, w_in, w_in, w_down)


def _proj_body(x_ref, g_ref, w_ref, o_ref, h_ref):
    @pl.when(pl.program_id(1) == 0)
    def _():
        x = x_ref[...]
        h_ref[...] = (x * _rms_scale(x) * g_ref[...]).astype(BF16)

    o_ref[...] = _dot(h_ref[...], w_ref[...])


def _proj(x, g, w):
    m = x.shape[0]
    tm = min(PROJ_TM, m)
    return pl.pallas_call(
        _proj_body,
        grid=(m // tm, PROJ_COLS // PROJ_TN),
        in_specs=[
            pl.BlockSpec((tm, D_MODEL), lambda i, n: (i, 0)),
            pl.BlockSpec((1, D_MODEL), lambda i, n: (0, 0)),
            pl.BlockSpec((D_MODEL, PROJ_TN), lambda i, n: (0, n)),
        ],
        out_specs=pl.BlockSpec((tm, PROJ_TN), lambda i, n: (i, n)),
        out_shape=jax.ShapeDtypeStruct((m, PROJ_COLS), F32),
        scratch_shapes=[pltpu.VMEM((tm, D_MODEL), BF16)],
        compiler_params=pltpu.CompilerParams(
            dimension_semantics=("parallel", "arbitrary"), vmem_limit_bytes=VMEM_LIMIT),
        name="proj",
    )(x, g, w)


def _softmax_pv(parts):
    m = functools.reduce(jnp.maximum, [jnp.max(s, axis=-1, keepdims=True) for s, _ in parts])
    num, den = 0.0, 0.0
    for s, v in parts:
        p = jnp.exp(s - m)
        den = den + jnp.sum(p, axis=-1, keepdims=True)
        num = num + _dot(p.astype(BF16), v)
    return num / den


def _qk(q, k):
    return lax.dot_general(q, k, (((1,), (1,)), ((), ())), preferred_element_type=F32)


def _attn_prompt_body(q_ref, k_ref, v_ref, diag_ref, o_ref, kbf_ref, vbf_ref, bias_ref):
    c = pl.program_id(1)
    shift = jnp.maximum(ATT_LEAD - c, 0)

    @pl.when(c == 0)
    def _():
        kbf_ref[...] = k_ref[...].astype(BF16)
        vbf_ref[...] = v_ref[...].astype(BF16)

    @pl.when(c <= ATT_LEAD)
    def _():
        key = lax.broadcasted_iota(jnp.int32, (ATT_KEYS, ATT_Q), 0) // CHUNK
        qry = lax.broadcasted_iota(jnp.int32, (ATT_KEYS, ATT_Q), 1) // CHUNK
        dist = (LEFT_CHUNKS + qry) - (key + shift * (ATT_Q // CHUNK))
        in_band = (dist >= 0) & (dist <= LEFT_CHUNKS)
        for h in range(HEADS):
            diag = jnp.broadcast_to(diag_ref[shift, pl.ds(h, 1), :], (ATT_Q, ATT_DIAG))
            toep = pltpu.roll(diag, ATT_DIAG - ATT_Q, axis=1, stride=1, stride_axis=0)
            bias_ref[h] = jnp.where(in_band, toep[:, :ATT_KEYS].T, NEG_INF)

    start = pl.multiple_of(jnp.maximum(c - ATT_LEAD, 0) * ATT_Q, ATT_Q)
    lane = lax.broadcasted_iota(jnp.int32, (ATT_Q, 128), 1)
    for pair in range(HEADS // 2):
        sl = slice(pair * 128, (pair + 1) * 128)
        kp = kbf_ref[pl.ds(start, ATT_KEYS), sl]
        vp = vbf_ref[pl.ds(start, ATT_KEYS), sl]
        qp = q_ref[:, sl] * (HEAD_DIM ** -0.5)
        halves = []
        for half in range(2):
            qm = jnp.where((lane >= HEAD_DIM) == bool(half), qp, 0.0).astype(BF16)
            s = _qk(kp, qm) + bias_ref[2 * pair + half]
            p = jnp.exp(s - jnp.max(s, axis=0, keepdims=True))
            p = (p * (1.0 / jnp.sum(p, axis=0, keepdims=True))).astype(BF16)
            halves.append(lax.dot_general(p, vp, (((0,), (0,)), ((), ())),
                                          preferred_element_type=F32))
        o_ref[:, sl] = jnp.where(lane < HEAD_DIM, halves[0], halves[1]).astype(BF16)


def _attn_prompt(proj, diag, batch, seq):
    n_c = seq // ATT_Q
    once = dict(pipeline_mode=pl.Buffered(1))
    return pl.pallas_call(
        _attn_prompt_body,
        grid=(batch, n_c),
        in_specs=[
            pl.BlockSpec((ATT_Q, WIDTH), lambda b, c: (b * n_c + c, COL_Q // WIDTH)),
            pl.BlockSpec((seq, WIDTH), lambda b, c: (b, COL_K // WIDTH), **once),
            pl.BlockSpec((seq, WIDTH), lambda b, c: (b, COL_V // WIDTH), **once),
            pl.BlockSpec(diag.shape, lambda b, c: (0, 0, 0)),
        ],
        out_specs=pl.BlockSpec((ATT_Q, WIDTH), lambda b, c: (b * n_c + c, 0)),
        out_shape=jax.ShapeDtypeStruct((batch * seq, WIDTH), BF16),
        scratch_shapes=[pltpu.VMEM((seq, WIDTH), BF16), pltpu.VMEM((seq, WIDTH), BF16),
                        pltpu.VMEM((HEADS, ATT_KEYS, ATT_Q), F32)],
        compiler_params=pltpu.CompilerParams(
            dimension_semantics=("arbitrary", "arbitrary"), vmem_limit_bytes=VMEM_LIMIT),
        name="attn_prompt",
    )(proj, proj, proj, diag)


def _attn_sample_body(q_ref, kn_ref, vn_ref, ck_ref, cv_ref, bias_c_ref, bias_n_ref, o_ref):
    q = (q_ref[...] * (HEAD_DIM ** -0.5)).astype(BF16)
    kn = kn_ref[...].astype(BF16)
    vn = vn_ref[...].astype(BF16)
    outs = []
    for h in range(HEADS):
        sl = slice(h * HEAD_DIM, (h + 1) * HEAD_DIM)
        qh = q[:, sl]
        ck = ck_ref[0, :, sl].astype(BF16)
        cv = cv_ref[0, :, sl].astype(BF16)
        s_c = _qk(qh, ck) + bias_c_ref[h]
        s_n = _qk(qh, kn[:, sl]) + bias_n_ref[h]
        outs.append(_softmax_pv([(s_c, cv), (s_n, vn[:, sl])]))
    o_ref[...] = jnp.concatenate(outs, axis=-1).astype(BF16)


def _attn_sample(proj, row0, cache_k, cache_v, bias_c, bias_n, batch, t):
    n_past = cache_k.shape[1]
    blk0 = row0 // t
    return pl.pallas_call(
        _attn_sample_body,
        grid=(batch,),
        in_specs=[
            pl.BlockSpec((t, WIDTH), lambda b: (blk0 + b, COL_Q // WIDTH)),
            pl.BlockSpec((t, WIDTH), lambda b: (blk0 + b, COL_K // WIDTH)),
            pl.BlockSpec((t, WIDTH), lambda b: (blk0 + b, COL_V // WIDTH)),
            pl.BlockSpec((1, n_past, WIDTH), lambda b: (b, 0, 0)),
            pl.BlockSpec((1, n_past, WIDTH), lambda b: (b, 0, 0)),
            pl.BlockSpec((HEADS, t, n_past), lambda b: (0, 0, 0)),
            pl.BlockSpec((HEADS, t, t), lambda b: (0, 0, 0)),
        ],
        out_specs=pl.BlockSpec((t, WIDTH), lambda b: (b, 0)),
        out_shape=jax.ShapeDtypeStruct((batch * t, WIDTH), BF16),
        compiler_params=pltpu.CompilerParams(
            dimension_semantics=("parallel",), vmem_limit_bytes=VMEM_LIMIT),
        name="attn_sample",
    )(proj, proj, proj, cache_k, cache_v, bias_c, bias_n)


def _prep_body(p_ref, pl_ref, bp_ref, bl_ref, mu_ref, mul_ref, w0_ref, a0_ref, kk_ref, ka_ref, rk_ref,
               lnb_ref, wup_hi, wup_lo, aup_hi, aup_lo, gup_hi, gup_lo, bd_ref,
               r_o, w_o, k_o, v_o, kk_o, kka_o, g_o, cb_o):
    def shifted(p, first_row, mu):
        row = lax.broadcasted_iota(jnp.int32, p.shape, 0)
        prev = jnp.where(row == 0, first_row, pltpu.roll(p, 1, axis=0))
        return p + (prev - p) * mu

    xm = shifted(p_ref[...], bp_ref[0], mu_ref[...])
    xl = shifted(pl_ref[...], bl_ref[0], mul_ref[...])
    r = xm[:, 0:WIDTH]
    k = xm[:, WIDTH:2 * WIDTH]
    v = xm[:, 2 * WIDTH:3 * WIDTH]
    wd = xl[:, 0:LORA_PAD]
    ad = xl[:, LORA_PAD:2 * LORA_PAD]
    gd = xl[:, 2 * LORA_PAD:]

    z = -(w0_ref[...] + _dot_f32(jnp.tanh(wd), wup_hi[...], wup_lo[...]))
    softplus = jnp.maximum(z, 0.0) + jnp.log(1.0 + jnp.exp(-jnp.abs(z)))
    decay = jnp.exp(-jnp.exp(-softplus - 0.5))
    a = _sigmoid(a0_ref[...] + _dot_f32(ad, aup_hi[...], aup_lo[...]))
    g = _dot_f32(_sigmoid(gd), gup_hi[...], gup_lo[...])

    bd = bd_ref[...]
    kk = k * kk_ref[...]
    kk = kk / jnp.maximum(jnp.sqrt(_segsum(kk * kk, bd)), 1e-12)
    k2 = k * (1.0 + (a - 1.0) * ka_ref[...])
    bonus = _segsum(r * k2 * rk_ref[...], bd) * v

    r_o[...] = r
    w_o[...] = decay
    k_o[...] = k2
    v_o[...] = v
    kk_o[...] = kk
    kka_o[...] = kk * a
    g_o[...] = g
    cb_o[...] = lnb_ref[...] + bonus


def _prep(proj, row0, n_rows, tm, first_rkv, first_lora, params):
    n_t = n_rows // tm
    blk0 = row0 // tm
    vec = lambda w: pl.BlockSpec((1, w), lambda i: (0, 0))
    mat = lambda r: pl.BlockSpec((r, WIDTH), lambda i: (0, 0))
    out = pl.BlockSpec((tm, WIDTH), lambda i: (i, 0))
    return pl.pallas_call(
        _prep_body,
        grid=(n_t,),
        in_specs=[
            pl.BlockSpec((tm, 3 * WIDTH), lambda i: (blk0 + i, COL_RKV // (3 * WIDTH))),
            pl.BlockSpec((tm, LORA_COLS), lambda i: (blk0 + i, COL_LORA // LORA_COLS)),
            pl.BlockSpec((1, 1, 3 * WIDTH), lambda i: (i, 0, 0)),
            pl.BlockSpec((1, 1, LORA_COLS), lambda i: (i, 0, 0)),
            vec(3 * WIDTH), vec(LORA_COLS), vec(WIDTH), vec(WIDTH), vec(WIDTH), vec(WIDTH), vec(WIDTH),
            vec(WIDTH), mat(LORA_PAD), mat(LORA_PAD), mat(LORA_PAD), mat(LORA_PAD),
            mat(GATE_LORA), mat(GATE_LORA),
            pl.BlockSpec((128, 128), lambda i: (0, 0)),
        ],
        out_specs=[out] * 8,
        out_shape=[jax.ShapeDtypeStruct((n_rows, WIDTH), F32)] * 8,
        compiler_params=pltpu.CompilerParams(
            dimension_semantics=("parallel",), vmem_limit_bytes=VMEM_LIMIT),
        name="rwkv_prep",
    )(proj, proj, first_rkv, first_lora, *params)


def _scan_body(r_ref, w_ref, k_ref, kk_ref, kka_ref, v_ref, s0_ref, y_ref, st_ref, s_ref, rows_ref, *, tb):
    operands = (w_ref, kka_ref, k_ref, r_ref, kk_ref)
    W, KKA, K, R, KK = range(5)

    @pl.when(pl.program_id(1) == 0)
    def _():
        s_ref[...] = s0_ref[0]

    lane = lax.broadcasted_iota(jnp.int32, (tb * KLO, 128), 1)
    for a, ref in enumerate(operands):
        x = ref[0].reshape(tb * KLO, 128)
        swapped = pltpu.roll(x, HEAD_DIM, axis=1)
        rows_ref[a, :, 0:KLO, :] = jnp.where(lane < HEAD_DIM, x, swapped).reshape(tb, KLO, 128)
        rows_ref[a, :, KLO:, :] = jnp.where(lane < HEAD_DIM, swapped, x).reshape(tb, KLO, 128)

    def row(a, t, k):
        return jnp.broadcast_to(rows_ref[a, t, pl.ds(k, 1), :], (KLO, 128))

    def minus_s_dot(t):
        acc = [jnp.zeros((KLO, 128), F32), jnp.zeros((KLO, 128), F32)]
        for k in range(HEAD_DIM):
            acc[k % 2] = acc[k % 2] - s_ref[k] * row(KK, t, k)
        return acc[0] + acc[1]

    def step(t, sa):
        v = v_ref[0, t]
        t_next = jnp.minimum(t + 1, tb - 1)
        y = [jnp.zeros((KLO, 128), F32), jnp.zeros((KLO, 128), F32)]
        sa_next = [jnp.zeros((KLO, 128), F32), jnp.zeros((KLO, 128), F32)]
        for k in range(HEAD_DIM):
            s_new = s_ref[k] * row(W, t, k) + sa * row(KKA, t, k) + v * row(K, t, k)
            s_ref[k] = s_new
            y[k % 2] = y[k % 2] + s_new * row(R, t, k)
            sa_next[k % 2] = sa_next[k % 2] - s_new * row(KK, t_next, k)
        y_ref[0, t] = y[0] + y[1]
        return sa_next[0] + sa_next[1]

    lax.fori_loop(0, tb, step, minus_s_dot(0))

    @pl.when(pl.program_id(1) == pl.num_programs(1) - 1)
    def _():
        st_ref[0] = s_ref[...]


def _scan(rows, v, s0, tb):
    g, t = v.shape[0], v.shape[1]
    row_spec = pl.BlockSpec((1, tb, KLO, 128), lambda i, j: (i, j, 0, 0))
    st_spec = pl.BlockSpec((1, HEAD_DIM, KLO, 128), lambda i, j: (i, 0, 0, 0))
    return pl.pallas_call(
        functools.partial(_scan_body, tb=tb),
        grid=(g, t // tb),
        in_specs=[row_spec] * 6 + [st_spec],
        out_specs=[row_spec, st_spec],
        out_shape=[jax.ShapeDtypeStruct((g, t, KLO, 128), F32),
                   jax.ShapeDtypeStruct((g, HEAD_DIM, KLO, 128), F32)],
        scratch_shapes=[pltpu.VMEM((HEAD_DIM, KLO, 128), F32),
                        pltpu.VMEM((5, tb, HEAD_DIM, 128), F32)],
        compiler_params=pltpu.CompilerParams(
            dimension_semantics=("parallel", "arbitrary"), vmem_limit_bytes=VMEM_LIMIT),
        name="rwkv_scan",
    )(*rows, v, s0)


def _rows_to_scan(x, n_batch, t):
    g = n_batch // LANE_GROUP_BATCH
    x = x.reshape(g, LANE_GROUP_BATCH, t, HEADS, 2, KLO)
    return x.transpose(0, 2, 5, 4, 1, 3).reshape(g, t, KLO, 128)


def _rows_from_scan(y, n_batch, t):
    g = n_batch // LANE_GROUP_BATCH
    y = y.reshape(g, t, KLO, 2, LANE_GROUP_BATCH, HEADS).transpose(0, 4, 1, 5, 3, 2)
    return y.reshape(n_batch * t, WIDTH)


def _state_to_scan(s, n_batch):
    g = n_batch // LANE_GROUP_BATCH
    s = s.reshape(g, LANE_GROUP_BATCH, HEADS, 2, KLO, HEAD_DIM)
    return s.transpose(0, 5, 4, 3, 1, 2).reshape(g, HEAD_DIM, KLO, 128)


def _state_from_scan(s, n_batch):
    g = n_batch // LANE_GROUP_BATCH
    s = s.reshape(g, HEAD_DIM, KLO, 2, LANE_GROUP_BATCH, HEADS)
    return s.transpose(0, 4, 5, 3, 2, 1).reshape(n_batch, HEADS, HEAD_DIM, HEAD_DIM)


def _merge_body(att_ref, y_ref, g_ref, cb_ref, ga_ref, gb_ref, x_ref, lnw_ref, gpost_ref, bd_ref,
                wa_ref, wr_ref, wo_ref, o_ref):
    bd = bd_ref[...]
    y = y_ref[...]
    d = y - _segsum(y, bd) * (1.0 / HEAD_DIM)
    var = _segsum(d * d, bd) * (1.0 / HEAD_DIM)
    rw = ((d * lax.rsqrt(var + GN_EPS) * lnw_ref[...] + cb_ref[...]) * g_ref[...]).astype(BF16)
    rw_o = _dot(rw, wr_ref[...])
    att_o = _dot(att_ref[...], wa_ref[...])
    mixed = (_sigmoid(ga_ref[...]) * att_o + _sigmoid(gb_ref[...]) * rw_o).astype(BF16)
    z = _dot(mixed, wo_ref[...])
    o_ref[...] = x_ref[...] + z * _rms_scale(z) * gpost_ref[...]


def _merge(att, y, g, cb, proj, x, ln_w, g_post, ones_bd, w_att, w_rwkv, w_out):
    m = x.shape[0]
    tm = min(MERGE_TM, m)
    tok = lambda w: pl.BlockSpec((tm, w), lambda i: (i, 0))
    const = lambda a: pl.BlockSpec(a.shape, lambda i: (0,) * a.ndim)
    return pl.pallas_call(
        _merge_body,
        grid=(m // tm,),
        in_specs=[
            tok(WIDTH), tok(WIDTH), tok(WIDTH), tok(WIDTH),
            pl.BlockSpec((tm, D_MODEL), lambda i: (i, COL_GA // D_MODEL)),
            pl.BlockSpec((tm, D_MODEL), lambda i: (i, COL_GB // D_MODEL)),
            tok(D_MODEL), const(ln_w), const(g_post), const(ones_bd),
            const(w_att), const(w_rwkv), const(w_out),
        ],
        out_specs=tok(D_MODEL),
        out_shape=jax.ShapeDtypeStruct((m, D_MODEL), F32),
        compiler_params=pltpu.CompilerParams(
            dimension_semantics=("parallel",), vmem_limit_bytes=VMEM_LIMIT),
        name="merge",
    )(att, y, g, cb, proj, proj, x, ln_w, g_post, ones_bd, w_att, w_rwkv, w_out)


def _pad_cols(x, width):
    return jnp.pad(x, ((0, 0), (0, width - x.shape[1])))


def _pad_rows(x, height):
    return jnp.pad(x, ((0, height - x.shape[0]), (0, 0)))


def _reorder_shift_cols(x):
    o = 3 * WIDTH
    wd = x[..., o:o + DECAY_LORA]
    ad = x[..., o + DECAY_LORA:o + DECAY_LORA + ICLR_LORA]
    gd = x[..., o + DECAY_LORA + ICLR_LORA:]
    pad = [(0, 0)] * (x.ndim - 1)
    lora = jnp.concatenate([jnp.pad(wd, pad + [(0, LORA_PAD - DECAY_LORA)]),
                            jnp.pad(ad, pad + [(0, LORA_PAD - ICLR_LORA)]), gd], axis=-1)
    return x[..., :o], lora


def _shift_row_from_proj(rows):
    lo = rows[:, COL_LORA:]
    return jnp.concatenate([rows[:, COL_RKV:COL_RKV + 3 * WIDTH], lo[:, :DECAY_LORA],
                            lo[:, LORA_PAD:LORA_PAD + ICLR_LORA], lo[:, 2 * LORA_PAD:]], axis=-1)


def _bias_diagonals(rel_bias):
    u = np.arange(ATT_DIAG)[None, :]
    s = np.arange(ATT_LEAD + 1)[:, None]
    idx = np.clip(ATT_KEYS - u - ATT_Q * s, -MAX_REL, MAX_REL) + MAX_REL
    tab = jnp.take(rel_bias, jnp.asarray(idx.reshape(-1), jnp.int32), axis=1)
    return tab.reshape(HEADS, ATT_LEAD + 1, ATT_DIAG).transpose(1, 0, 2)


def _bias_sample(rel_bias, t, n_past):
    rel = np.arange(t)[:, None] + n_past - np.arange(n_past + t)[None, :]
    idx = np.clip(rel, -MAX_REL, MAX_REL) + MAX_REL
    tab = jnp.take(rel_bias, jnp.asarray(idx.reshape(-1), jnp.int32), axis=1).reshape(HEADS, t, n_past + t)
    return tab[:, :, :n_past], tab[:, :, n_past:]


def _layer(x, n_batch, t, attend, first_rows, prep_tm, wkv0, scan_tb, wts):
    m = n_batch * t
    x = _ffn(x, *wts["ffn1"])
    proj = _proj(x, *wts["proj"])
    att = attend(proj)

    prep = _prep(proj, 0, m, prep_tm, *first_rows(proj), wts["prep"])
    r, wdec, k2, v, kk, kka, g, cb = prep
    rows = [_rows_to_scan(a, n_batch, t) for a in (r, wdec, k2, kk, kka)]
    y, s_t = _scan(rows, _rows_to_scan(v, n_batch, t), _state_to_scan(wkv0, n_batch), scan_tb)
    y = _rows_from_scan(y, n_batch, t)

    x = _merge(att, y, g, cb, proj, x, *wts["merge"])
    x = _ffn(x, *wts["ffn2"])

    last = jnp.concatenate([proj[(b + 1) * t - 1:(b + 1) * t] for b in range(n_batch)])
    return x, proj, _state_from_scan(s_t, n_batch), _shift_row_from_proj(last)[:, None]


def kernel(x_prompt, x_sample, cache_k, cache_v, state_wkv, state_shift, norm_ffn1_pre, norm_ffn1_post, w_ffn1_in, w_ffn1_down, norm_mix_pre, norm_mix_post, w_in, rel_bias, w_att_out, rwkv_mu, rwkv_w0, rwkv_w_up, rwkv_a0, rwkv_a_up, rwkv_g_up, rwkv_k_k, rwkv_k_a, rwkv_r_k, rwkv_ln_w, rwkv_ln_b, w_rwkv_out, w_out, norm_ffn2_pre, norm_ffn2_post, w_ffn2_in, w_ffn2_down):
    depth = w_in.shape[0]
    batch, seq, _ = x_prompt.shape
    dec_batch, dec_seq, _ = x_sample.shape
    n_past = cache_k.shape[2]
    m_p, m_s = batch * seq, dec_batch * dec_seq
    keep = min(PAST_BAND, seq)
    assert n_past == PAST_BAND and seq % ATT_Q == 0 and seq // ATT_Q > ATT_LEAD
    assert m_p % FFN_TM == 0 and m_p % PROJ_TM == 0 and m_p % MERGE_TM == 0 and seq % PREP_TM == 0
    assert seq % SCAN_TB == 0 and batch == LANE_GROUP_BATCH and dec_batch % LANE_GROUP_BATCH == 0

    x_p = x_prompt.reshape(m_p, D_MODEL)
    x_s = x_sample.reshape(m_s, D_MODEL)
    ones_bd = jnp.asarray(np.kron(np.eye(2), np.ones((HEAD_DIM, HEAD_DIM))), BF16)
    row = lambda v: v.reshape(1, -1)
    heads = lambda a, n, t: a.reshape(n, t, HEADS, HEAD_DIM)

    outs = {k: [] for k in ("kp", "vp", "wp", "sp", "ks", "vs", "ws", "ss")}
    for l in range(depth):
        w = w_in[l]
        o_lora = 6 * WIDTH
        o_ga = o_lora + DECAY_LORA + ICLR_LORA + GATE_LORA
        w_lora = jnp.concatenate([
            _pad_cols(w[:, o_lora:o_lora + DECAY_LORA], LORA_PAD),
            _pad_cols(w[:, o_lora + DECAY_LORA:o_lora + DECAY_LORA + ICLR_LORA], LORA_PAD),
            w[:, o_lora + DECAY_LORA + ICLR_LORA:o_ga]], axis=1)
        w_proj = jnp.concatenate([w[:, :o_lora], w[:, o_ga:], w_lora], axis=1).astype(BF16)
        mu_rkv, mu_lora = _reorder_shift_cols(rwkv_mu[l][None])
        wts = {
            "ffn1": (row(norm_ffn1_pre[l]), row(norm_ffn1_post[l]),
                     w_ffn1_in[l].astype(BF16), w_ffn1_down[l].astype(BF16)),
            "proj": (row(norm_mix_pre[l]), w_proj),
            "prep": (mu_rkv, mu_lora, row(rwkv_w0[l]), row(rwkv_a0[l]), row(rwkv_k_k[l]), row(rwkv_k_a[l]),
                     row(rwkv_r_k[l]), row(rwkv_ln_b[l]),
                     *_split_bf16(_pad_rows(rwkv_w_up[l], LORA_PAD)),
                     *_split_bf16(_pad_rows(rwkv_a_up[l], LORA_PAD)),
                     *_split_bf16(rwkv_g_up[l]), ones_bd),
            "merge": (row(rwkv_ln_w[l]), row(norm_mix_post[l]), ones_bd,
                      w_att_out[l].astype(BF16), w_rwkv_out[l].astype(BF16), w_out[l].astype(BF16)),
            "ffn2": (row(norm_ffn2_pre[l]), row(norm_ffn2_post[l]),
                     w_ffn2_in[l].astype(BF16), w_ffn2_down[l].astype(BF16)),
        }

        diag = _bias_diagonals(rel_bias[l])
        starts_seq = (jnp.arange(m_p // PREP_TM) % (seq // PREP_TM) == 0)[:, None]

        def first_rows_prompt(proj):
            def rows(col, width):
                last_of_tile = proj[PREP_TM - 1::PREP_TM, col:col + width]
                return jnp.where(starts_seq, 0.0, jnp.roll(last_of_tile, 1, axis=0))[:, None]
            return rows(COL_RKV, 3 * WIDTH), rows(COL_LORA, LORA_COLS)

        x_p, proj_p, wkv_p, sh_p = _layer(
            x_p, batch, seq, lambda proj: _attn_prompt(proj, diag, batch, seq), first_rows_prompt, PREP_TM,
            jnp.zeros((batch, HEADS, HEAD_DIM, HEAD_DIM), F32), SCAN_TB, wts)
        tail = lambda col: jnp.stack([proj_p[(b + 1) * seq - keep:(b + 1) * seq, col:col + WIDTH]
                                      for b in range(batch)])
        outs["kp"].append(heads(tail(COL_K), batch, keep))
        outs["vp"].append(heads(tail(COL_V), batch, keep))
        outs["wp"].append(wkv_p.astype(state_wkv.dtype))
        outs["sp"].append(sh_p)

        ck = cache_k[l].reshape(dec_batch, n_past, WIDTH)
        cv = cache_v[l].reshape(dec_batch, n_past, WIDTH)
        bias_c, bias_n = _bias_sample(rel_bias[l], dec_seq, n_past)
        x_s, proj_s, wkv_s, sh_s = _layer(
            x_s, dec_batch, dec_seq,
            lambda proj: _attn_sample(proj, 0, ck, cv, bias_c, bias_n, dec_batch, dec_seq),
            lambda proj: _reorder_shift_cols(state_shift[l]), dec_seq,
            state_wkv[l].astype(F32), dec_seq, wts)
        outs["ks"].append(heads(proj_s[:, COL_K:COL_K + WIDTH], dec_batch, dec_seq))
        outs["vs"].append(heads(proj_s[:, COL_V:COL_V + WIDTH], dec_batch, dec_seq))
        outs["ws"].append(wkv_s.astype(state_wkv.dtype))
        outs["ss"].append(sh_s)

    st = lambda k: jnp.stack(outs[k])
    return (x_p.reshape(batch, seq, D_MODEL), x_s.reshape(dec_batch, dec_seq, D_MODEL),
            st("kp"), st("vp"), st("wp"), st("sp"), st("ks"), st("vs"), st("ws"), st("ss"))
```

```python
import functools

import numpy as np
import jax
import jax.numpy as jnp
from jax import lax
from jax.experimental import pallas as pl
from jax.experimental.pallas import tpu as pltpu

F32 = jnp.float32
BF16 = jnp.bfloat16

D_MODEL = 2048
CHUNK = 64
LEFT_CHUNKS = 8
PAST_BAND = LEFT_CHUNKS * CHUNK
BAND = PAST_BAND + CHUNK
HEAD_DIM = 64
HEADS = 16
WIDTH = HEADS * HEAD_DIM
DECAY_LORA = 96
ICLR_LORA = 96
GATE_LORA = 256
LORA_PAD = 128
LORA_COLS = 2 * LORA_PAD + GATE_LORA
D_FF = 5632
MAX_REL = 128
RMS_EPS = 1e-6
GN_EPS = 64e-5
NEG_INF = -1e30

COL_Q, COL_K, COL_V = 0, WIDTH, 2 * WIDTH
COL_RKV = 3 * WIDTH
COL_GA = 6 * WIDTH
COL_GB = COL_GA + D_MODEL
COL_LORA = COL_GB + D_MODEL
PROJ_COLS = COL_LORA + LORA_COLS

VMEM_LIMIT = 56 * 1024 * 1024

FFN_TM = 512
FFN_TF = 512
PROJ_TM = 1024
PROJ_TN = 1536
MERGE_TM = 256
PREP_TM = 256
SCAN_TB = 64
ATT_Q = 2 * CHUNK
ATT_KEYS = PAST_BAND + ATT_Q
ATT_LEAD = PAST_BAND // ATT_Q
ATT_DIAG = ATT_KEYS + ATT_Q
LANE_GROUP_BATCH = 4
KLO = HEAD_DIM // 2
STEP_PITCH = KLO + 8


def _sigmoid(x):
    return 1.0 / (1.0 + jnp.exp(-x))


def _split_bf16(x):
    hi = x.astype(BF16)
    lo = (x - hi.astype(F32)).astype(BF16)
    return hi, lo


def _dot(a, b):
    return jnp.dot(a, b, preferred_element_type=F32)


def _dot_f32(a, b_hi, b_lo):
    a_hi, a_lo = _split_bf16(a)
    return _dot(a_hi, b_hi) + _dot(a_lo, b_hi) + _dot(a_hi, b_lo)


def _segsum(x, ones_bd):
    hi, lo = _split_bf16(x)
    outs = []
    for j in range(x.shape[-1] // 128):
        sl = slice(j * 128, (j + 1) * 128)
        outs.append(_dot(hi[:, sl], ones_bd) + _dot(lo[:, sl], ones_bd))
    return jnp.concatenate(outs, axis=-1)


def _rms_scale(x):
    return lax.rsqrt(jnp.mean(x * x, axis=-1, keepdims=True) + RMS_EPS)


def _ffn_body(x_ref, gpre_ref, gpost_ref, wg_ref, wu_ref, wd_ref, o_ref, h_ref, *, n_f):
    f = pl.program_id(1)

    @pl.when(f == 0)
    def _():
        x = x_ref[...]
        h_ref[...] = (x * _rms_scale(x) * gpre_ref[...]).astype(BF16)
        o_ref[...] = jnp.zeros_like(o_ref)

    h = h_ref[...]
    g = _dot(h, wg_ref[...])
    u = _dot(h, wu_ref[...])
    act = (g * _sigmoid(g) * u).astype(BF16)
    o_ref[...] += _dot(act, wd_ref[...])

    @pl.when(f == n_f - 1)
    def _():
        y = o_ref[...]
        o_ref[...] = x_ref[...] + 0.5 * (y * _rms_scale(y) * gpost_ref[...])


def _ffn(x, g_pre, g_post, w_in, w_down):
    m = x.shape[0]
    tm = min(FFN_TM, m)
    n_m, n_f = m // tm, D_FF // FFN_TF
    return pl.pallas_call(
        functools.partial(_ffn_body, n_f=n_f),
        grid=(n_m, n_f),
        in_specs=[
            pl.BlockSpec((tm, D_MODEL), lambda i, f: (i, 0)),
            pl.BlockSpec((1, D_MODEL), lambda i, f: (0, 0)),
            pl.BlockSpec((1, D_MODEL), lambda i, f: (0, 0)),
            pl.BlockSpec((D_MODEL, FFN_TF), lambda i, f: (0, f)),
            pl.BlockSpec((D_MODEL, FFN_TF), lambda i, f: (0, f + n_f)),
            pl.BlockSpec((FFN_TF, D_MODEL), lambda i, f: (f, 0)),
        ],
        out_specs=pl.BlockSpec((tm, D_MODEL), lambda i, f: (i, 0)),
        out_shape=jax.ShapeDtypeStruct((m, D_MODEL), F32),
        scratch_shapes=[pltpu.VMEM((tm, D_MODEL), BF16)],
        compiler_params=pltpu.CompilerParams(
            dimension_semantics=("parallel", "arbitrary"), vmem_limit_bytes=VMEM_LIMIT),
        name="ffn",
    )(x, g_pre, g_post, w_in, w_in, w_down)


def _proj_body(x_ref, g_ref, w_ref, o_ref, h_ref):
    @pl.when(pl.program_id(1) == 0)
    def _():
        x = x_ref[...]
        h_ref[...] = (x * _rms_scale(x) * g_ref[...]).astype(BF16)

    o_ref[...] = _dot(h_ref[...], w_ref[...])


def _proj(x, g, w):
    m = x.shape[0]
    tm = min(PROJ_TM, m)
    return pl.pallas_call(
        _proj_body,
        grid=(m // tm, PROJ_COLS // PROJ_TN),
        in_specs=[
            pl.BlockSpec((tm, D_MODEL), lambda i, n: (i, 0)),
            pl.BlockSpec((1, D_MODEL), lambda i, n: (0, 0)),
            pl.BlockSpec((D_MODEL, PROJ_TN), lambda i, n: (0, n)),
        ],
        out_specs=pl.BlockSpec((tm, PROJ_TN), lambda i, n: (i, n)),
        out_shape=jax.ShapeDtypeStruct((m, PROJ_COLS), F32),
        scratch_shapes=[pltpu.VMEM((tm, D_MODEL), BF16)],
        compiler_params=pltpu.CompilerParams(
            dimension_semantics=("parallel", "arbitrary"), vmem_limit_bytes=VMEM_LIMIT),
        name="proj",
    )(x, g, w)


def _softmax_pv(parts):
    m = functools.reduce(jnp.maximum, [jnp.max(s, axis=-1, keepdims=True) for s, _ in parts])
    num, den = 0.0, 0.0
    for s, v in parts:
        p = jnp.exp(s - m)
        den = den + jnp.sum(p, axis=-1, keepdims=True)
        num = num + _dot(p.astype(BF16), v)
    return num / den


def _qk(q, k):
    return lax.dot_general(q, k, (((1,), (1,)), ((), ())), preferred_element_type=F32)


def _attn_prompt_body(q_ref, k_ref, v_ref, diag_ref, o_ref, kbf_ref, vbf_ref, bias_ref):
    c = pl.program_id(1)
    shift = jnp.maximum(ATT_LEAD - c, 0)

    @pl.when(c == 0)
    def _():
        kbf_ref[...] = k_ref[...].astype(BF16)
        vbf_ref[...] = v_ref[...].astype(BF16)

    @pl.when(c <= ATT_LEAD)
    def _():
        key = lax.broadcasted_iota(jnp.int32, (ATT_KEYS, ATT_Q), 0) // CHUNK
        qry = lax.broadcasted_iota(jnp.int32, (ATT_KEYS, ATT_Q), 1) // CHUNK
        dist = (LEFT_CHUNKS + qry) - (key + shift * (ATT_Q // CHUNK))
        in_band = (dist >= 0) & (dist <= LEFT_CHUNKS)
        for h in range(HEADS):
            diag = jnp.broadcast_to(diag_ref[shift, pl.ds(h, 1), :], (ATT_Q, ATT_DIAG))
            toep = pltpu.roll(diag, ATT_DIAG - ATT_Q, axis=1, stride=1, stride_axis=0)
            bias_ref[h] = jnp.where(in_band, toep[:, :ATT_KEYS].T, NEG_INF)

    start = pl.multiple_of(jnp.maximum(c - ATT_LEAD, 0) * ATT_Q, ATT_Q)
    lane = lax.broadcasted_iota(jnp.int32, (ATT_Q, 128), 1)
    for pair in range(HEADS // 2):
        sl = slice(pair * 128, (pair + 1) * 128)
        kp = kbf_ref[pl.ds(start, ATT_KEYS), sl]
        vp = vbf_ref[pl.ds(start, ATT_KEYS), sl]
        qp = q_ref[:, sl] * (HEAD_DIM ** -0.5)
        halves = []
        for half in range(2):
            qm = jnp.where((lane >= HEAD_DIM) == bool(half), qp, 0.0).astype(BF16)
            s = _qk(kp, qm) + bias_ref[2 * pair + half]
            p = jnp.exp(s - jnp.max(s, axis=0, keepdims=True))
            p = (p * (1.0 / jnp.sum(p, axis=0, keepdims=True))).astype(BF16)
            halves.append(lax.dot_general(p, vp, (((0,), (0,)), ((), ())),
                                          preferred_element_type=F32))
        o_ref[:, sl] = jnp.where(lane < HEAD_DIM, halves[0], halves[1]).astype(BF16)


def _attn_prompt(proj, diag, batch, seq):
    n_c = seq // ATT_Q
    once = dict(pipeline_mode=pl.Buffered(1))
    return pl.pallas_call(
        _attn_prompt_body,
        grid=(batch, n_c),
        in_specs=[
            pl.BlockSpec((ATT_Q, WIDTH), lambda b, c: (b * n_c + c, COL_Q // WIDTH)),
            pl.BlockSpec((seq, WIDTH), lambda b, c: (b, COL_K // WIDTH), **once),
            pl.BlockSpec((seq, WIDTH), lambda b, c: (b, COL_V // WIDTH), **once),
            pl.BlockSpec(diag.shape, lambda b, c: (0, 0, 0)),
        ],
        out_specs=pl.BlockSpec((ATT_Q, WIDTH), lambda b, c: (b * n_c + c, 0)),
        out_shape=jax.ShapeDtypeStruct((batch * seq, WIDTH), BF16),
        scratch_shapes=[pltpu.VMEM((seq, WIDTH), BF16), pltpu.VMEM((seq, WIDTH), BF16),
                        pltpu.VMEM((HEADS, ATT_KEYS, ATT_Q), F32)],
        compiler_params=pltpu.CompilerParams(
            dimension_semantics=("arbitrary", "arbitrary"), vmem_limit_bytes=VMEM_LIMIT),
        name="attn_prompt",
    )(proj, proj, proj, diag)


def _attn_sample_body(q_ref, kn_ref, vn_ref, ck_ref, cv_ref, bias_c_ref, bias_n_ref, o_ref):
    q = (q_ref[...] * (HEAD_DIM ** -0.5)).astype(BF16)
    kn = kn_ref[...].astype(BF16)
    vn = vn_ref[...].astype(BF16)
    outs = []
    for h in range(HEADS):
        sl = slice(h * HEAD_DIM, (h + 1) * HEAD_DIM)
        qh = q[:, sl]
        ck = ck_ref[0, :, sl].astype(BF16)
        cv = cv_ref[0, :, sl].astype(BF16)
        s_c = _qk(qh, ck) + bias_c_ref[h]
        s_n = _qk(qh, kn[:, sl]) + bias_n_ref[h]
        outs.append(_softmax_pv([(s_c, cv), (s_n, vn[:, sl])]))
    o_ref[...] = jnp.concatenate(outs, axis=-1).astype(BF16)


def _attn_sample(proj, row0, cache_k, cache_v, bias_c, bias_n, batch, t):
    n_past = cache_k.shape[1]
    blk0 = row0 // t
    return pl.pallas_call(
        _attn_sample_body,
        grid=(batch,),
        in_specs=[
            pl.BlockSpec((t, WIDTH), lambda b: (blk0 + b, COL_Q // WIDTH)),
            pl.BlockSpec((t, WIDTH), lambda b: (blk0 + b, COL_K // WIDTH)),
            pl.BlockSpec((t, WIDTH), lambda b: (blk0 + b, COL_V // WIDTH)),
            pl.BlockSpec((1, n_past, WIDTH), lambda b: (b, 0, 0)),
            pl.BlockSpec((1, n_past, WIDTH), lambda b: (b, 0, 0)),
            pl.BlockSpec((HEADS, t, n_past), lambda b: (0, 0, 0)),
            pl.BlockSpec((HEADS, t, t), lambda b: (0, 0, 0)),
        ],
        out_specs=pl.BlockSpec((t, WIDTH), lambda b: (b, 0)),
        out_shape=jax.ShapeDtypeStruct((batch * t, WIDTH), BF16),
        compiler_params=pltpu.CompilerParams(
            dimension_semantics=("parallel",), vmem_limit_bytes=VMEM_LIMIT),
        name="attn_sample",
    )(proj, proj, proj, cache_k, cache_v, bias_c, bias_n)


def _prep_body(p_ref, pl_ref, bp_ref, bl_ref, mu_ref, mul_ref, w0_ref, a0_ref, kk_ref, ka_ref, rk_ref,
               lnb_ref, wup_hi, wup_lo, aup_hi, aup_lo, gup_hi, gup_lo, bd_ref,
               r_o, w_o, k_o, v_o, kk_o, kka_o, g_o, cb_o):
    def shifted(p, first_row, mu):
        row = lax.broadcasted_iota(jnp.int32, p.shape, 0)
        prev = jnp.where(row == 0, first_row, pltpu.roll(p, 1, axis=0))
        return p + (prev - p) * mu

    xm = shifted(p_ref[...], bp_ref[0], mu_ref[...])
    xl = shifted(pl_ref[...], bl_ref[0], mul_ref[...])
    r = xm[:, 0:WIDTH]
    k = xm[:, WIDTH:2 * WIDTH]
    v = xm[:, 2 * WIDTH:3 * WIDTH]
    wd = xl[:, 0:LORA_PAD]
    ad = xl[:, LORA_PAD:2 * LORA_PAD]
    gd = xl[:, 2 * LORA_PAD:]

    z = -(w0_ref[...] + _dot_f32(jnp.tanh(wd), wup_hi[...], wup_lo[...]))
    softplus = jnp.maximum(z, 0.0) + jnp.log(1.0 + jnp.exp(-jnp.abs(z)))
    decay = jnp.exp(-jnp.exp(-softplus - 0.5))
    a = _sigmoid(a0_ref[...] + _dot_f32(ad, aup_hi[...], aup_lo[...]))
    g = _dot_f32(_sigmoid(gd), gup_hi[...], gup_lo[...])

    bd = bd_ref[...]
    kk = k * kk_ref[...]
    kk = kk / jnp.maximum(jnp.sqrt(_segsum(kk * kk, bd)), 1e-12)
    k2 = k * (1.0 + (a - 1.0) * ka_ref[...])
    bonus = _segsum(r * k2 * rk_ref[...], bd) * v

    r_o[...] = r
    w_o[...] = decay
    k_o[...] = k2
    v_o[...] = v
    kk_o[...] = kk
    kka_o[...] = kk * a
    g_o[...] = g
    cb_o[...] = lnb_ref[...] + bonus


def _prep(proj, row0, n_rows, tm, first_rkv, first_lora, params):
    n_t = n_rows // tm
    blk0 = row0 // tm
    vec = lambda w: pl.BlockSpec((1, w), lambda i: (0, 0))
    mat = lambda r: pl.BlockSpec((r, WIDTH), lambda i: (0, 0))
    out = pl.BlockSpec((tm, WIDTH), lambda i: (i, 0))
    return pl.pallas_call(
        _prep_body,
        grid=(n_t,),
        in_specs=[
            pl.BlockSpec((tm, 3 * WIDTH), lambda i: (blk0 + i, COL_RKV // (3 * WIDTH))),
            pl.BlockSpec((tm, LORA_COLS), lambda i: (blk0 + i, COL_LORA // LORA_COLS)),
            pl.BlockSpec((1, 1, 3 * WIDTH), lambda i: (i, 0, 0)),
            pl.BlockSpec((1, 1, LORA_COLS), lambda i: (i, 0, 0)),
            vec(3 * WIDTH), vec(LORA_COLS), vec(WIDTH), vec(WIDTH), vec(WIDTH), vec(WIDTH), vec(WIDTH),
            vec(WIDTH), mat(LORA_PAD), mat(LORA_PAD), mat(LORA_PAD), mat(LORA_PAD),
            mat(GATE_LORA), mat(GATE_LORA),
            pl.BlockSpec((128, 128), lambda i: (0, 0)),
        ],
        out_specs=[out] * 8,
        out_shape=[jax.ShapeDtypeStruct((n_rows, WIDTH), F32)] * 8,
        compiler_params=pltpu.CompilerParams(
            dimension_semantics=("parallel",), vmem_limit_bytes=VMEM_LIMIT),
        name="rwkv_prep",
    )(proj, proj, first_rkv, first_lora, *params)


def _scan_steps(s_ref, row, get_v, put_y, n_steps):
    W, KKA, K, R, KK = range(5)
    zero = jnp.zeros((KLO, 128), F32)

    sa0 = [zero, zero]
    for k in range(HEAD_DIM):
        sa0[k % 2] = sa0[k % 2] - s_ref[k] * row(KK, 0, k)

    def step(t, sa):
        v = get_v(t)
        t_next = jnp.minimum(t + 1, n_steps - 1)
        y = [zero, zero]
        sa_next = [zero, zero]
        for k in range(HEAD_DIM):
            s_new = s_ref[k] * row(W, t, k) + sa * row(KKA, t, k) + v * row(K, t, k)
            s_ref[k] = s_new
            y[k % 2] = y[k % 2] + s_new * row(R, t, k)
            sa_next[k % 2] = sa_next[k % 2] - s_new * row(KK, t_next, k)
        put_y(t, y[0] + y[1])
        return sa_next[0] + sa_next[1]

    lax.fori_loop(0, n_steps, step, sa0[0] + sa0[1])


def _scan_body(w_ref, kka_ref, k_ref, r_ref, kk_ref, v_ref, s0_ref, y_ref, st_ref, s_ref, rows_ref, *, tb):
    @pl.when(pl.program_id(1) == 0)
    def _():
        s_ref[...] = s0_ref[0]

    lane = lax.broadcasted_iota(jnp.int32, (tb * KLO, 128), 1)
    for a, ref in enumerate((w_ref, kka_ref, k_ref, r_ref, kk_ref)):
        x = ref[0].reshape(tb * KLO, 128)
        swapped = pltpu.roll(x, HEAD_DIM, axis=1)
        rows_ref[a, :, 0:KLO, :] = jnp.where(lane < HEAD_DIM, x, swapped).reshape(tb, KLO, 128)
        rows_ref[a, :, KLO:, :] = jnp.where(lane < HEAD_DIM, swapped, x).reshape(tb, KLO, 128)

    def row(a, t, k):
        return jnp.broadcast_to(rows_ref[a, t, pl.ds(k, 1), :], (KLO, 128))

    def put_y(t, y):
        y_ref[0, t] = y

    _scan_steps(s_ref, row, lambda t: v_ref[0, t], put_y, tb)

    @pl.when(pl.program_id(1) == pl.num_programs(1) - 1)
    def _():
        st_ref[0] = s_ref[...]


def _scan_tok_body(w_ref, kka_ref, k_ref, r_ref, kk_ref, v_ref, s0_ref, y_ref, st_ref,
                   s_ref, rows_ref, xt_ref, vt_ref, yt_ref, *, tb):
    @pl.when(pl.program_id(0) == 0)
    def _():
        s_ref[...] = s0_ref[0]

    def channel_major(ref):
        for b in range(LANE_GROUP_BATCH):
            xt_ref[b] = ref[b].T

    def lane_tile(chan):
        parts = [xt_ref[b, pl.ds(c, HEADS, stride=HEAD_DIM), :] for c in chan for b in range(LANE_GROUP_BATCH)]
        return jnp.concatenate(parts, axis=0)

    for a, ref in enumerate((w_ref, kka_ref, k_ref, r_ref, kk_ref)):
        channel_major(ref)
        for k in range(HEAD_DIM):
            rows_ref[a, k] = lane_tile((k, k)).T
    channel_major(v_ref)
    for j in range(KLO):
        vt_ref[pl.ds(j, tb, stride=STEP_PITCH), :] = lane_tile((j, KLO + j)).T

    def row(a, t, k):
        return jnp.broadcast_to(rows_ref[a, k, pl.ds(t, 1), :], (KLO, 128))

    def step_rows(t):
        return pl.ds(pl.multiple_of(t * STEP_PITCH, 8), KLO)

    def put_y(t, y):
        yt_ref[step_rows(t), :] = y

    _scan_steps(s_ref, row, lambda t: vt_ref[step_rows(t), :], put_y, tb)

    for j in range(KLO):
        tile = yt_ref[pl.ds(j, tb, stride=STEP_PITCH), :].T
        for half in range(2):
            for b in range(LANE_GROUP_BATCH):
                r0 = (half * LANE_GROUP_BATCH + b) * HEADS
                xt_ref[b, pl.ds(half * KLO + j, HEADS, stride=HEAD_DIM), :] = tile[r0:r0 + HEADS]
    for b in range(LANE_GROUP_BATCH):
        y_ref[b] = xt_ref[b].T

    @pl.when(pl.program_id(0) == pl.num_programs(0) - 1)
    def _():
        st_ref[0] = s_ref[...]


def _scan_tok(ops, v, s0, tb):
    nb, t, _ = v.shape
    tok_spec = pl.BlockSpec((nb, tb, WIDTH), lambda j: (0, j, 0))
    st_spec = pl.BlockSpec((1, HEAD_DIM, KLO, 128), lambda j: (0, 0, 0, 0))
    return pl.pallas_call(
        functools.partial(_scan_tok_body, tb=tb),
        grid=(t // tb,),
        in_specs=[tok_spec] * 6 + [st_spec],
        out_specs=[tok_spec, st_spec],
        out_shape=[jax.ShapeDtypeStruct((nb, t, WIDTH), F32),
                   jax.ShapeDtypeStruct((1, HEAD_DIM, KLO, 128), F32)],
        scratch_shapes=[pltpu.VMEM((HEAD_DIM, KLO, 128), F32),
                        pltpu.VMEM((5, HEAD_DIM, tb, 128), F32),
                        pltpu.VMEM((nb, WIDTH, tb), F32),
                        pltpu.VMEM((tb * STEP_PITCH, 128), F32),
                        pltpu.VMEM((tb * STEP_PITCH, 128), F32)],
        compiler_params=pltpu.CompilerParams(
            dimension_semantics=("arbitrary",), vmem_limit_bytes=VMEM_LIMIT),
        name="rwkv_scan_tok",
    )(*ops, v, s0)


def _scan(rows, v, s0, tb):
    g, t = v.shape[0], v.shape[1]
    row_spec = pl.BlockSpec((1, tb, KLO, 128), lambda i, j: (i, j, 0, 0))
    st_spec = pl.BlockSpec((1, HEAD_DIM, KLO, 128), lambda i, j: (i, 0, 0, 0))
    return pl.pallas_call(
        functools.partial(_scan_body, tb=tb),
        grid=(g, t // tb),
        in_specs=[row_spec] * 6 + [st_spec],
        out_specs=[row_spec, st_spec],
        out_shape=[jax.ShapeDtypeStruct((g, t, KLO, 128), F32),
                   jax.ShapeDtypeStruct((g, HEAD_DIM, KLO, 128), F32)],
        scratch_shapes=[pltpu.VMEM((HEAD_DIM, KLO, 128), F32),
                        pltpu.VMEM((5, tb, HEAD_DIM, 128), F32)],
        compiler_params=pltpu.CompilerParams(
            dimension_semantics=("parallel", "arbitrary"), vmem_limit_bytes=VMEM_LIMIT),
        name="rwkv_scan",
    )(*rows, v, s0)


def _rows_to_scan(x, n_batch, t):
    g = n_batch // LANE_GROUP_BATCH
    x = x.reshape(g, LANE_GROUP_BATCH, t, HEADS, 2, KLO)
    return x.transpose(0, 2, 5, 4, 1, 3).reshape(g, t, KLO, 128)


def _rows_from_scan(y, n_batch, t):
    g = n_batch // LANE_GROUP_BATCH
    y = y.reshape(g, t, KLO, 2, LANE_GROUP_BATCH, HEADS).transpose(0, 4, 1, 5, 3, 2)
    return y.reshape(n_batch * t, WIDTH)


def _state_to_scan(s, n_batch):
    g = n_batch // LANE_GROUP_BATCH
    s = s.reshape(g, LANE_GROUP_BATCH, HEADS, 2, KLO, HEAD_DIM)
    return s.transpose(0, 5, 4, 3, 1, 2).reshape(g, HEAD_DIM, KLO, 128)


def _state_from_scan(s, n_batch):
    g = n_batch // LANE_GROUP_BATCH
    s = s.reshape(g, HEAD_DIM, KLO, 2, LANE_GROUP_BATCH, HEADS)
    return s.transpose(0, 4, 5, 3, 2, 1).reshape(n_batch, HEADS, HEAD_DIM, HEAD_DIM)


def _merge_body(att_ref, y_ref, g_ref, cb_ref, ga_ref, gb_ref, x_ref, lnw_ref, gpost_ref, bd_ref,
                wa_ref, wr_ref, wo_ref, o_ref):
    bd = bd_ref[...]
    y = y_ref[...]
    d = y - _segsum(y, bd) * (1.0 / HEAD_DIM)
    var = _segsum(d * d, bd) * (1.0 / HEAD_DIM)
    rw = ((d * lax.rsqrt(var + GN_EPS) * lnw_ref[...] + cb_ref[...]) * g_ref[...]).astype(BF16)
    rw_o = _dot(rw, wr_ref[...])
    att_o = _dot(att_ref[...], wa_ref[...])
    mixed = (_sigmoid(ga_ref[...]) * att_o + _sigmoid(gb_ref[...]) * rw_o).astype(BF16)
    z = _dot(mixed, wo_ref[...])
    o_ref[...] = x_ref[...] + z * _rms_scale(z) * gpost_ref[...]


def _merge(att, y, g, cb, proj, x, ln_w, g_post, ones_bd, w_att, w_rwkv, w_out):
    m = x.shape[0]
    tm = min(MERGE_TM, m)
    tok = lambda w: pl.BlockSpec((tm, w), lambda i: (i, 0))
    const = lambda a: pl.BlockSpec(a.shape, lambda i: (0,) * a.ndim)
    return pl.pallas_call(
        _merge_body,
        grid=(m // tm,),
        in_specs=[
            tok(WIDTH), tok(WIDTH), tok(WIDTH), tok(WIDTH),
            pl.BlockSpec((tm, D_MODEL), lambda i: (i, COL_GA // D_MODEL)),
            pl.BlockSpec((tm, D_MODEL), lambda i: (i, COL_GB // D_MODEL)),
            tok(D_MODEL), const(ln_w), const(g_post), const(ones_bd),
            const(w_att), const(w_rwkv), const(w_out),
        ],
        out_specs=tok(D_MODEL),
        out_shape=jax.ShapeDtypeStruct((m, D_MODEL), F32),
        compiler_params=pltpu.CompilerParams(
            dimension_semantics=("parallel",), vmem_limit_bytes=VMEM_LIMIT),
        name="merge",
    )(att, y, g, cb, proj, proj, x, ln_w, g_post, ones_bd, w_att, w_rwkv, w_out)


def _pad_cols(x, width):
    return jnp.pad(x, ((0, 0), (0, width - x.shape[1])))


def _pad_rows(x, height):
    return jnp.pad(x, ((0, height - x.shape[0]), (0, 0)))


def _reorder_shift_cols(x):
    o = 3 * WIDTH
    wd = x[..., o:o + DECAY_LORA]
    ad = x[..., o + DECAY_LORA:o + DECAY_LORA + ICLR_LORA]
    gd = x[..., o + DECAY_LORA + ICLR_LORA:]
    pad = [(0, 0)] * (x.ndim - 1)
    lora = jnp.concatenate([jnp.pad(wd, pad + [(0, LORA_PAD - DECAY_LORA)]),
                            jnp.pad(ad, pad + [(0, LORA_PAD - ICLR_LORA)]), gd], axis=-1)
    return x[..., :o], lora


def _shift_row_from_proj(rows):
    lo = rows[:, COL_LORA:]
    return jnp.concatenate([rows[:, COL_RKV:COL_RKV + 3 * WIDTH], lo[:, :DECAY_LORA],
                            lo[:, LORA_PAD:LORA_PAD + ICLR_LORA], lo[:, 2 * LORA_PAD:]], axis=-1)


def _bias_diagonals(rel_bias):
    u = np.arange(ATT_DIAG)[None, :]
    s = np.arange(ATT_LEAD + 1)[:, None]
    idx = np.clip(ATT_KEYS - u - ATT_Q * s, -MAX_REL, MAX_REL) + MAX_REL
    tab = jnp.take(rel_bias, jnp.asarray(idx.reshape(-1), jnp.int32), axis=1)
    return tab.reshape(HEADS, ATT_LEAD + 1, ATT_DIAG).transpose(1, 0, 2)


def _bias_sample(rel_bias, t, n_past):
    rel = np.arange(t)[:, None] + n_past - np.arange(n_past + t)[None, :]
    idx = np.clip(rel, -MAX_REL, MAX_REL) + MAX_REL
    tab = jnp.take(rel_bias, jnp.asarray(idx.reshape(-1), jnp.int32), axis=1).reshape(HEADS, t, n_past + t)
    return tab[:, :, :n_past], tab[:, :, n_past:]


def _layer(x, n_batch, t, attend, first_rows, prep_tm, wkv0, scan_tb, wts):
    m = n_batch * t
    x = _ffn(x, *wts["ffn1"])
    proj = _proj(x, *wts["proj"])
    att = attend(proj)

    prep = _prep(proj, 0, m, prep_tm, *first_rows(proj), wts["prep"])
    r, wdec, k2, v, kk, kka, g, cb = prep
    ops = (wdec, kka, k2, r, kk)
    if n_batch == LANE_GROUP_BATCH and scan_tb % 64 == 0:
        tok = lambda a: a.reshape(n_batch, t, WIDTH)
        y, s_t = _scan_tok([tok(a) for a in ops], tok(v), _state_to_scan(wkv0, n_batch), scan_tb)
        y = y.reshape(m, WIDTH)
    else:
        rows = [_rows_to_scan(a, n_batch, t) for a in ops]
        y, s_t = _scan(rows, _rows_to_scan(v, n_batch, t), _state_to_scan(wkv0, n_batch), scan_tb)
        y = _rows_from_scan(y, n_batch, t)

    x = _merge(att, y, g, cb, proj, x, *wts["merge"])
    x = _ffn(x, *wts["ffn2"])

    last = jnp.concatenate([proj[(b + 1) * t - 1:(b + 1) * t] for b in range(n_batch)])
    return x, proj, _state_from_scan(s_t, n_batch), _shift_row_from_proj(last)[:, None]


def kernel(x_prompt, x_sample, cache_k, cache_v, state_wkv, state_shift, norm_ffn1_pre, norm_ffn1_post, w_ffn1_in, w_ffn1_down, norm_mix_pre, norm_mix_post, w_in, rel_bias, w_att_out, rwkv_mu, rwkv_w0, rwkv_w_up, rwkv_a0, rwkv_a_up, rwkv_g_up, rwkv_k_k, rwkv_k_a, rwkv_r_k, rwkv_ln_w, rwkv_ln_b, w_rwkv_out, w_out, norm_ffn2_pre, norm_ffn2_post, w_ffn2_in, w_ffn2_down):
    depth = w_in.shape[0]
    batch, seq, _ = x_prompt.shape
    dec_batch, dec_seq, _ = x_sample.shape
    n_past = cache_k.shape[2]
    m_p, m_s = batch * seq, dec_batch * dec_seq
    keep = min(PAST_BAND, seq)
    assert n_past == PAST_BAND and seq % ATT_Q == 0 and seq // ATT_Q > ATT_LEAD
    assert m_p % FFN_TM == 0 and m_p % PROJ_TM == 0 and m_p % MERGE_TM == 0 and seq % PREP_TM == 0
    assert seq % SCAN_TB == 0 and batch == LANE_GROUP_BATCH and dec_batch % LANE_GROUP_BATCH == 0

    x_p = x_prompt.reshape(m_p, D_MODEL)
    x_s = x_sample.reshape(m_s, D_MODEL)
    ones_bd = jnp.asarray(np.kron(np.eye(2), np.ones((HEAD_DIM, HEAD_DIM))), BF16)
    row = lambda v: v.reshape(1, -1)
    heads = lambda a, n, t: a.reshape(n, t, HEADS, HEAD_DIM)

    outs = {k: [] for k in ("kp", "vp", "wp", "sp", "ks", "vs", "ws", "ss")}
    for l in range(depth):
        w = w_in[l]
        o_lora = 6 * WIDTH
        o_ga = o_lora + DECAY_LORA + ICLR_LORA + GATE_LORA
        w_lora = jnp.concatenate([
            _pad_cols(w[:, o_lora:o_lora + DECAY_LORA], LORA_PAD),
            _pad_cols(w[:, o_lora + DECAY_LORA:o_lora + DECAY_LORA + ICLR_LORA], LORA_PAD),
            w[:, o_lora + DECAY_LORA + ICLR_LORA:o_ga]], axis=1)
        w_proj = jnp.concatenate([w[:, :o_lora], w[:, o_ga:], w_lora], axis=1).astype(BF16)
        mu_rkv, mu_lora = _reorder_shift_cols(rwkv_mu[l][None])
        wts = {
            "ffn1": (row(norm_ffn1_pre[l]), row(norm_ffn1_post[l]),
                     w_ffn1_in[l].astype(BF16), w_ffn1_down[l].astype(BF16)),
            "proj": (row(norm_mix_pre[l]), w_proj),
            "prep": (mu_rkv, mu_lora, row(rwkv_w0[l]), row(rwkv_a0[l]), row(rwkv_k_k[l]), row(rwkv_k_a[l]),
                     row(rwkv_r_k[l]), row(rwkv_ln_b[l]),
                     *_split_bf16(_pad_rows(rwkv_w_up[l], LORA_PAD)),
                     *_split_bf16(_pad_rows(rwkv_a_up[l], LORA_PAD)),
                     *_split_bf16(rwkv_g_up[l]), ones_bd),
            "merge": (row(rwkv_ln_w[l]), row(norm_mix_post[l]), ones_bd,
                      w_att_out[l].astype(BF16), w_rwkv_out[l].astype(BF16), w_out[l].astype(BF16)),
            "ffn2": (row(norm_ffn2_pre[l]), row(norm_ffn2_post[l]),
                     w_ffn2_in[l].astype(BF16), w_ffn2_down[l].astype(BF16)),
        }

        diag = _bias_diagonals(rel_bias[l])
        starts_seq = (jnp.arange(m_p // PREP_TM) % (seq // PREP_TM) == 0)[:, None]

        def first_rows_prompt(proj):
            def rows(col, width):
                last_of_tile = proj[PREP_TM - 1::PREP_TM, col:col + width]
                return jnp.where(starts_seq, 0.0, jnp.roll(last_of_tile, 1, axis=0))[:, None]
            return rows(COL_RKV, 3 * WIDTH), rows(COL_LORA, LORA_COLS)

        x_p, proj_p, wkv_p, sh_p = _layer(
            x_p, batch, seq, lambda proj: _attn_prompt(proj, diag, batch, seq), first_rows_prompt, PREP_TM,
            jnp.zeros((batch, HEADS, HEAD_DIM, HEAD_DIM), F32), SCAN_TB, wts)
        tail = lambda col: jnp.stack([proj_p[(b + 1) * seq - keep:(b + 1) * seq, col:col + WIDTH]
                                      for b in range(batch)])
        outs["kp"].append(heads(tail(COL_K), batch, keep))
        outs["vp"].append(heads(tail(COL_V), batch, keep))
        outs["wp"].append(wkv_p.astype(state_wkv.dtype))
        outs["sp"].append(sh_p)

        ck = cache_k[l].reshape(dec_batch, n_past, WIDTH)
        cv = cache_v[l].reshape(dec_batch, n_past, WIDTH)
        bias_c, bias_n = _bias_sample(rel_bias[l], dec_seq, n_past)
        x_s, proj_s, wkv_s, sh_s = _layer(
            x_s, dec_batch, dec_seq,
            lambda proj: _attn_sample(proj, 0, ck, cv, bias_c, bias_n, dec_batch, dec_seq),
            lambda proj: _reorder_shift_cols(state_shift[l]), dec_seq,
            state_wkv[l].astype(F32), dec_seq, wts)
        outs["ks"].append(heads(proj_s[:, COL_K:COL_K + WIDTH], dec_batch, dec_seq))
        outs["vs"].append(heads(proj_s[:, COL_V:COL_V + WIDTH], dec_batch, dec_seq))
        outs["ws"].append(wkv_s.astype(state_wkv.dtype))
        outs["ss"].append(sh_s)

    st = lambda k: jnp.stack(outs[k])
    return (x_p.reshape(batch, seq, D_MODEL), x_s.reshape(dec_batch, dec_seq, D_MODEL),
            st("kp"), st("vp"), st("wp"), st("sp"), st("ks"), st("vs"), st("ws"), st("ss"))
```

```python
import functools

import numpy as np
import jax
import jax.numpy as jnp
from jax import lax
from jax.experimental import pallas as pl
from jax.experimental.pallas import tpu as pltpu

F32 = jnp.float32
BF16 = jnp.bfloat16

D_MODEL = 2048
CHUNK = 64
LEFT_CHUNKS = 8
PAST_BAND = LEFT_CHUNKS * CHUNK
BAND = PAST_BAND + CHUNK
HEAD_DIM = 64
HEADS = 16
WIDTH = HEADS * HEAD_DIM
DECAY_LORA = 96
ICLR_LORA = 96
GATE_LORA = 256
LORA_PAD = 128
LORA_COLS = 2 * LORA_PAD + GATE_LORA
D_FF = 5632
MAX_REL = 128
RMS_EPS = 1e-6
GN_EPS = 64e-5
NEG_INF = -1e30

COL_Q, COL_K, COL_V = 0, WIDTH, 2 * WIDTH
COL_RKV = 3 * WIDTH
COL_GA = 6 * WIDTH
COL_GB = COL_GA + D_MODEL
COL_LORA = COL_GB + D_MODEL
PROJ_COLS = COL_LORA + LORA_COLS

VMEM_LIMIT = 56 * 1024 * 1024
VMEM_LIMIT_MAX = 60 * 1024 * 1024

FFN_TM = 512
FFN_TF = 512
PROJ_TM = 1024
PROJ_TN = 768
MERGE_TM = 256
PREP_TM = 256
SCAN_TB = 64
ATT_Q = 2 * CHUNK
ATT_KEYS = PAST_BAND + ATT_Q
ATT_LEAD = PAST_BAND // ATT_Q
ATT_DIAG = ATT_KEYS + ATT_Q
LANE_GROUP_BATCH = 4
KLO = HEAD_DIM // 2
STEP_PITCH = KLO + 8
XT_PITCH = HEAD_DIM + 8


def _sigmoid(x):
    return 1.0 / (1.0 + jnp.exp(-x))


def _split_bf16(x):
    hi = x.astype(BF16)
    lo = (x - hi.astype(F32)).astype(BF16)
    return hi, lo


def _dot(a, b):
    return jnp.dot(a, b, preferred_element_type=F32)


def _dot_f32(a, b_hi, b_lo):
    a_hi, a_lo = _split_bf16(a)
    return _dot(a_hi, b_hi) + _dot(a_lo, b_hi) + _dot(a_hi, b_lo)


def _segsum(x, ones_bd):
    hi, lo = _split_bf16(x)
    outs = []
    for j in range(x.shape[-1] // 128):
        sl = slice(j * 128, (j + 1) * 128)
        outs.append(_dot(hi[:, sl], ones_bd) + _dot(lo[:, sl], ones_bd))
    return jnp.concatenate(outs, axis=-1)


def _rms_scale(x):
    return lax.rsqrt(jnp.mean(x * x, axis=-1, keepdims=True) + RMS_EPS)


def _ffn_body(x_ref, gpre_ref, gpost_ref, wg_ref, wu_ref, wd_ref, o_ref, h_ref, *, n_f):
    f = pl.program_id(1)

    @pl.when(f == 0)
    def _():
        x = x_ref[...]
        h_ref[...] = (x * _rms_scale(x) * gpre_ref[...]).astype(BF16)
        o_ref[...] = jnp.zeros_like(o_ref)

    h = h_ref[...]
    g = _dot(h, wg_ref[...])
    u = _dot(h, wu_ref[...])
    act = (g * _sigmoid(g) * u).astype(BF16)
    o_ref[...] += _dot(act, wd_ref[...])

    @pl.when(f == n_f - 1)
    def _():
        y = o_ref[...]
        o_ref[...] = x_ref[...] + 0.5 * (y * _rms_scale(y) * gpost_ref[...])


def _ffn(x, g_pre, g_post, w_in, w_down):
    m = x.shape[0]
    tm = min(FFN_TM, m)
    n_m, n_f = m // tm, D_FF // FFN_TF
    return pl.pallas_call(
        functools.partial(_ffn_body, n_f=n_f),
        grid=(n_m, n_f),
        in_specs=[
            pl.BlockSpec((tm, D_MODEL), lambda i, f: (i, 0)),
            pl.BlockSpec((1, D_MODEL), lambda i, f: (0, 0)),
            pl.BlockSpec((1, D_MODEL), lambda i, f: (0, 0)),
            pl.BlockSpec((D_MODEL, FFN_TF), lambda i, f: (0, f)),
            pl.BlockSpec((D_MODEL, FFN_TF), lambda i, f: (0, f + n_f)),
            pl.BlockSpec((FFN_TF, D_MODEL), lambda i, f: (f, 0)),
        ],
        out_specs=pl.BlockSpec((tm, D_MODEL), lambda i, f: (i, 0)),
        out_shape=jax.ShapeDtypeStruct((m, D_MODEL), F32),
        scratch_shapes=[pltpu.VMEM((tm, D_MODEL), BF16)],
        compiler_params=pltpu.CompilerParams(
            dimension_semantics=("parallel", "arbitrary"), vmem_limit_bytes=VMEM_LIMIT),
        name="ffn",
    )(x, g_pre, g_post, w_in, w_in, w_down)


def _proj_body(x_ref, g_ref, w_head_ref, w_tail_ref, o_ref, h_ref, *, n_head):
    n = pl.program_id(1)

    @pl.when(n == 0)
    def _():
        x = x_ref[...]
        h_ref[...] = (x * _rms_scale(x) * g_ref[...]).astype(BF16)

    @pl.when(n < n_head)
    def _():
        o_ref[...] = _dot(h_ref[...], w_head_ref[...])

    @pl.when(n >= n_head)
    def _():
        o_ref[...] = _dot(h_ref[...], w_tail_ref[...])


def _proj(x, g, w_head, w_tail):
    m = x.shape[0]
    tm = min(PROJ_TM, m)
    n_head = COL_GA // PROJ_TN
    return pl.pallas_call(
        functools.partial(_proj_body, n_head=n_head),
        grid=(m // tm, PROJ_COLS // PROJ_TN),
        in_specs=[
            pl.BlockSpec((tm, D_MODEL), lambda i, n: (i, 0)),
            pl.BlockSpec((1, D_MODEL), lambda i, n: (0, 0)),
            pl.BlockSpec((D_MODEL, PROJ_TN), lambda i, n: (0, jnp.minimum(n, n_head - 1))),
            pl.BlockSpec((D_MODEL, PROJ_TN), lambda i, n: (0, jnp.maximum(n - n_head, 0))),
        ],
        out_specs=pl.BlockSpec((tm, PROJ_TN), lambda i, n: (i, n)),
        out_shape=jax.ShapeDtypeStruct((m, PROJ_COLS), F32),
        scratch_shapes=[pltpu.VMEM((tm, D_MODEL), BF16)],
        compiler_params=pltpu.CompilerParams(
            dimension_semantics=("parallel", "arbitrary"), vmem_limit_bytes=VMEM_LIMIT),
        name="proj",
    )(x, g, w_head, w_tail)


def _softmax_pv(parts):
    m = functools.reduce(jnp.maximum, [jnp.max(s, axis=-1, keepdims=True) for s, _ in parts])
    num, den = 0.0, 0.0
    for s, v in parts:
        p = jnp.exp(s - m)
        den = den + jnp.sum(p, axis=-1, keepdims=True)
        num = num + _dot(p.astype(BF16), v)
    return num / den


def _qk(q, k):
    return lax.dot_general(q, k, (((1,), (1,)), ((), ())), preferred_element_type=F32)


def _attn_prompt_body(q_ref, k_ref, v_ref, diag_ref, o_ref, kbf_ref, vbf_ref, bias_ref):
    c = pl.program_id(1)
    shift = jnp.maximum(ATT_LEAD - c, 0)

    @pl.when(c == 0)
    def _():
        kbf_ref[...] = k_ref[...].astype(BF16)
        vbf_ref[...] = v_ref[...].astype(BF16)

    @pl.when((pl.program_id(0) == 0) & (c <= ATT_LEAD))
    def _():
        key = lax.broadcasted_iota(jnp.int32, (ATT_KEYS, ATT_Q), 0) // CHUNK
        qry = lax.broadcasted_iota(jnp.int32, (ATT_KEYS, ATT_Q), 1) // CHUNK
        dist = (LEFT_CHUNKS + qry) - (key + shift * (ATT_Q // CHUNK))
        in_band = (dist >= 0) & (dist <= LEFT_CHUNKS)
        for h in range(HEADS):
            diag = jnp.broadcast_to(diag_ref[shift, pl.ds(h, 1), :], (ATT_Q, ATT_DIAG))
            toep = pltpu.roll(diag, ATT_DIAG - ATT_Q, axis=1, stride=1, stride_axis=0)
            bias_ref[shift, h] = jnp.where(in_band, toep[:, :ATT_KEYS].T, NEG_INF)

    start = pl.multiple_of(jnp.maximum(c - ATT_LEAD, 0) * ATT_Q, ATT_Q)
    lane = lax.broadcasted_iota(jnp.int32, (ATT_Q, 128), 1)
    for pair in range(HEADS // 2):
        sl = slice(pair * 128, (pair + 1) * 128)
        kp = kbf_ref[pl.ds(start, ATT_KEYS), sl]
        vp = vbf_ref[pl.ds(start, ATT_KEYS), sl]
        qp = q_ref[:, sl] * (HEAD_DIM ** -0.5)
        halves = []
        for half in range(2):
            qm = jnp.where((lane >= HEAD_DIM) == bool(half), qp, 0.0).astype(BF16)
            s = _qk(kp, qm) + bias_ref[shift, 2 * pair + half]
            p = jnp.exp(s - jnp.max(s, axis=0, keepdims=True))
            p = (p * (1.0 / jnp.sum(p, axis=0, keepdims=True))).astype(BF16)
            halves.append(lax.dot_general(p, vp, (((0,), (0,)), ((), ())),
                                          preferred_element_type=F32))
        o_ref[:, sl] = jnp.where(lane < HEAD_DIM, halves[0], halves[1]).astype(BF16)


def _attn_prompt(proj, diag, batch, seq):
    n_c = seq // ATT_Q
    once = dict(pipeline_mode=pl.Buffered(1))
    return pl.pallas_call(
        _attn_prompt_body,
        grid=(batch, n_c),
        in_specs=[
            pl.BlockSpec((ATT_Q, WIDTH), lambda b, c: (b * n_c + c, COL_Q // WIDTH)),
            pl.BlockSpec((seq, WIDTH), lambda b, c: (b, COL_K // WIDTH), **once),
            pl.BlockSpec((seq, WIDTH), lambda b, c: (b, COL_V // WIDTH), **once),
            pl.BlockSpec(diag.shape, lambda b, c: (0, 0, 0)),
        ],
        out_specs=pl.BlockSpec((ATT_Q, WIDTH), lambda b, c: (b * n_c + c, 0)),
        out_shape=jax.ShapeDtypeStruct((batch * seq, WIDTH), BF16),
        scratch_shapes=[pltpu.VMEM((seq, WIDTH), BF16), pltpu.VMEM((seq, WIDTH), BF16),
                        pltpu.VMEM((ATT_LEAD + 1, HEADS, ATT_KEYS, ATT_Q), F32)],
        compiler_params=pltpu.CompilerParams(
            dimension_semantics=("arbitrary", "arbitrary"), vmem_limit_bytes=VMEM_LIMIT_MAX),
        name="attn_prompt",
    )(proj, proj, proj, diag)


def _attn_sample_body(q_ref, kn_ref, vn_ref, ck_ref, cv_ref, bias_c_ref, bias_n_ref, o_ref):
    q = (q_ref[...] * (HEAD_DIM ** -0.5)).astype(BF16)
    kn = kn_ref[...].astype(BF16)
    vn = vn_ref[...].astype(BF16)
    outs = []
    for h in range(HEADS):
        sl = slice(h * HEAD_DIM, (h + 1) * HEAD_DIM)
        qh = q[:, sl]
        ck = ck_ref[0, :, sl].astype(BF16)
        cv = cv_ref[0, :, sl].astype(BF16)
        s_c = _qk(qh, ck) + bias_c_ref[h]
        s_n = _qk(qh, kn[:, sl]) + bias_n_ref[h]
        outs.append(_softmax_pv([(s_c, cv), (s_n, vn[:, sl])]))
    o_ref[...] = jnp.concatenate(outs, axis=-1).astype(BF16)


def _attn_sample(proj, row0, cache_k, cache_v, bias_c, bias_n, batch, t):
    n_past = cache_k.shape[1]
    blk0 = row0 // t
    return pl.pallas_call(
        _attn_sample_body,
        grid=(batch,),
        in_specs=[
            pl.BlockSpec((t, WIDTH), lambda b: (blk0 + b, COL_Q // WIDTH)),
            pl.BlockSpec((t, WIDTH), lambda b: (blk0 + b, COL_K // WIDTH)),
            pl.BlockSpec((t, WIDTH), lambda b: (blk0 + b, COL_V // WIDTH)),
            pl.BlockSpec((1, n_past, WIDTH), lambda b: (b, 0, 0)),
            pl.BlockSpec((1, n_past, WIDTH), lambda b: (b, 0, 0)),
            pl.BlockSpec((HEADS, t, n_past), lambda b: (0, 0, 0)),
            pl.BlockSpec((HEADS, t, t), lambda b: (0, 0, 0)),
        ],
        out_specs=pl.BlockSpec((t, WIDTH), lambda b: (b, 0)),
        out_shape=jax.ShapeDtypeStruct((batch * t, WIDTH), BF16),
        compiler_params=pltpu.CompilerParams(
            dimension_semantics=("parallel",), vmem_limit_bytes=VMEM_LIMIT),
        name="attn_sample",
    )(proj, proj, proj, cache_k, cache_v, bias_c, bias_n)


def _prep_body(p_ref, pl_ref, bp_ref, bl_ref, mu_ref, mul_ref, w0_ref, a0_ref, kk_ref, ka_ref, rk_ref,
               lnb_ref, wup_hi, wup_lo, aup_hi, aup_lo, gup_hi, gup_lo, bd_ref,
               r_o, w_o, k_o, v_o, kk_o, kka_o, g_o, cb_o):
    def shifted(p, first_row, mu):
        row = lax.broadcasted_iota(jnp.int32, p.shape, 0)
        prev = jnp.where(row == 0, first_row, pltpu.roll(p, 1, axis=0))
        return p + (prev - p) * mu

    xm = shifted(p_ref[...], bp_ref[0], mu_ref[...])
    xl = shifted(pl_ref[...], bl_ref[0], mul_ref[...])
    r = xm[:, 0:WIDTH]
    k = xm[:, WIDTH:2 * WIDTH]
    v = xm[:, 2 * WIDTH:3 * WIDTH]
    wd = xl[:, 0:LORA_PAD]
    ad = xl[:, LORA_PAD:2 * LORA_PAD]
    gd = xl[:, 2 * LORA_PAD:]

    z = -(w0_ref[...] + _dot_f32(jnp.tanh(wd), wup_hi[...], wup_lo[...]))
    softplus = jnp.maximum(z, 0.0) + jnp.log(1.0 + jnp.exp(-jnp.abs(z)))
    decay = jnp.exp(-jnp.exp(-softplus - 0.5))
    a = _sigmoid(a0_ref[...] + _dot_f32(ad, aup_hi[...], aup_lo[...]))
    g = _dot_f32(_sigmoid(gd), gup_hi[...], gup_lo[...])

    bd = bd_ref[...]
    kk = k * kk_ref[...]
    kk = kk / jnp.maximum(jnp.sqrt(_segsum(kk * kk, bd)), 1e-12)
    k2 = k * (1.0 + (a - 1.0) * ka_ref[...])
    bonus = _segsum(r * k2 * rk_ref[...], bd) * v

    r_o[...] = r
    w_o[...] = decay
    k_o[...] = k2
    v_o[...] = v
    kk_o[...] = kk
    kka_o[...] = kk * a
    g_o[...] = g
    cb_o[...] = lnb_ref[...] + bonus


def _prep(proj, row0, n_rows, tm, first_rkv, first_lora, params):
    n_t = n_rows // tm
    blk0 = row0 // tm
    vec = lambda w: pl.BlockSpec((1, w), lambda i: (0, 0))
    mat = lambda r: pl.BlockSpec((r, WIDTH), lambda i: (0, 0))
    out = pl.BlockSpec((tm, WIDTH), lambda i: (i, 0))
    return pl.pallas_call(
        _prep_body,
        grid=(n_t,),
        in_specs=[
            pl.BlockSpec((tm, 3 * WIDTH), lambda i: (blk0 + i, COL_RKV // (3 * WIDTH))),
            pl.BlockSpec((tm, LORA_COLS), lambda i: (blk0 + i, COL_LORA // LORA_COLS)),
            pl.BlockSpec((1, 1, 3 * WIDTH), lambda i: (i, 0, 0)),
            pl.BlockSpec((1, 1, LORA_COLS), lambda i: (i, 0, 0)),
            vec(3 * WIDTH), vec(LORA_COLS), vec(WIDTH), vec(WIDTH), vec(WIDTH), vec(WIDTH), vec(WIDTH),
            vec(WIDTH), mat(LORA_PAD), mat(LORA_PAD), mat(LORA_PAD), mat(LORA_PAD),
            mat(GATE_LORA), mat(GATE_LORA),
            pl.BlockSpec((128, 128), lambda i: (0, 0)),
        ],
        out_specs=[out] * 8,
        out_shape=[jax.ShapeDtypeStruct((n_rows, WIDTH), F32)] * 8,
        compiler_params=pltpu.CompilerParams(
            dimension_semantics=("parallel",), vmem_limit_bytes=VMEM_LIMIT),
        name="rwkv_prep",
    )(proj, proj, first_rkv, first_lora, *params)


def _scan_steps(s_ref, row, get_v, put_y, n_steps):
    W, KKA, K, R, KK = range(5)
    zero = jnp.zeros((KLO, 128), F32)

    sa0 = [zero, zero]
    for k in range(HEAD_DIM):
        sa0[k % 2] = sa0[k % 2] - s_ref[k] * row(KK, 0, k)

    def step(t, sa):
        v = get_v(t)
        t_next = jnp.minimum(t + 1, n_steps - 1)
        y = [zero, zero]
        sa_next = [zero, zero]
        for k in range(HEAD_DIM):
            s_new = s_ref[k] * row(W, t, k) + sa * row(KKA, t, k) + v * row(K, t, k)
            s_ref[k] = s_new
            y[k % 2] = y[k % 2] + s_new * row(R, t, k)
            sa_next[k % 2] = sa_next[k % 2] - s_new * row(KK, t_next, k)
        put_y(t, y[0] + y[1])
        return sa_next[0] + sa_next[1]

    lax.fori_loop(0, n_steps, step, sa0[0] + sa0[1])


def _scan_body(w_ref, kka_ref, k_ref, r_ref, kk_ref, v_ref, s0_ref, y_ref, st_ref, s_ref, rows_ref, *, tb):
    @pl.when(pl.program_id(1) == 0)
    def _():
        s_ref[...] = s0_ref[0]

    lane = lax.broadcasted_iota(jnp.int32, (tb * KLO, 128), 1)
    for a, ref in enumerate((w_ref, kka_ref, k_ref, r_ref, kk_ref)):
        x = ref[0].reshape(tb * KLO, 128)
        swapped = pltpu.roll(x, HEAD_DIM, axis=1)
        rows_ref[a, :, 0:KLO, :] = jnp.where(lane < HEAD_DIM, x, swapped).reshape(tb, KLO, 128)
        rows_ref[a, :, KLO:, :] = jnp.where(lane < HEAD_DIM, swapped, x).reshape(tb, KLO, 128)

    def row(a, t, k):
        return jnp.broadcast_to(rows_ref[a, t, pl.ds(k, 1), :], (KLO, 128))

    def put_y(t, y):
        y_ref[0, t] = y

    _scan_steps(s_ref, row, lambda t: v_ref[0, t], put_y, tb)

    @pl.when(pl.program_id(1) == pl.num_programs(1) - 1)
    def _():
        st_ref[0] = s_ref[...]


def _scan_tok_body(w_ref, kka_ref, k_ref, r_ref, kk_ref, v_ref, s0_ref, y_ref, st_ref,
                   s_ref, rows_ref, xt_ref, vt_ref, yt_ref, *, tb):
    @pl.when(pl.program_id(0) == 0)
    def _():
        s_ref[...] = s0_ref[0]
        xt_ref[...] = jnp.zeros_like(xt_ref)

    def head_rows(c):
        return pl.ds(c, HEADS, stride=XT_PITCH)

    def channel_major(ref):
        for b in range(LANE_GROUP_BATCH):
            xt = ref[b].T
            for h in range(HEADS):
                xt_ref[b, h * XT_PITCH:h * XT_PITCH + HEAD_DIM, 0:tb] = xt[h * HEAD_DIM:(h + 1) * HEAD_DIM]

    def lane_tile(chan):
        parts = [xt_ref[b, head_rows(c), :] for c in chan for b in range(LANE_GROUP_BATCH)]
        return jnp.concatenate(parts, axis=0).T[0:tb]

    for a, ref in enumerate((w_ref, kka_ref, k_ref, r_ref, kk_ref)):
        channel_major(ref)
        for k in range(HEAD_DIM):
            rows_ref[a, k] = lane_tile((k, k))
    channel_major(v_ref)
    for j in range(KLO):
        vt_ref[pl.ds(j, tb, stride=STEP_PITCH), :] = lane_tile((j, KLO + j))

    def row(a, t, k):
        return jnp.broadcast_to(rows_ref[a, k, pl.ds(t, 1), :], (KLO, 128))

    def step_rows(t):
        return pl.ds(pl.multiple_of(t * STEP_PITCH, 8), KLO)

    def put_y(t, y):
        yt_ref[step_rows(t), :] = y

    _scan_steps(s_ref, row, lambda t: vt_ref[step_rows(t), :], put_y, tb)

    for j in range(KLO):
        steps = yt_ref[pl.ds(j, tb, stride=STEP_PITCH), :]
        tile = jnp.concatenate([steps, jnp.zeros((128 - tb, 128), F32)], axis=0).T
        for half in range(2):
            for b in range(LANE_GROUP_BATCH):
                r0 = (half * LANE_GROUP_BATCH + b) * HEADS
                xt_ref[b, head_rows(half * KLO + j), :] = tile[r0:r0 + HEADS]
    for b in range(LANE_GROUP_BATCH):
        heads = [xt_ref[b, h * XT_PITCH:h * XT_PITCH + HEAD_DIM, 0:tb] for h in range(HEADS)]
        y_ref[b] = jnp.concatenate(heads, axis=0).T

    @pl.when(pl.program_id(0) == pl.num_programs(0) - 1)
    def _():
        st_ref[0] = s_ref[...]


def _scan_tok(ops, v, s0, tb):
    nb, t, _ = v.shape
    tok_spec = pl.BlockSpec((nb, tb, WIDTH), lambda j: (0, j, 0))
    st_spec = pl.BlockSpec((1, HEAD_DIM, KLO, 128), lambda j: (0, 0, 0, 0))
    return pl.pallas_call(
        functools.partial(_scan_tok_body, tb=tb),
        grid=(t // tb,),
        in_specs=[tok_spec] * 6 + [st_spec],
        out_specs=[tok_spec, st_spec],
        out_shape=[jax.ShapeDtypeStruct((nb, t, WIDTH), F32),
                   jax.ShapeDtypeStruct((1, HEAD_DIM, KLO, 128), F32)],
        scratch_shapes=[pltpu.VMEM((HEAD_DIM, KLO, 128), F32),
                        pltpu.VMEM((5, HEAD_DIM, tb, 128), F32),
                        pltpu.VMEM((nb, HEADS * XT_PITCH, 128), F32),
                        pltpu.VMEM((tb * STEP_PITCH, 128), F32),
                        pltpu.VMEM((tb * STEP_PITCH, 128), F32)],
        compiler_params=pltpu.CompilerParams(
            dimension_semantics=("arbitrary",), vmem_limit_bytes=VMEM_LIMIT),
        name="rwkv_scan_tok",
    )(*ops, v, s0)


def _scan(rows, v, s0, tb):
    g, t = v.shape[0], v.shape[1]
    row_spec = pl.BlockSpec((1, tb, KLO, 128), lambda i, j: (i, j, 0, 0))
    st_spec = pl.BlockSpec((1, HEAD_DIM, KLO, 128), lambda i, j: (i, 0, 0, 0))
    return pl.pallas_call(
        functools.partial(_scan_body, tb=tb),
        grid=(g, t // tb),
        in_specs=[row_spec] * 6 + [st_spec],
        out_specs=[row_spec, st_spec],
        out_shape=[jax.ShapeDtypeStruct((g, t, KLO, 128), F32),
                   jax.ShapeDtypeStruct((g, HEAD_DIM, KLO, 128), F32)],
        scratch_shapes=[pltpu.VMEM((HEAD_DIM, KLO, 128), F32),
                        pltpu.VMEM((5, tb, HEAD_DIM, 128), F32)],
        compiler_params=pltpu.CompilerParams(
            dimension_semantics=("parallel", "arbitrary"), vmem_limit_bytes=VMEM_LIMIT),
        name="rwkv_scan",
    )(*rows, v, s0)


def _rows_to_scan(x, n_batch, t):
    g = n_batch // LANE_GROUP_BATCH
    x = x.reshape(g, LANE_GROUP_BATCH, t, HEADS, 2, KLO)
    return x.transpose(0, 2, 5, 4, 1, 3).reshape(g, t, KLO, 128)


def _rows_from_scan(y, n_batch, t):
    g = n_batch // LANE_GROUP_BATCH
    y = y.reshape(g, t, KLO, 2, LANE_GROUP_BATCH, HEADS).transpose(0, 4, 1, 5, 3, 2)
    return y.reshape(n_batch * t, WIDTH)


def _state_to_scan(s, n_batch):
    g = n_batch // LANE_GROUP_BATCH
    s = s.reshape(g, LANE_GROUP_BATCH, HEADS, 2, KLO, HEAD_DIM)
    return s.transpose(0, 5, 4, 3, 1, 2).reshape(g, HEAD_DIM, KLO, 128)


def _state_from_scan(s, n_batch):
    g = n_batch // LANE_GROUP_BATCH
    s = s.reshape(g, HEAD_DIM, KLO, 2, LANE_GROUP_BATCH, HEADS)
    return s.transpose(0, 4, 5, 3, 2, 1).reshape(n_batch, HEADS, HEAD_DIM, HEAD_DIM)


def _merge_body(att_ref, y_ref, g_ref, cb_ref, ga_ref, gb_ref, x_ref, lnw_ref, gpost_ref, bd_ref,
                wa_ref, wr_ref, wo_ref, o_ref):
    bd = bd_ref[...]
    y = y_ref[...]
    d = y - _segsum(y, bd) * (1.0 / HEAD_DIM)
    var = _segsum(d * d, bd) * (1.0 / HEAD_DIM)
    rw = ((d * lax.rsqrt(var + GN_EPS) * lnw_ref[...] + cb_ref[...]) * g_ref[...]).astype(BF16)
    rw_o = _dot(rw, wr_ref[...])
    att_o = _dot(att_ref[...], wa_ref[...])
    mixed = (_sigmoid(ga_ref[...]) * att_o + _sigmoid(gb_ref[...]) * rw_o).astype(BF16)
    z = _dot(mixed, wo_ref[...])
    o_ref[...] = x_ref[...] + z * _rms_scale(z) * gpost_ref[...]


def _merge(att, y, g, cb, proj, x, ln_w, g_post, ones_bd, w_att, w_rwkv, w_out):
    m = x.shape[0]
    tm = min(MERGE_TM, m)
    tok = lambda w: pl.BlockSpec((tm, w), lambda i: (i, 0))
    const = lambda a: pl.BlockSpec(a.shape, lambda i: (0,) * a.ndim)
    return pl.pallas_call(
        _merge_body,
        grid=(m // tm,),
        in_specs=[
            tok(WIDTH), tok(WIDTH), tok(WIDTH), tok(WIDTH),
            pl.BlockSpec((tm, D_MODEL), lambda i: (i, COL_GA // D_MODEL)),
            pl.BlockSpec((tm, D_MODEL), lambda i: (i, COL_GB // D_MODEL)),
            tok(D_MODEL), const(ln_w), const(g_post), const(ones_bd),
            const(w_att), const(w_rwkv), const(w_out),
        ],
        out_specs=tok(D_MODEL),
        out_shape=jax.ShapeDtypeStruct((m, D_MODEL), F32),
        compiler_params=pltpu.CompilerParams(
            dimension_semantics=("parallel",), vmem_limit_bytes=VMEM_LIMIT),
        name="merge",
    )(att, y, g, cb, proj, proj, x, ln_w, g_post, ones_bd, w_att, w_rwkv, w_out)


def _pad_cols(x, width):
    return jnp.pad(x, ((0, 0), (0, width - x.shape[1])))


def _pad_rows(x, height):
    return jnp.pad(x, ((0, height - x.shape[0]), (0, 0)))


def _reorder_shift_cols(x):
    o = 3 * WIDTH
    wd = x[..., o:o + DECAY_LORA]
    ad = x[..., o + DECAY_LORA:o + DECAY_LORA + ICLR_LORA]
    gd = x[..., o + DECAY_LORA + ICLR_LORA:]
    pad = [(0, 0)] * (x.ndim - 1)
    lora = jnp.concatenate([jnp.pad(wd, pad + [(0, LORA_PAD - DECAY_LORA)]),
                            jnp.pad(ad, pad + [(0, LORA_PAD - ICLR_LORA)]), gd], axis=-1)
    return x[..., :o], lora


def _shift_row_from_proj(rows):
    lo = rows[:, COL_LORA:]
    return jnp.concatenate([rows[:, COL_RKV:COL_RKV + 3 * WIDTH], lo[:, :DECAY_LORA],
                            lo[:, LORA_PAD:LORA_PAD + ICLR_LORA], lo[:, 2 * LORA_PAD:]], axis=-1)


def _bias_diagonals(rel_bias):
    u = np.arange(ATT_DIAG)[None, :]
    s = np.arange(ATT_LEAD + 1)[:, None]
    idx = np.clip(ATT_KEYS - u - ATT_Q * s, -MAX_REL, MAX_REL) + MAX_REL
    tab = jnp.take(rel_bias, jnp.asarray(idx.reshape(-1), jnp.int32), axis=1)
    return tab.reshape(HEADS, ATT_LEAD + 1, ATT_DIAG).transpose(1, 0, 2)


def _bias_sample(rel_bias, t, n_past):
    rel = np.arange(t)[:, None] + n_past - np.arange(n_past + t)[None, :]
    idx = np.clip(rel, -MAX_REL, MAX_REL) + MAX_REL
    tab = jnp.take(rel_bias, jnp.asarray(idx.reshape(-1), jnp.int32), axis=1).reshape(HEADS, t, n_past + t)
    return tab[:, :, :n_past], tab[:, :, n_past:]


def _layer(x, n_batch, t, attend, first_rows, prep_tm, wkv0, scan_tb, wts):
    m = n_batch * t
    x = _ffn(x, *wts["ffn1"])
    proj = _proj(x, *wts["proj"])
    att = attend(proj)

    prep = _prep(proj, 0, m, prep_tm, *first_rows(proj), wts["prep"])
    r, wdec, k2, v, kk, kka, g, cb = prep
    ops = (wdec, kka, k2, r, kk)
    if n_batch == LANE_GROUP_BATCH and scan_tb % 64 == 0:
        tok = lambda a: a.reshape(n_batch, t, WIDTH)
        y, s_t = _scan_tok([tok(a) for a in ops], tok(v), _state_to_scan(wkv0, n_batch), scan_tb)
        y = y.reshape(m, WIDTH)
    else:
        rows = [_rows_to_scan(a, n_batch, t) for a in ops]
        y, s_t = _scan(rows, _rows_to_scan(v, n_batch, t), _state_to_scan(wkv0, n_batch), scan_tb)
        y = _rows_from_scan(y, n_batch, t)

    x = _merge(att, y, g, cb, proj, x, *wts["merge"])
    x = _ffn(x, *wts["ffn2"])

    last = jnp.concatenate([proj[(b + 1) * t - 1:(b + 1) * t] for b in range(n_batch)])
    return x, proj, _state_from_scan(s_t, n_batch), _shift_row_from_proj(last)[:, None]


def kernel(x_prompt, x_sample, cache_k, cache_v, state_wkv, state_shift, norm_ffn1_pre, norm_ffn1_post, w_ffn1_in, w_ffn1_down, norm_mix_pre, norm_mix_post, w_in, rel_bias, w_att_out, rwkv_mu, rwkv_w0, rwkv_w_up, rwkv_a0, rwkv_a_up, rwkv_g_up, rwkv_k_k, rwkv_k_a, rwkv_r_k, rwkv_ln_w, rwkv_ln_b, w_rwkv_out, w_out, norm_ffn2_pre, norm_ffn2_post, w_ffn2_in, w_ffn2_down):
    depth = w_in.shape[0]
    batch, seq, _ = x_prompt.shape
    dec_batch, dec_seq, _ = x_sample.shape
    n_past = cache_k.shape[2]
    m_p, m_s = batch * seq, dec_batch * dec_seq
    keep = min(PAST_BAND, seq)
    assert n_past == PAST_BAND and seq % ATT_Q == 0 and seq // ATT_Q > ATT_LEAD
    assert m_p % FFN_TM == 0 and m_p % PROJ_TM == 0 and m_p % MERGE_TM == 0 and seq % PREP_TM == 0
    assert seq % SCAN_TB == 0 and batch == LANE_GROUP_BATCH and dec_batch % LANE_GROUP_BATCH == 0

    x_p = x_prompt.reshape(m_p, D_MODEL)
    x_s = x_sample.reshape(m_s, D_MODEL)
    ones_bd = jnp.asarray(np.kron(np.eye(2), np.ones((HEAD_DIM, HEAD_DIM))), BF16)
    row = lambda v: v.reshape(1, -1)
    heads = lambda a, n, t: a.reshape(n, t, HEADS, HEAD_DIM)

    outs = {k: [] for k in ("kp", "vp", "wp", "sp", "ks", "vs", "ws", "ss")}
    for l in range(depth):
        w = w_in[l].astype(BF16)
        o_lora = COL_GA
        o_ga = o_lora + DECAY_LORA + ICLR_LORA + GATE_LORA
        w_tail = jnp.concatenate([
            w[:, o_ga:],
            _pad_cols(w[:, o_lora:o_lora + DECAY_LORA], LORA_PAD),
            _pad_cols(w[:, o_lora + DECAY_LORA:o_lora + DECAY_LORA + ICLR_LORA], LORA_PAD),
            w[:, o_lora + DECAY_LORA + ICLR_LORA:o_ga]], axis=1)
        mu_rkv, mu_lora = _reorder_shift_cols(rwkv_mu[l][None])
        wts = {
            "ffn1": (row(norm_ffn1_pre[l]), row(norm_ffn1_post[l]),
                     w_ffn1_in[l].astype(BF16), w_ffn1_down[l].astype(BF16)),
            "proj": (row(norm_mix_pre[l]), w, w_tail),
            "prep": (mu_rkv, mu_lora, row(rwkv_w0[l]), row(rwkv_a0[l]), row(rwkv_k_k[l]), row(rwkv_k_a[l]),
                     row(rwkv_r_k[l]), row(rwkv_ln_b[l]),
                     *_split_bf16(_pad_rows(rwkv_w_up[l], LORA_PAD)),
                     *_split_bf16(_pad_rows(rwkv_a_up[l], LORA_PAD)),
                     *_split_bf16(rwkv_g_up[l]), ones_bd),
            "merge": (row(rwkv_ln_w[l]), row(norm_mix_post[l]), ones_bd,
                      w_att_out[l].astype(BF16), w_rwkv_out[l].astype(BF16), w_out[l].astype(BF16)),
            "ffn2": (row(norm_ffn2_pre[l]), row(norm_ffn2_post[l]),
                     w_ffn2_in[l].astype(BF16), w_ffn2_down[l].astype(BF16)),
        }

        diag = _bias_diagonals(rel_bias[l])
        starts_seq = (jnp.arange(m_p // PREP_TM) % (seq // PREP_TM) == 0)[:, None]

        def first_rows_prompt(proj):
            def rows(col, width):
                last_of_tile = proj[PREP_TM - 1::PREP_TM, col:col + width]
                return jnp.where(starts_seq, 0.0, jnp.roll(last_of_tile, 1, axis=0))[:, None]
            return rows(COL_RKV, 3 * WIDTH), rows(COL_LORA, LORA_COLS)

        x_p, proj_p, wkv_p, sh_p = _layer(
            x_p, batch, seq, lambda proj: _attn_prompt(proj, diag, batch, seq), first_rows_prompt, PREP_TM,
            jnp.zeros((batch, HEADS, HEAD_DIM, HEAD_DIM), F32), SCAN_TB, wts)
        tail = lambda col: jnp.stack([proj_p[(b + 1) * seq - keep:(b + 1) * seq, col:col + WIDTH]
                                      for b in range(batch)])
        outs["kp"].append(heads(tail(COL_K), batch, keep))
        outs["vp"].append(heads(tail(COL_V), batch, keep))
        outs["wp"].append(wkv_p.astype(state_wkv.dtype))
        outs["sp"].append(sh_p)

        ck = cache_k[l].reshape(dec_batch, n_past, WIDTH)
        cv = cache_v[l].reshape(dec_batch, n_past, WIDTH)
        bias_c, bias_n = _bias_sample(rel_bias[l], dec_seq, n_past)
        x_s, proj_s, wkv_s, sh_s = _layer(
            x_s, dec_batch, dec_seq,
            lambda proj: _attn_sample(proj, 0, ck, cv, bias_c, bias_n, dec_batch, dec_seq),
            lambda proj: _reorder_shift_cols(state_shift[l]), dec_seq,
            state_wkv[l].astype(F32), dec_seq, wts)
        outs["ks"].append(heads(proj_s[:, COL_K:COL_K + WIDTH], dec_batch, dec_seq))
        outs["vs"].append(heads(proj_s[:, COL_V:COL_V + WIDTH], dec_batch, dec_seq))
        outs["ws"].append(wkv_s.astype(state_wkv.dtype))
        outs["ss"].append(sh_s)

    st = lambda k: jnp.stack(outs[k])
    return (x_p.reshape(batch, seq, D_MODEL), x_s.reshape(dec_batch, dec_seq, D_MODEL),
            st("kp"), st("vp"), st("wp"), st("sp"), st("ks"), st("vs"), st("ws"), st("ss"))
```

```python
import functools

import numpy as np
import jax
import jax.numpy as jnp
from jax import lax
from jax.experimental import pallas as pl
from jax.experimental.pallas import tpu as pltpu

F32 = jnp.float32
BF16 = jnp.bfloat16

D_MODEL = 2048
CHUNK = 64
LEFT_CHUNKS = 8
PAST_BAND = LEFT_CHUNKS * CHUNK
BAND = PAST_BAND + CHUNK
HEAD_DIM = 64
HEADS = 16
WIDTH = HEADS * HEAD_DIM
DECAY_LORA = 96
ICLR_LORA = 96
GATE_LORA = 256
LORA_PAD = 128
LORA_COLS = 2 * LORA_PAD + GATE_LORA
D_FF = 5632
MAX_REL = 128
RMS_EPS = 1e-6
GN_EPS = 64e-5
NEG_INF = -1e30

COL_Q, COL_K, COL_V = 0, WIDTH, 2 * WIDTH
COL_RKV = 3 * WIDTH
COL_GA = 6 * WIDTH
COL_GB = COL_GA + D_MODEL
COL_LORA = COL_GB + D_MODEL
PROJ_COLS = COL_LORA + LORA_COLS

VMEM_LIMIT = 56 * 1024 * 1024
VMEM_LIMIT_MAX = 60 * 1024 * 1024

FFN_TM = 512
FFN_TF = 512
PROJ_TM = 1024
PROJ_TN = 1536
MERGE_TM = 256
PREP_TM = 256
SCAN_TB = 128
ATT_Q = 2 * CHUNK
ATT_KEYS = PAST_BAND + ATT_Q
ATT_LEAD = PAST_BAND // ATT_Q
ATT_DIAG = ATT_KEYS + ATT_Q
LANE_GROUP_BATCH = 4
KLO = HEAD_DIM // 2
STEP_PITCH = KLO + 8
XT_PITCH = HEAD_DIM + 8


def _sigmoid(x):
    return 1.0 / (1.0 + jnp.exp(-x))


def _split_bf16(x):
    hi = x.astype(BF16)
    lo = (x - hi.astype(F32)).astype(BF16)
    return hi, lo


def _dot(a, b):
    return jnp.dot(a, b, preferred_element_type=F32)


def _dot_f32(a, b_hi, b_lo):
    a_hi, a_lo = _split_bf16(a)
    return _dot(a_hi, b_hi) + _dot(a_lo, b_hi) + _dot(a_hi, b_lo)


def _segsum(x, ones_bd):
    hi, lo = _split_bf16(x)
    outs = []
    for j in range(x.shape[-1] // 128):
        sl = slice(j * 128, (j + 1) * 128)
        outs.append(_dot(hi[:, sl], ones_bd) + _dot(lo[:, sl], ones_bd))
    return jnp.concatenate(outs, axis=-1)


def _rms_scale(x):
    return lax.rsqrt(jnp.mean(x * x, axis=-1, keepdims=True) + RMS_EPS)


def _ffn_body(x_ref, gpre_ref, gpost_ref, wg_ref, wu_ref, wd_ref, o_ref, h_ref, *, n_f):
    f = pl.program_id(1)

    @pl.when(f == 0)
    def _():
        x = x_ref[...]
        h_ref[...] = (x * _rms_scale(x) * gpre_ref[...]).astype(BF16)
        o_ref[...] = jnp.zeros_like(o_ref)

    h = h_ref[...]
    g = _dot(h, wg_ref[...])
    u = _dot(h, wu_ref[...])
    act = (g * _sigmoid(g) * u).astype(BF16)
    o_ref[...] += _dot(act, wd_ref[...])

    @pl.when(f == n_f - 1)
    def _():
        y = o_ref[...]
        o_ref[...] = x_ref[...] + 0.5 * (y * _rms_scale(y) * gpost_ref[...])


def _ffn(x, g_pre, g_post, w_in, w_down):
    m = x.shape[0]
    tm = min(FFN_TM, m)
    n_m, n_f = m // tm, D_FF // FFN_TF
    return pl.pallas_call(
        functools.partial(_ffn_body, n_f=n_f),
        grid=(n_m, n_f),
        in_specs=[
            pl.BlockSpec((tm, D_MODEL), lambda i, f: (i, 0)),
            pl.BlockSpec((1, D_MODEL), lambda i, f: (0, 0)),
            pl.BlockSpec((1, D_MODEL), lambda i, f: (0, 0)),
            pl.BlockSpec((D_MODEL, FFN_TF), lambda i, f: (0, f)),
            pl.BlockSpec((D_MODEL, FFN_TF), lambda i, f: (0, f + n_f)),
            pl.BlockSpec((FFN_TF, D_MODEL), lambda i, f: (f, 0)),
        ],
        out_specs=pl.BlockSpec((tm, D_MODEL), lambda i, f: (i, 0)),
        out_shape=jax.ShapeDtypeStruct((m, D_MODEL), F32),
        scratch_shapes=[pltpu.VMEM((tm, D_MODEL), BF16)],
        compiler_params=pltpu.CompilerParams(
            dimension_semantics=("parallel", "arbitrary"), vmem_limit_bytes=VMEM_LIMIT),
        name="ffn",
    )(x, g_pre, g_post, w_in, w_in, w_down)


def _proj_body(x_ref, g_ref, w_ref, o_ref, h_ref):
    @pl.when(pl.program_id(1) == 0)
    def _():
        x = x_ref[...]
        h_ref[...] = (x * _rms_scale(x) * g_ref[...]).astype(BF16)

    o_ref[...] = _dot(h_ref[...], w_ref[...])


def _proj(x, g, w):
    m = x.shape[0]
    tm = min(PROJ_TM, m)
    return pl.pallas_call(
        _proj_body,
        grid=(m // tm, PROJ_COLS // PROJ_TN),
        in_specs=[
            pl.BlockSpec((tm, D_MODEL), lambda i, n: (i, 0)),
            pl.BlockSpec((1, D_MODEL), lambda i, n: (0, 0)),
            pl.BlockSpec((D_MODEL, PROJ_TN), lambda i, n: (0, n)),
        ],
        out_specs=pl.BlockSpec((tm, PROJ_TN), lambda i, n: (i, n)),
        out_shape=jax.ShapeDtypeStruct((m, PROJ_COLS), F32),
        scratch_shapes=[pltpu.VMEM((tm, D_MODEL), BF16)],
        compiler_params=pltpu.CompilerParams(
            dimension_semantics=("parallel", "arbitrary"), vmem_limit_bytes=VMEM_LIMIT),
        name="proj",
    )(x, g, w)


def _softmax_pv(parts):
    m = functools.reduce(jnp.maximum, [jnp.max(s, axis=-1, keepdims=True) for s, _ in parts])
    num, den = 0.0, 0.0
    for s, v in parts:
        p = jnp.exp(s - m)
        den = den + jnp.sum(p, axis=-1, keepdims=True)
        num = num + _dot(p.astype(BF16), v)
    return num / den


def _qk(q, k):
    return lax.dot_general(q, k, (((1,), (1,)), ((), ())), preferred_element_type=F32)


def _attn_prompt_body(q_ref, k_ref, v_ref, diag_ref, o_ref, kbf_ref, vbf_ref, bias_ref):
    c = pl.program_id(1)
    shift = jnp.maximum(ATT_LEAD - c, 0)

    @pl.when(c == 0)
    def _():
        kbf_ref[...] = k_ref[...].astype(BF16)
        vbf_ref[...] = v_ref[...].astype(BF16)

    @pl.when((pl.program_id(0) == 0) & (c <= ATT_LEAD))
    def _():
        key = lax.broadcasted_iota(jnp.int32, (ATT_KEYS, ATT_Q), 0) // CHUNK
        qry = lax.broadcasted_iota(jnp.int32, (ATT_KEYS, ATT_Q), 1) // CHUNK
        dist = (LEFT_CHUNKS + qry) - (key + shift * (ATT_Q // CHUNK))
        in_band = (dist >= 0) & (dist <= LEFT_CHUNKS)
        for h in range(HEADS):
            diag = jnp.broadcast_to(diag_ref[shift, pl.ds(h, 1), :], (ATT_Q, ATT_DIAG))
            toep = pltpu.roll(diag, ATT_DIAG - ATT_Q, axis=1, stride=1, stride_axis=0)
            bias_ref[shift, h] = jnp.where(in_band, toep[:, :ATT_KEYS].T, NEG_INF)

    start = pl.multiple_of(jnp.maximum(c - ATT_LEAD, 0) * ATT_Q, ATT_Q)
    lane = lax.broadcasted_iota(jnp.int32, (ATT_Q, 128), 1)
    for pair in range(HEADS // 2):
        sl = slice(pair * 128, (pair + 1) * 128)
        kp = kbf_ref[pl.ds(start, ATT_KEYS), sl]
        vp = vbf_ref[pl.ds(start, ATT_KEYS), sl]
        qp = q_ref[:, sl] * (HEAD_DIM ** -0.5)
        halves = []
        for half in range(2):
            qm = jnp.where((lane >= HEAD_DIM) == bool(half), qp, 0.0).astype(BF16)
            s = _qk(kp, qm) + bias_ref[shift, 2 * pair + half]
            p = jnp.exp(s - jnp.max(s, axis=0, keepdims=True))
            p = (p * (1.0 / jnp.sum(p, axis=0, keepdims=True))).astype(BF16)
            halves.append(lax.dot_general(p, vp, (((0,), (0,)), ((), ())),
                                          preferred_element_type=F32))
        o_ref[:, sl] = jnp.where(lane < HEAD_DIM, halves[0], halves[1]).astype(BF16)


def _attn_prompt(proj, diag, batch, seq):
    n_c = seq // ATT_Q
    once = dict(pipeline_mode=pl.Buffered(1))
    return pl.pallas_call(
        _attn_prompt_body,
        grid=(batch, n_c),
        in_specs=[
            pl.BlockSpec((ATT_Q, WIDTH), lambda b, c: (b * n_c + c, COL_Q // WIDTH)),
            pl.BlockSpec((seq, WIDTH), lambda b, c: (b, COL_K // WIDTH), **once),
            pl.BlockSpec((seq, WIDTH), lambda b, c: (b, COL_V // WIDTH), **once),
            pl.BlockSpec(diag.shape, lambda b, c: (0, 0, 0)),
        ],
        out_specs=pl.BlockSpec((ATT_Q, WIDTH), lambda b, c: (b * n_c + c, 0)),
        out_shape=jax.ShapeDtypeStruct((batch * seq, WIDTH), BF16),
        scratch_shapes=[pltpu.VMEM((seq, WIDTH), BF16), pltpu.VMEM((seq, WIDTH), BF16),
                        pltpu.VMEM((ATT_LEAD + 1, HEADS, ATT_KEYS, ATT_Q), F32)],
        compiler_params=pltpu.CompilerParams(
            dimension_semantics=("arbitrary", "arbitrary"), vmem_limit_bytes=VMEM_LIMIT_MAX),
        name="attn_prompt",
    )(proj, proj, proj, diag)


def _attn_sample_body(q_ref, kn_ref, vn_ref, ck_ref, cv_ref, bias_c_ref, bias_n_ref, o_ref):
    q = (q_ref[...] * (HEAD_DIM ** -0.5)).astype(BF16)
    kn = kn_ref[...].astype(BF16)
    vn = vn_ref[...].astype(BF16)
    outs = []
    for h in range(HEADS):
        sl = slice(h * HEAD_DIM, (h + 1) * HEAD_DIM)
        qh = q[:, sl]
        ck = ck_ref[0, :, sl].astype(BF16)
        cv = cv_ref[0, :, sl].astype(BF16)
        s_c = _qk(qh, ck) + bias_c_ref[h]
        s_n = _qk(qh, kn[:, sl]) + bias_n_ref[h]
        outs.append(_softmax_pv([(s_c, cv), (s_n, vn[:, sl])]))
    o_ref[...] = jnp.concatenate(outs, axis=-1).astype(BF16)


def _attn_sample(proj, row0, cache_k, cache_v, bias_c, bias_n, batch, t):
    n_past = cache_k.shape[1]
    blk0 = row0 // t
    return pl.pallas_call(
        _attn_sample_body,
        grid=(batch,),
        in_specs=[
            pl.BlockSpec((t, WIDTH), lambda b: (blk0 + b, COL_Q // WIDTH)),
            pl.BlockSpec((t, WIDTH), lambda b: (blk0 + b, COL_K // WIDTH)),
            pl.BlockSpec((t, WIDTH), lambda b: (blk0 + b, COL_V // WIDTH)),
            pl.BlockSpec((1, n_past, WIDTH), lambda b: (b, 0, 0)),
            pl.BlockSpec((1, n_past, WIDTH), lambda b: (b, 0, 0)),
            pl.BlockSpec((HEADS, t, n_past), lambda b: (0, 0, 0)),
            pl.BlockSpec((HEADS, t, t), lambda b: (0, 0, 0)),
        ],
        out_specs=pl.BlockSpec((t, WIDTH), lambda b: (b, 0)),
        out_shape=jax.ShapeDtypeStruct((batch * t, WIDTH), BF16),
        compiler_params=pltpu.CompilerParams(
            dimension_semantics=("parallel",), vmem_limit_bytes=VMEM_LIMIT),
        name="attn_sample",
    )(proj, proj, proj, cache_k, cache_v, bias_c, bias_n)


def _prep_body(p_ref, pl_ref, bp_ref, bl_ref, mu_ref, mul_ref, w0_ref, a0_ref, kk_ref, ka_ref, rk_ref,
               lnb_ref, wup_hi, wup_lo, aup_hi, aup_lo, gup_hi, gup_lo, bd_ref,
               r_o, w_o, k_o, v_o, kk_o, kka_o, g_o, cb_o):
    def shifted(p, first_row, mu):
        row = lax.broadcasted_iota(jnp.int32, p.shape, 0)
        prev = jnp.where(row == 0, first_row, pltpu.roll(p, 1, axis=0))
        return p + (prev - p) * mu

    xm = shifted(p_ref[...], bp_ref[0], mu_ref[...])
    xl = shifted(pl_ref[...], bl_ref[0], mul_ref[...])
    r = xm[:, 0:WIDTH]
    k = xm[:, WIDTH:2 * WIDTH]
    v = xm[:, 2 * WIDTH:3 * WIDTH]
    wd = xl[:, 0:LORA_PAD]
    ad = xl[:, LORA_PAD:2 * LORA_PAD]
    gd = xl[:, 2 * LORA_PAD:]

    z = -(w0_ref[...] + _dot_f32(jnp.tanh(wd), wup_hi[...], wup_lo[...]))
    softplus = jnp.maximum(z, 0.0) + jnp.log(1.0 + jnp.exp(-jnp.abs(z)))
    decay = jnp.exp(-jnp.exp(-softplus - 0.5))
    a = _sigmoid(a0_ref[...] + _dot_f32(ad, aup_hi[...], aup_lo[...]))
    g = _dot_f32(_sigmoid(gd), gup_hi[...], gup_lo[...])

    bd = bd_ref[...]
    kk = k * kk_ref[...]
    kk = kk / jnp.maximum(jnp.sqrt(_segsum(kk * kk, bd)), 1e-12)
    k2 = k * (1.0 + (a - 1.0) * ka_ref[...])
    bonus = _segsum(r * k2 * rk_ref[...], bd) * v

    r_o[...] = r
    w_o[...] = decay
    k_o[...] = k2
    v_o[...] = v
    kk_o[...] = kk
    kka_o[...] = kk * a
    g_o[...] = g
    cb_o[...] = lnb_ref[...] + bonus


def _prep(proj, row0, n_rows, tm, first_rkv, first_lora, params):
    n_t = n_rows // tm
    blk0 = row0 // tm
    vec = lambda w: pl.BlockSpec((1, w), lambda i: (0, 0))
    mat = lambda r: pl.BlockSpec((r, WIDTH), lambda i: (0, 0))
    out = pl.BlockSpec((tm, WIDTH), lambda i: (i, 0))
    return pl.pallas_call(
        _prep_body,
        grid=(n_t,),
        in_specs=[
            pl.BlockSpec((tm, 3 * WIDTH), lambda i: (blk0 + i, COL_RKV // (3 * WIDTH))),
            pl.BlockSpec((tm, LORA_COLS), lambda i: (blk0 + i, COL_LORA // LORA_COLS)),
            pl.BlockSpec((1, 1, 3 * WIDTH), lambda i: (i, 0, 0)),
            pl.BlockSpec((1, 1, LORA_COLS), lambda i: (i, 0, 0)),
            vec(3 * WIDTH), vec(LORA_COLS), vec(WIDTH), vec(WIDTH), vec(WIDTH), vec(WIDTH), vec(WIDTH),
            vec(WIDTH), mat(LORA_PAD), mat(LORA_PAD), mat(LORA_PAD), mat(LORA_PAD),
            mat(GATE_LORA), mat(GATE_LORA),
            pl.BlockSpec((128, 128), lambda i: (0, 0)),
        ],
        out_specs=[out] * 8,
        out_shape=[jax.ShapeDtypeStruct((n_rows, WIDTH), F32)] * 8,
        compiler_params=pltpu.CompilerParams(
            dimension_semantics=("parallel",), vmem_limit_bytes=VMEM_LIMIT),
        name="rwkv_prep",
    )(proj, proj, first_rkv, first_lora, *params)


def _scan_steps(s_ref, row, get_v, put_y, n_steps):
    W, KKA, K, R, KK = range(5)
    zero = jnp.zeros((KLO, 128), F32)

    sa0 = [zero, zero]
    for k in range(HEAD_DIM):
        sa0[k % 2] = sa0[k % 2] - s_ref[k] * row(KK, 0, k)

    def step(t, sa):
        v = get_v(t)
        t_next = jnp.minimum(t + 1, n_steps - 1)
        y = [zero, zero]
        sa_next = [zero, zero]
        for k in range(HEAD_DIM):
            s_new = s_ref[k] * row(W, t, k) + sa * row(KKA, t, k) + v * row(K, t, k)
            s_ref[k] = s_new
            y[k % 2] = y[k % 2] + s_new * row(R, t, k)
            sa_next[k % 2] = sa_next[k % 2] - s_new * row(KK, t_next, k)
        put_y(t, y[0] + y[1])
        return sa_next[0] + sa_next[1]

    lax.fori_loop(0, n_steps, step, sa0[0] + sa0[1])


def _scan_body(w_ref, kka_ref, k_ref, r_ref, kk_ref, v_ref, s0_ref, y_ref, st_ref, s_ref, rows_ref, *, tb):
    @pl.when(pl.program_id(1) == 0)
    def _():
        s_ref[...] = s0_ref[0]

    lane = lax.broadcasted_iota(jnp.int32, (tb * KLO, 128), 1)
    for a, ref in enumerate((w_ref, kka_ref, k_ref, r_ref, kk_ref)):
        x = ref[0].reshape(tb * KLO, 128)
        swapped = pltpu.roll(x, HEAD_DIM, axis=1)
        rows_ref[a, :, 0:KLO, :] = jnp.where(lane < HEAD_DIM, x, swapped).reshape(tb, KLO, 128)
        rows_ref[a, :, KLO:, :] = jnp.where(lane < HEAD_DIM, swapped, x).reshape(tb, KLO, 128)

    def row(a, t, k):
        return jnp.broadcast_to(rows_ref[a, t, pl.ds(k, 1), :], (KLO, 128))

    def put_y(t, y):
        y_ref[0, t] = y

    _scan_steps(s_ref, row, lambda t: v_ref[0, t], put_y, tb)

    @pl.when(pl.program_id(1) == pl.num_programs(1) - 1)
    def _():
        st_ref[0] = s_ref[...]


def _scan_tok_body(w_ref, kka_ref, k_ref, r_ref, kk_ref, v_ref, s0_ref, y_ref, st_ref,
                   s_ref, rows_ref, xt_ref, vt_ref, yt_ref, *, tb):
    @pl.when(pl.program_id(0) == 0)
    def _():
        s_ref[...] = s0_ref[0]
        xt_ref[...] = jnp.zeros_like(xt_ref)

    def head_rows(c):
        return pl.ds(c, HEADS, stride=XT_PITCH)

    def channel_major(ref):
        for b in range(LANE_GROUP_BATCH):
            xt = ref[b].T
            for h in range(HEADS):
                xt_ref[b, h * XT_PITCH:h * XT_PITCH + HEAD_DIM, 0:tb] = xt[h * HEAD_DIM:(h + 1) * HEAD_DIM]

    def lane_tile(chan):
        parts = [xt_ref[b, head_rows(c), :] for c in chan for b in range(LANE_GROUP_BATCH)]
        return jnp.concatenate(parts, axis=0).T[0:tb]

    for a, ref in enumerate((w_ref, kka_ref, k_ref, r_ref, kk_ref)):
        channel_major(ref)
        for k in range(HEAD_DIM):
            rows_ref[a, k] = lane_tile((k, k))
    channel_major(v_ref)
    for j in range(KLO):
        vt_ref[pl.ds(j, tb, stride=STEP_PITCH), :] = lane_tile((j, KLO + j))

    def row(a, t, k):
        return jnp.broadcast_to(rows_ref[a, k, pl.ds(t, 1), :], (KLO, 128))

    def step_rows(t):
        return pl.ds(pl.multiple_of(t * STEP_PITCH, 8), KLO)

    def put_y(t, y):
        yt_ref[step_rows(t), :] = y

    _scan_steps(s_ref, row, lambda t: vt_ref[step_rows(t), :], put_y, tb)

    for j in range(KLO):
        steps = yt_ref[pl.ds(j, tb, stride=STEP_PITCH), :]
        if tb < 128:
            steps = jnp.concatenate([steps, jnp.zeros((128 - tb, 128), F32)], axis=0)
        tile = steps.T
        for half in range(2):
            for b in range(LANE_GROUP_BATCH):
                r0 = (half * LANE_GROUP_BATCH + b) * HEADS
                xt_ref[b, head_rows(half * KLO + j), :] = tile[r0:r0 + HEADS]
    for b in range(LANE_GROUP_BATCH):
        heads = [xt_ref[b, h * XT_PITCH:h * XT_PITCH + HEAD_DIM, 0:tb] for h in range(HEADS)]
        y_ref[b] = jnp.concatenate(heads, axis=0).T

    @pl.when(pl.program_id(0) == pl.num_programs(0) - 1)
    def _():
        st_ref[0] = s_ref[...]


def _scan_tok(ops, v, s0, tb):
    nb, t, _ = v.shape
    tok_spec = pl.BlockSpec((nb, tb, WIDTH), lambda j: (0, j, 0))
    in_spec = pl.BlockSpec((nb, tb, WIDTH), lambda j: (0, j, 0), pipeline_mode=pl.Buffered(1))
    st_spec = pl.BlockSpec((1, HEAD_DIM, KLO, 128), lambda j: (0, 0, 0, 0))
    return pl.pallas_call(
        functools.partial(_scan_tok_body, tb=tb),
        grid=(t // tb,),
        in_specs=[in_spec] * 6 + [st_spec],
        out_specs=[tok_spec, st_spec],
        out_shape=[jax.ShapeDtypeStruct((nb, t, WIDTH), F32),
                   jax.ShapeDtypeStruct((1, HEAD_DIM, KLO, 128), F32)],
        scratch_shapes=[pltpu.VMEM((HEAD_DIM, KLO, 128), F32),
                        pltpu.VMEM((5, HEAD_DIM, tb, 128), F32),
                        pltpu.VMEM((nb, HEADS * XT_PITCH, 128), F32),
                        pltpu.VMEM((tb * STEP_PITCH, 128), F32),
                        pltpu.VMEM((tb * STEP_PITCH, 128), F32)],
        compiler_params=pltpu.CompilerParams(
            dimension_semantics=("arbitrary",), vmem_limit_bytes=VMEM_LIMIT),
        name="rwkv_scan_tok",
    )(*ops, v, s0)


def _scan(rows, v, s0, tb):
    g, t = v.shape[0], v.shape[1]
    row_spec = pl.BlockSpec((1, tb, KLO, 128), lambda i, j: (i, j, 0, 0))
    st_spec = pl.BlockSpec((1, HEAD_DIM, KLO, 128), lambda i, j: (i, 0, 0, 0))
    return pl.pallas_call(
        functools.partial(_scan_body, tb=tb),
        grid=(g, t // tb),
        in_specs=[row_spec] * 6 + [st_spec],
        out_specs=[row_spec, st_spec],
        out_shape=[jax.ShapeDtypeStruct((g, t, KLO, 128), F32),
                   jax.ShapeDtypeStruct((g, HEAD_DIM, KLO, 128), F32)],
        scratch_shapes=[pltpu.VMEM((HEAD_DIM, KLO, 128), F32),
                        pltpu.VMEM((5, tb, HEAD_DIM, 128), F32)],
        compiler_params=pltpu.CompilerParams(
            dimension_semantics=("parallel", "arbitrary"), vmem_limit_bytes=VMEM_LIMIT),
        name="rwkv_scan",
    )(*rows, v, s0)


def _rows_to_scan(x, n_batch, t):
    g = n_batch // LANE_GROUP_BATCH
    x = x.reshape(g, LANE_GROUP_BATCH, t, HEADS, 2, KLO)
    return x.transpose(0, 2, 5, 4, 1, 3).reshape(g, t, KLO, 128)


def _rows_from_scan(y, n_batch, t):
    g = n_batch // LANE_GROUP_BATCH
    y = y.reshape(g, t, KLO, 2, LANE_GROUP_BATCH, HEADS).transpose(0, 4, 1, 5, 3, 2)
    return y.reshape(n_batch * t, WIDTH)


def _state_to_scan(s, n_batch):
    g = n_batch // LANE_GROUP_BATCH
    s = s.reshape(g, LANE_GROUP_BATCH, HEADS, 2, KLO, HEAD_DIM)
    return s.transpose(0, 5, 4, 3, 1, 2).reshape(g, HEAD_DIM, KLO, 128)


def _state_from_scan(s, n_batch):
    g = n_batch // LANE_GROUP_BATCH
    s = s.reshape(g, HEAD_DIM, KLO, 2, LANE_GROUP_BATCH, HEADS)
    return s.transpose(0, 4, 5, 3, 2, 1).reshape(n_batch, HEADS, HEAD_DIM, HEAD_DIM)


def _merge_body(att_ref, y_ref, g_ref, cb_ref, ga_ref, gb_ref, x_ref, lnw_ref, gpost_ref, bd_ref,
                wa_ref, wr_ref, wo_ref, o_ref):
    bd = bd_ref[...]
    y = y_ref[...]
    d = y - _segsum(y, bd) * (1.0 / HEAD_DIM)
    var = _segsum(d * d, bd) * (1.0 / HEAD_DIM)
    rw = ((d * lax.rsqrt(var + GN_EPS) * lnw_ref[...] + cb_ref[...]) * g_ref[...]).astype(BF16)
    rw_o = _dot(rw, wr_ref[...])
    att_o = _dot(att_ref[...], wa_ref[...])
    mixed = (_sigmoid(ga_ref[...]) * att_o + _sigmoid(gb_ref[...]) * rw_o).astype(BF16)
    z = _dot(mixed, wo_ref[...])
    o_ref[...] = x_ref[...] + z * _rms_scale(z) * gpost_ref[...]


def _merge(att, y, g, cb, proj, x, ln_w, g_post, ones_bd, w_att, w_rwkv, w_out):
    m = x.shape[0]
    tm = min(MERGE_TM, m)
    tok = lambda w: pl.BlockSpec((tm, w), lambda i: (i, 0))
    const = lambda a: pl.BlockSpec(a.shape, lambda i: (0,) * a.ndim)
    return pl.pallas_call(
        _merge_body,
        grid=(m // tm,),
        in_specs=[
            tok(WIDTH), tok(WIDTH), tok(WIDTH), tok(WIDTH),
            pl.BlockSpec((tm, D_MODEL), lambda i: (i, COL_GA // D_MODEL)),
            pl.BlockSpec((tm, D_MODEL), lambda i: (i, COL_GB // D_MODEL)),
            tok(D_MODEL), const(ln_w), const(g_post), const(ones_bd),
            const(w_att), const(w_rwkv), const(w_out),
        ],
        out_specs=tok(D_MODEL),
        out_shape=jax.ShapeDtypeStruct((m, D_MODEL), F32),
        compiler_params=pltpu.CompilerParams(
            dimension_semantics=("parallel",), vmem_limit_bytes=VMEM_LIMIT),
        name="merge",
    )(att, y, g, cb, proj, proj, x, ln_w, g_post, ones_bd, w_att, w_rwkv, w_out)


def _pad_cols(x, width):
    return jnp.pad(x, ((0, 0), (0, width - x.shape[1])))


def _pad_rows(x, height):
    return jnp.pad(x, ((0, height - x.shape[0]), (0, 0)))


def _reorder_shift_cols(x):
    o = 3 * WIDTH
    wd = x[..., o:o + DECAY_LORA]
    ad = x[..., o + DECAY_LORA:o + DECAY_LORA + ICLR_LORA]
    gd = x[..., o + DECAY_LORA + ICLR_LORA:]
    pad = [(0, 0)] * (x.ndim - 1)
    lora = jnp.concatenate([jnp.pad(wd, pad + [(0, LORA_PAD - DECAY_LORA)]),
                            jnp.pad(ad, pad + [(0, LORA_PAD - ICLR_LORA)]), gd], axis=-1)
    return x[..., :o], lora


def _shift_row_from_proj(rows):
    lo = rows[:, COL_LORA:]
    return jnp.concatenate([rows[:, COL_RKV:COL_RKV + 3 * WIDTH], lo[:, :DECAY_LORA],
                            lo[:, LORA_PAD:LORA_PAD + ICLR_LORA], lo[:, 2 * LORA_PAD:]], axis=-1)


def _bias_diagonals(rel_bias):
    u = np.arange(ATT_DIAG)[None, :]
    s = np.arange(ATT_LEAD + 1)[:, None]
    idx = np.clip(ATT_KEYS - u - ATT_Q * s, -MAX_REL, MAX_REL) + MAX_REL
    tab = jnp.take(rel_bias, jnp.asarray(idx.reshape(-1), jnp.int32), axis=1)
    return tab.reshape(HEADS, ATT_LEAD + 1, ATT_DIAG).transpose(1, 0, 2)


def _bias_sample(rel_bias, t, n_past):
    rel = np.arange(t)[:, None] + n_past - np.arange(n_past + t)[None, :]
    idx = np.clip(rel, -MAX_REL, MAX_REL) + MAX_REL
    tab = jnp.take(rel_bias, jnp.asarray(idx.reshape(-1), jnp.int32), axis=1).reshape(HEADS, t, n_past + t)
    return tab[:, :, :n_past], tab[:, :, n_past:]


def _layer(x, n_batch, t, attend, first_rows, prep_tm, wkv0, scan_tb, wts):
    m = n_batch * t
    x = _ffn(x, *wts["ffn1"])
    proj = _proj(x, *wts["proj"])
    att = attend(proj)

    prep = _prep(proj, 0, m, prep_tm, *first_rows(proj), wts["prep"])
    r, wdec, k2, v, kk, kka, g, cb = prep
    ops = (wdec, kka, k2, r, kk)
    if n_batch == LANE_GROUP_BATCH and scan_tb % 64 == 0:
        tok = lambda a: a.reshape(n_batch, t, WIDTH)
        y, s_t = _scan_tok([tok(a) for a in ops], tok(v), _state_to_scan(wkv0, n_batch), scan_tb)
        y = y.reshape(m, WIDTH)
    else:
        rows = [_rows_to_scan(a, n_batch, t) for a in ops]
        y, s_t = _scan(rows, _rows_to_scan(v, n_batch, t), _state_to_scan(wkv0, n_batch), scan_tb)
        y = _rows_from_scan(y, n_batch, t)

    x = _merge(att, y, g, cb, proj, x, *wts["merge"])
    x = _ffn(x, *wts["ffn2"])

    last = jnp.concatenate([proj[(b + 1) * t - 1:(b + 1) * t] for b in range(n_batch)])
    return x, proj, _state_from_scan(s_t, n_batch), _shift_row_from_proj(last)[:, None]


def kernel(x_prompt, x_sample, cache_k, cache_v, state_wkv, state_shift, norm_ffn1_pre, norm_ffn1_post, w_ffn1_in, w_ffn1_down, norm_mix_pre, norm_mix_post, w_in, rel_bias, w_att_out, rwkv_mu, rwkv_w0, rwkv_w_up, rwkv_a0, rwkv_a_up, rwkv_g_up, rwkv_k_k, rwkv_k_a, rwkv_r_k, rwkv_ln_w, rwkv_ln_b, w_rwkv_out, w_out, norm_ffn2_pre, norm_ffn2_post, w_ffn2_in, w_ffn2_down):
    depth = w_in.shape[0]
    batch, seq, _ = x_prompt.shape
    dec_batch, dec_seq, _ = x_sample.shape
    n_past = cache_k.shape[2]
    m_p, m_s = batch * seq, dec_batch * dec_seq
    keep = min(PAST_BAND, seq)
    assert n_past == PAST_BAND and seq % ATT_Q == 0 and seq // ATT_Q > ATT_LEAD
    assert m_p % FFN_TM == 0 and m_p % PROJ_TM == 0 and m_p % MERGE_TM == 0 and seq % PREP_TM == 0
    assert seq % SCAN_TB == 0 and batch == LANE_GROUP_BATCH and dec_batch % LANE_GROUP_BATCH == 0

    x_p = x_prompt.reshape(m_p, D_MODEL)
    x_s = x_sample.reshape(m_s, D_MODEL)
    ones_bd = jnp.asarray(np.kron(np.eye(2), np.ones((HEAD_DIM, HEAD_DIM))), BF16)
    row = lambda v: v.reshape(1, -1)
    heads = lambda a, n, t: a.reshape(n, t, HEADS, HEAD_DIM)

    outs = {k: [] for k in ("kp", "vp", "wp", "sp", "ks", "vs", "ws", "ss")}
    for l in range(depth):
        w = w_in[l].astype(BF16)
        o_lora = COL_GA
        o_ga = o_lora + DECAY_LORA + ICLR_LORA + GATE_LORA
        w_proj = jnp.concatenate([
            w[:, :o_lora], w[:, o_ga:],
            _pad_cols(w[:, o_lora:o_lora + DECAY_LORA], LORA_PAD),
            _pad_cols(w[:, o_lora + DECAY_LORA:o_lora + DECAY_LORA + ICLR_LORA], LORA_PAD),
            w[:, o_lora + DECAY_LORA + ICLR_LORA:o_ga]], axis=1)
        mu_rkv, mu_lora = _reorder_shift_cols(rwkv_mu[l][None])
        wts = {
            "ffn1": (row(norm_ffn1_pre[l]), row(norm_ffn1_post[l]),
                     w_ffn1_in[l].astype(BF16), w_ffn1_down[l].astype(BF16)),
            "proj": (row(norm_mix_pre[l]), w_proj),
            "prep": (mu_rkv, mu_lora, row(rwkv_w0[l]), row(rwkv_a0[l]), row(rwkv_k_k[l]), row(rwkv_k_a[l]),
                     row(rwkv_r_k[l]), row(rwkv_ln_b[l]),
                     *_split_bf16(_pad_rows(rwkv_w_up[l], LORA_PAD)),
                     *_split_bf16(_pad_rows(rwkv_a_up[l], LORA_PAD)),
                     *_split_bf16(rwkv_g_up[l]), ones_bd),
            "merge": (row(rwkv_ln_w[l]), row(norm_mix_post[l]), ones_bd,
                      w_att_out[l].astype(BF16), w_rwkv_out[l].astype(BF16), w_out[l].astype(BF16)),
            "ffn2": (row(norm_ffn2_pre[l]), row(norm_ffn2_post[l]),
                     w_ffn2_in[l].astype(BF16), w_ffn2_down[l].astype(BF16)),
        }

        diag = _bias_diagonals(rel_bias[l])
        starts_seq = (jnp.arange(m_p // PREP_TM) % (seq // PREP_TM) == 0)[:, None]

        def first_rows_prompt(proj):
            last_of_tile = proj.reshape(m_p // PREP_TM, PREP_TM, PROJ_COLS)[:, PREP_TM - 1]
            first = jnp.where(starts_seq, 0.0, jnp.roll(last_of_tile, 1, axis=0))[:, None]
            return first[..., COL_RKV:COL_RKV + 3 * WIDTH], first[..., COL_LORA:]

        x_p, proj_p, wkv_p, sh_p = _layer(
            x_p, batch, seq, lambda proj: _attn_prompt(proj, diag, batch, seq), first_rows_prompt, PREP_TM,
            jnp.zeros((batch, HEADS, HEAD_DIM, HEAD_DIM), F32), SCAN_TB, wts)
        tail = lambda col: jnp.stack([proj_p[(b + 1) * seq - keep:(b + 1) * seq, col:col + WIDTH]
                                      for b in range(batch)])
        outs["kp"].append(heads(tail(COL_K), batch, keep))
        outs["vp"].append(heads(tail(COL_V), batch, keep))
        outs["wp"].append(wkv_p.astype(state_wkv.dtype))
        outs["sp"].append(sh_p)

        ck = cache_k[l].reshape(dec_batch, n_past, WIDTH)
        cv = cache_v[l].reshape(dec_batch, n_past, WIDTH)
        bias_c, bias_n = _bias_sample(rel_bias[l], dec_seq, n_past)
        x_s, proj_s, wkv_s, sh_s = _layer(
            x_s, dec_batch, dec_seq,
            lambda proj: _attn_sample(proj, 0, ck, cv, bias_c, bias_n, dec_batch, dec_seq),
            lambda proj: _reorder_shift_cols(state_shift[l]), dec_seq,
            state_wkv[l].astype(F32), dec_seq, wts)
        outs["ks"].append(heads(proj_s[:, COL_K:COL_K + WIDTH], dec_batch, dec_seq))
        outs["vs"].append(heads(proj_s[:, COL_V:COL_V + WIDTH], dec_batch, dec_seq))
        outs["ws"].append(wkv_s.astype(state_wkv.dtype))
        outs["ss"].append(sh_s)

    st = lambda k: jnp.stack(outs[k])
    return (x_p.reshape(batch, seq, D_MODEL), x_s.reshape(dec_batch, dec_seq, D_MODEL),
            st("kp"), st("vp"), st("wp"), st("sp"), st("ks"), st("vs"), st("ws"), st("ss"))
```

```python
import functools

import numpy as np
import jax
import jax.numpy as jnp
from jax import lax
from jax.experimental import pallas as pl
from jax.experimental.pallas import tpu as pltpu

F32 = jnp.float32
BF16 = jnp.bfloat16

D_MODEL = 2048
CHUNK = 64
LEFT_CHUNKS = 8
PAST_BAND = LEFT_CHUNKS * CHUNK
BAND = PAST_BAND + CHUNK
HEAD_DIM = 64
HEADS = 16
WIDTH = HEADS * HEAD_DIM
DECAY_LORA = 96
ICLR_LORA = 96
GATE_LORA = 256
LORA_PAD = 128
LORA_COLS = 2 * LORA_PAD + GATE_LORA
D_FF = 5632
MAX_REL = 128
RMS_EPS = 1e-6
GN_EPS = 64e-5
NEG_INF = -1e30

COL_Q, COL_K, COL_V = 0, WIDTH, 2 * WIDTH
COL_RKV = 3 * WIDTH
COL_GA = 6 * WIDTH
COL_GB = COL_GA + D_MODEL
COL_LORA = COL_GB + D_MODEL
PROJ_COLS = COL_LORA + LORA_COLS

VMEM_LIMIT = 56 * 1024 * 1024
VMEM_LIMIT_MAX = 60 * 1024 * 1024

FFN_TM = 512
FFN_TF = 512
PROJ_TM = 1024
PROJ_TN = 1536
MERGE_TM = 256
PREP_TM = 256
SCAN_TB = 128
ATT_Q = 2 * CHUNK
ATT_KEYS = PAST_BAND + ATT_Q
ATT_LEAD = PAST_BAND // ATT_Q
ATT_DIAG = ATT_KEYS + ATT_Q
LANE_GROUP_BATCH = 4
KLO = HEAD_DIM // 2
STEP_PITCH = KLO + 8
XT_PITCH = HEAD_DIM + 8


def _sigmoid(x):
    return 1.0 / (1.0 + jnp.exp(-x))


def _split_bf16(x):
    hi = x.astype(BF16)
    lo = (x - hi.astype(F32)).astype(BF16)
    return hi, lo


def _dot(a, b):
    return jnp.dot(a, b, preferred_element_type=F32)


def _dot_f32(a, b_hi, b_lo):
    a_hi, a_lo = _split_bf16(a)
    return _dot(a_hi, b_hi) + _dot(a_lo, b_hi) + _dot(a_hi, b_lo)


def _segsum(x, ones_bd):
    hi, lo = _split_bf16(x)
    outs = []
    for j in range(x.shape[-1] // 128):
        sl = slice(j * 128, (j + 1) * 128)
        outs.append(_dot(hi[:, sl], ones_bd) + _dot(lo[:, sl], ones_bd))
    return jnp.concatenate(outs, axis=-1)


def _rms_scale(x):
    return lax.rsqrt(jnp.mean(x * x, axis=-1, keepdims=True) + RMS_EPS)


def _ffn_body(x_ref, gpre_ref, gpost_ref, wg_ref, wu_ref, wd_ref, o_ref, h_ref, *, n_f):
    f = pl.program_id(1)

    @pl.when(f == 0)
    def _():
        x = x_ref[...]
        h_ref[...] = (x * _rms_scale(x) * gpre_ref[...]).astype(BF16)
        o_ref[...] = jnp.zeros_like(o_ref)

    h = h_ref[...]
    g = _dot(h, wg_ref[...])
    u = _dot(h, wu_ref[...])
    act = (g * _sigmoid(g) * u).astype(BF16)
    o_ref[...] += _dot(act, wd_ref[...])

    @pl.when(f == n_f - 1)
    def _():
        y = o_ref[...]
        o_ref[...] = x_ref[...] + 0.5 * (y * _rms_scale(y) * gpost_ref[...])


def _ffn(x, g_pre, g_post, w_in, w_down):
    m = x.shape[0]
    tm = min(FFN_TM, m)
    n_m, n_f = m // tm, D_FF // FFN_TF
    return pl.pallas_call(
        functools.partial(_ffn_body, n_f=n_f),
        grid=(n_m, n_f),
        in_specs=[
            pl.BlockSpec((tm, D_MODEL), lambda i, f: (i, 0)),
            pl.BlockSpec((1, D_MODEL), lambda i, f: (0, 0)),
            pl.BlockSpec((1, D_MODEL), lambda i, f: (0, 0)),
            pl.BlockSpec((D_MODEL, FFN_TF), lambda i, f: (0, f)),
            pl.BlockSpec((D_MODEL, FFN_TF), lambda i, f: (0, f + n_f)),
            pl.BlockSpec((FFN_TF, D_MODEL), lambda i, f: (f, 0)),
        ],
        out_specs=pl.BlockSpec((tm, D_MODEL), lambda i, f: (i, 0)),
        out_shape=jax.ShapeDtypeStruct((m, D_MODEL), F32),
        scratch_shapes=[pltpu.VMEM((tm, D_MODEL), BF16)],
        compiler_params=pltpu.CompilerParams(
            dimension_semantics=("parallel", "arbitrary"), vmem_limit_bytes=VMEM_LIMIT),
        name="ffn",
    )(x, g_pre, g_post, w_in, w_in, w_down)


def _proj_body(x_ref, g_ref, w_ref, o_ref, h_ref):
    @pl.when(pl.program_id(1) == 0)
    def _():
        x = x_ref[...]
        h_ref[...] = (x * _rms_scale(x) * g_ref[...]).astype(BF16)

    o_ref[...] = _dot(h_ref[...], w_ref[...])


def _proj(x, g, w):
    m = x.shape[0]
    tm = min(PROJ_TM, m)
    return pl.pallas_call(
        _proj_body,
        grid=(m // tm, PROJ_COLS // PROJ_TN),
        in_specs=[
            pl.BlockSpec((tm, D_MODEL), lambda i, n: (i, 0)),
            pl.BlockSpec((1, D_MODEL), lambda i, n: (0, 0)),
            pl.BlockSpec((D_MODEL, PROJ_TN), lambda i, n: (0, n)),
        ],
        out_specs=pl.BlockSpec((tm, PROJ_TN), lambda i, n: (i, n)),
        out_shape=jax.ShapeDtypeStruct((m, PROJ_COLS), F32),
        scratch_shapes=[pltpu.VMEM((tm, D_MODEL), BF16)],
        compiler_params=pltpu.CompilerParams(
            dimension_semantics=("parallel", "arbitrary"), vmem_limit_bytes=VMEM_LIMIT),
        name="proj",
    )(x, g, w)


def _softmax_pv(parts):
    m = functools.reduce(jnp.maximum, [jnp.max(s, axis=-1, keepdims=True) for s, _ in parts])
    num, den = 0.0, 0.0
    for s, v in parts:
        p = jnp.exp(s - m)
        den = den + jnp.sum(p, axis=-1, keepdims=True)
        num = num + _dot(p.astype(BF16), v)
    return num / den


def _qk(q, k):
    return lax.dot_general(q, k, (((1,), (1,)), ((), ())), preferred_element_type=F32)


def _attn_prompt_body(q_ref, k_ref, v_ref, diag_ref, o_ref, kbf_ref, vbf_ref, bias_ref):
    c = pl.program_id(1)
    shift = jnp.maximum(ATT_LEAD - c, 0)

    @pl.when(c == 0)
    def _():
        kbf_ref[...] = k_ref[...].astype(BF16)
        vbf_ref[...] = v_ref[...].astype(BF16)

    @pl.when((pl.program_id(0) == 0) & (c <= ATT_LEAD))
    def _():
        key = lax.broadcasted_iota(jnp.int32, (ATT_KEYS, ATT_Q), 0) // CHUNK
        qry = lax.broadcasted_iota(jnp.int32, (ATT_KEYS, ATT_Q), 1) // CHUNK
        dist = (LEFT_CHUNKS + qry) - (key + shift * (ATT_Q // CHUNK))
        in_band = (dist >= 0) & (dist <= LEFT_CHUNKS)
        for h in range(HEADS):
            diag = jnp.broadcast_to(diag_ref[shift, pl.ds(h, 1), :], (ATT_Q, ATT_DIAG))
            toep = pltpu.roll(diag, ATT_DIAG - ATT_Q, axis=1, stride=1, stride_axis=0)
            bias_ref[shift, h] = jnp.where(in_band, toep[:, :ATT_KEYS].T, NEG_INF)

    start = pl.multiple_of(jnp.maximum(c - ATT_LEAD, 0) * ATT_Q, ATT_Q)
    lane = lax.broadcasted_iota(jnp.int32, (ATT_Q, 128), 1)
    for pair in range(HEADS // 2):
        sl = slice(pair * 128, (pair + 1) * 128)
        kp = kbf_ref[pl.ds(start, ATT_KEYS), sl]
        vp = vbf_ref[pl.ds(start, ATT_KEYS), sl]
        qp = q_ref[:, sl] * (HEAD_DIM ** -0.5)
        halves = []
        for half in range(2):
            qm = jnp.where((lane >= HEAD_DIM) == bool(half), qp, 0.0).astype(BF16)
            s = _qk(kp, qm) + bias_ref[shift, 2 * pair + half]
            p = jnp.exp(s - jnp.max(s, axis=0, keepdims=True))
            p = (p * (1.0 / jnp.sum(p, axis=0, keepdims=True))).astype(BF16)
            halves.append(lax.dot_general(p, vp, (((0,), (0,)), ((), ())),
                                          preferred_element_type=F32))
        o_ref[:, sl] = jnp.where(lane < HEAD_DIM, halves[0], halves[1]).astype(BF16)


def _attn_prompt(proj, diag, batch, seq):
    n_c = seq // ATT_Q
    once = dict(pipeline_mode=pl.Buffered(1))
    return pl.pallas_call(
        _attn_prompt_body,
        grid=(batch, n_c),
        in_specs=[
            pl.BlockSpec((ATT_Q, WIDTH), lambda b, c: (b * n_c + c, COL_Q // WIDTH)),
            pl.BlockSpec((seq, WIDTH), lambda b, c: (b, COL_K // WIDTH), **once),
            pl.BlockSpec((seq, WIDTH), lambda b, c: (b, COL_V // WIDTH), **once),
            pl.BlockSpec(diag.shape, lambda b, c: (0, 0, 0)),
        ],
        out_specs=pl.BlockSpec((ATT_Q, WIDTH), lambda b, c: (b * n_c + c, 0)),
        out_shape=jax.ShapeDtypeStruct((batch * seq, WIDTH), BF16),
        scratch_shapes=[pltpu.VMEM((seq, WIDTH), BF16), pltpu.VMEM((seq, WIDTH), BF16),
                        pltpu.VMEM((ATT_LEAD + 1, HEADS, ATT_KEYS, ATT_Q), F32)],
        compiler_params=pltpu.CompilerParams(
            dimension_semantics=("arbitrary", "arbitrary"), vmem_limit_bytes=VMEM_LIMIT_MAX),
        name="attn_prompt",
    )(proj, proj, proj, diag)


def _attn_sample_body(q_ref, kn_ref, vn_ref, ck_ref, cv_ref, bias_c_ref, bias_n_ref, o_ref):
    q = (q_ref[...] * (HEAD_DIM ** -0.5)).astype(BF16)
    kn = kn_ref[...].astype(BF16)
    vn = vn_ref[...].astype(BF16)
    outs = []
    for h in range(HEADS):
        sl = slice(h * HEAD_DIM, (h + 1) * HEAD_DIM)
        qh = q[:, sl]
        ck = ck_ref[0, :, sl].astype(BF16)
        cv = cv_ref[0, :, sl].astype(BF16)
        s_c = _qk(qh, ck) + bias_c_ref[h]
        s_n = _qk(qh, kn[:, sl]) + bias_n_ref[h]
        outs.append(_softmax_pv([(s_c, cv), (s_n, vn[:, sl])]))
    o_ref[...] = jnp.concatenate(outs, axis=-1).astype(BF16)


def _attn_sample(proj, row0, cache_k, cache_v, bias_c, bias_n, batch, t):
    n_past = cache_k.shape[1]
    blk0 = row0 // t
    return pl.pallas_call(
        _attn_sample_body,
        grid=(batch,),
        in_specs=[
            pl.BlockSpec((t, WIDTH), lambda b: (blk0 + b, COL_Q // WIDTH)),
            pl.BlockSpec((t, WIDTH), lambda b: (blk0 + b, COL_K // WIDTH)),
            pl.BlockSpec((t, WIDTH), lambda b: (blk0 + b, COL_V // WIDTH)),
            pl.BlockSpec((1, n_past, WIDTH), lambda b: (b, 0, 0)),
            pl.BlockSpec((1, n_past, WIDTH), lambda b: (b, 0, 0)),
            pl.BlockSpec((HEADS, t, n_past), lambda b: (0, 0, 0)),
            pl.BlockSpec((HEADS, t, t), lambda b: (0, 0, 0)),
        ],
        out_specs=pl.BlockSpec((t, WIDTH), lambda b: (b, 0)),
        out_shape=jax.ShapeDtypeStruct((batch * t, WIDTH), BF16),
        compiler_params=pltpu.CompilerParams(
            dimension_semantics=("parallel",), vmem_limit_bytes=VMEM_LIMIT),
        name="attn_sample",
    )(proj, proj, proj, cache_k, cache_v, bias_c, bias_n)


def _prep_body(p_ref, pl_ref, bp_ref, bl_ref, mu_ref, mul_ref, w0_ref, a0_ref, kk_ref, ka_ref, rk_ref,
               lnb_ref, wup_hi, wup_lo, aup_hi, aup_lo, gup_hi, gup_lo, bd_ref,
               r_o, w_o, k_o, v_o, kk_o, kka_o, g_o, cb_o):
    def shifted(p, first_row, mu):
        row = lax.broadcasted_iota(jnp.int32, p.shape, 0)
        prev = jnp.where(row == 0, first_row, pltpu.roll(p, 1, axis=0))
        return p + (prev - p) * mu

    xm = shifted(p_ref[...], bp_ref[0], mu_ref[...])
    xl = shifted(pl_ref[...], bl_ref[0], mul_ref[...])
    r = xm[:, 0:WIDTH]
    k = xm[:, WIDTH:2 * WIDTH]
    v = xm[:, 2 * WIDTH:3 * WIDTH]
    wd = xl[:, 0:LORA_PAD]
    ad = xl[:, LORA_PAD:2 * LORA_PAD]
    gd = xl[:, 2 * LORA_PAD:]

    z = -(w0_ref[...] + _dot_f32(jnp.tanh(wd), wup_hi[...], wup_lo[...]))
    softplus = jnp.maximum(z, 0.0) + jnp.log(1.0 + jnp.exp(-jnp.abs(z)))
    decay = jnp.exp(-jnp.exp(-softplus - 0.5))
    a = _sigmoid(a0_ref[...] + _dot_f32(ad, aup_hi[...], aup_lo[...]))
    g = _dot_f32(_sigmoid(gd), gup_hi[...], gup_lo[...])

    bd = bd_ref[...]
    kk = k * kk_ref[...]
    kk = kk / jnp.maximum(jnp.sqrt(_segsum(kk * kk, bd)), 1e-12)
    k2 = k * (1.0 + (a - 1.0) * ka_ref[...])
    bonus = _segsum(r * k2 * rk_ref[...], bd) * v

    r_o[...] = r
    w_o[...] = decay
    k_o[...] = k2
    v_o[...] = v
    kk_o[...] = kk
    kka_o[...] = kk * a
    g_o[...] = g
    cb_o[...] = lnb_ref[...] + bonus


def _prep(proj, row0, n_rows, tm, first_rkv, first_lora, params):
    n_t = n_rows // tm
    blk0 = row0 // tm
    vec = lambda w: pl.BlockSpec((1, w), lambda i: (0, 0))
    mat = lambda r: pl.BlockSpec((r, WIDTH), lambda i: (0, 0))
    out = pl.BlockSpec((tm, WIDTH), lambda i: (i, 0))
    return pl.pallas_call(
        _prep_body,
        grid=(n_t,),
        in_specs=[
            pl.BlockSpec((tm, 3 * WIDTH), lambda i: (blk0 + i, COL_RKV // (3 * WIDTH))),
            pl.BlockSpec((tm, LORA_COLS), lambda i: (blk0 + i, COL_LORA // LORA_COLS)),
            pl.BlockSpec((1, 1, 3 * WIDTH), lambda i: (i, 0, 0)),
            pl.BlockSpec((1, 1, LORA_COLS), lambda i: (i, 0, 0)),
            vec(3 * WIDTH), vec(LORA_COLS), vec(WIDTH), vec(WIDTH), vec(WIDTH), vec(WIDTH), vec(WIDTH),
            vec(WIDTH), mat(LORA_PAD), mat(LORA_PAD), mat(LORA_PAD), mat(LORA_PAD),
            mat(GATE_LORA), mat(GATE_LORA),
            pl.BlockSpec((128, 128), lambda i: (0, 0)),
        ],
        out_specs=[out] * 8,
        out_shape=[jax.ShapeDtypeStruct((n_rows, WIDTH), F32)] * 8,
        compiler_params=pltpu.CompilerParams(
            dimension_semantics=("parallel",), vmem_limit_bytes=VMEM_LIMIT),
        name="rwkv_prep",
    )(proj, proj, first_rkv, first_lora, *params)


def _scan_steps(s_ref, row, get_v, put_y, n_steps):
    W, KKA, K, R, KK = range(5)
    zero = jnp.zeros((KLO, 128), F32)

    sa0 = [zero, zero]
    for k in range(HEAD_DIM):
        sa0[k % 2] = sa0[k % 2] - s_ref[k] * row(KK, 0, k)

    def step(t, sa):
        v = get_v(t)
        t_next = jnp.minimum(t + 1, n_steps - 1)
        y = [zero, zero]
        sa_next = [zero, zero]
        for k in range(HEAD_DIM):
            s_new = s_ref[k] * row(W, t, k) + sa * row(KKA, t, k) + v * row(K, t, k)
            s_ref[k] = s_new
            y[k % 2] = y[k % 2] + s_new * row(R, t, k)
            sa_next[k % 2] = sa_next[k % 2] - s_new * row(KK, t_next, k)
        put_y(t, y[0] + y[1])
        return sa_next[0] + sa_next[1]

    lax.fori_loop(0, n_steps, step, sa0[0] + sa0[1])


def _scan_body(w_ref, kka_ref, k_ref, r_ref, kk_ref, v_ref, s0_ref, y_ref, st_ref, s_ref, rows_ref, *, tb):
    @pl.when(pl.program_id(1) == 0)
    def _():
        s_ref[...] = s0_ref[0]

    lane = lax.broadcasted_iota(jnp.int32, (tb * KLO, 128), 1)
    for a, ref in enumerate((w_ref, kka_ref, k_ref, r_ref, kk_ref)):
        x = ref[0].reshape(tb * KLO, 128)
        swapped = pltpu.roll(x, HEAD_DIM, axis=1)
        rows_ref[a, :, 0:KLO, :] = jnp.where(lane < HEAD_DIM, x, swapped).reshape(tb, KLO, 128)
        rows_ref[a, :, KLO:, :] = jnp.where(lane < HEAD_DIM, swapped, x).reshape(tb, KLO, 128)

    def row(a, t, k):
        return jnp.broadcast_to(rows_ref[a, t, pl.ds(k, 1), :], (KLO, 128))

    def put_y(t, y):
        y_ref[0, t] = y

    _scan_steps(s_ref, row, lambda t: v_ref[0, t], put_y, tb)

    @pl.when(pl.program_id(1) == pl.num_programs(1) - 1)
    def _():
        st_ref[0] = s_ref[...]


def _scan_tok_body(w_hbm, kka_hbm, k_hbm, r_hbm, kk_hbm, v_hbm, s0_ref, y_ref, st_ref,
                   s_ref, rows_ref, xt_ref, vt_ref, yt_ref, in_ref, in_sem, *, tb):
    step = pl.program_id(0)
    operands = (w_hbm, kka_hbm, k_hbm, r_hbm, kk_hbm, v_hbm)

    def block_copy(a, blk):
        return pltpu.make_async_copy(operands[a].at[:, pl.ds(blk * tb, tb), :], in_ref.at[a], in_sem.at[a])

    @pl.when(step == 0)
    def _():
        for a in range(len(operands)):
            block_copy(a, 0).start()
        s_ref[...] = s0_ref[0]
        xt_ref[...] = jnp.zeros_like(xt_ref)

    def head_rows(c):
        return pl.ds(c, HEADS, stride=XT_PITCH)

    def channel_major(ref):
        for b in range(LANE_GROUP_BATCH):
            xt = ref[b].T
            for h in range(HEADS):
                xt_ref[b, h * XT_PITCH:h * XT_PITCH + HEAD_DIM, 0:tb] = xt[h * HEAD_DIM:(h + 1) * HEAD_DIM]

    def lane_tile(chan):
        parts = [xt_ref[b, head_rows(c), :] for c in chan for b in range(LANE_GROUP_BATCH)]
        return jnp.concatenate(parts, axis=0).T[0:tb]

    for a in range(5):
        block_copy(a, step).wait()
        channel_major(in_ref.at[a])
        for k in range(HEAD_DIM):
            rows_ref[a, k] = lane_tile((k, k))
    block_copy(5, step).wait()
    channel_major(in_ref.at[5])
    for j in range(KLO):
        vt_ref[pl.ds(j, tb, stride=STEP_PITCH), :] = lane_tile((j, KLO + j))

    @pl.when(step + 1 < pl.num_programs(0))
    def _():
        for a in range(len(operands)):
            block_copy(a, step + 1).start()

    def row(a, t, k):
        return jnp.broadcast_to(rows_ref[a, k, pl.ds(t, 1), :], (KLO, 128))

    def step_rows(t):
        return pl.ds(pl.multiple_of(t * STEP_PITCH, 8), KLO)

    def put_y(t, y):
        yt_ref[step_rows(t), :] = y

    _scan_steps(s_ref, row, lambda t: vt_ref[step_rows(t), :], put_y, tb)

    for j in range(KLO):
        steps = yt_ref[pl.ds(j, tb, stride=STEP_PITCH), :]
        if tb < 128:
            steps = jnp.concatenate([steps, jnp.zeros((128 - tb, 128), F32)], axis=0)
        tile = steps.T
        for half in range(2):
            for b in range(LANE_GROUP_BATCH):
                r0 = (half * LANE_GROUP_BATCH + b) * HEADS
                xt_ref[b, head_rows(half * KLO + j), :] = tile[r0:r0 + HEADS]
    for b in range(LANE_GROUP_BATCH):
        heads = [xt_ref[b, h * XT_PITCH:h * XT_PITCH + HEAD_DIM, 0:tb] for h in range(HEADS)]
        y_ref[b] = jnp.concatenate(heads, axis=0).T

    @pl.when(pl.program_id(0) == pl.num_programs(0) - 1)
    def _():
        st_ref[0] = s_ref[...]


def _scan_tok(ops, v, s0, tb):
    nb, t, _ = v.shape
    tok_spec = pl.BlockSpec((nb, tb, WIDTH), lambda j: (0, j, 0))
    st_spec = pl.BlockSpec((1, HEAD_DIM, KLO, 128), lambda j: (0, 0, 0, 0))
    return pl.pallas_call(
        functools.partial(_scan_tok_body, tb=tb),
        grid=(t // tb,),
        in_specs=[pl.BlockSpec(memory_space=pl.ANY)] * 6 + [st_spec],
        out_specs=[tok_spec, st_spec],
        out_shape=[jax.ShapeDtypeStruct((nb, t, WIDTH), F32),
                   jax.ShapeDtypeStruct((1, HEAD_DIM, KLO, 128), F32)],
        scratch_shapes=[pltpu.VMEM((HEAD_DIM, KLO, 128), F32),
                        pltpu.VMEM((5, HEAD_DIM, tb, 128), F32),
                        pltpu.VMEM((nb, HEADS * XT_PITCH, 128), F32),
                        pltpu.VMEM((tb * STEP_PITCH, 128), F32),
                        pltpu.VMEM((tb * STEP_PITCH, 128), F32),
                        pltpu.VMEM((6, nb, tb, WIDTH), F32),
                        pltpu.SemaphoreType.DMA((6,))],
        compiler_params=pltpu.CompilerParams(
            dimension_semantics=("arbitrary",), vmem_limit_bytes=VMEM_LIMIT),
        name="rwkv_scan_tok",
    )(*ops, v, s0)


def _scan(rows, v, s0, tb):
    g, t = v.shape[0], v.shape[1]
    row_spec = pl.BlockSpec((1, tb, KLO, 128), lambda i, j: (i, j, 0, 0))
    st_spec = pl.BlockSpec((1, HEAD_DIM, KLO, 128), lambda i, j: (i, 0, 0, 0))
    return pl.pallas_call(
        functools.partial(_scan_body, tb=tb),
        grid=(g, t // tb),
        in_specs=[row_spec] * 6 + [st_spec],
        out_specs=[row_spec, st_spec],
        out_shape=[jax.ShapeDtypeStruct((g, t, KLO, 128), F32),
                   jax.ShapeDtypeStruct((g, HEAD_DIM, KLO, 128), F32)],
        scratch_shapes=[pltpu.VMEM((HEAD_DIM, KLO, 128), F32),
                        pltpu.VMEM((5, tb, HEAD_DIM, 128), F32)],
        compiler_params=pltpu.CompilerParams(
            dimension_semantics=("parallel", "arbitrary"), vmem_limit_bytes=VMEM_LIMIT),
        name="rwkv_scan",
    )(*rows, v, s0)


def _rows_to_scan(x, n_batch, t):
    g = n_batch // LANE_GROUP_BATCH
    x = x.reshape(g, LANE_GROUP_BATCH, t, HEADS, 2, KLO)
    return x.transpose(0, 2, 5, 4, 1, 3).reshape(g, t, KLO, 128)


def _rows_from_scan(y, n_batch, t):
    g = n_batch // LANE_GROUP_BATCH
    y = y.reshape(g, t, KLO, 2, LANE_GROUP_BATCH, HEADS).transpose(0, 4, 1, 5, 3, 2)
    return y.reshape(n_batch * t, WIDTH)


def _state_to_scan(s, n_batch):
    g = n_batch // LANE_GROUP_BATCH
    s = s.reshape(g, LANE_GROUP_BATCH, HEADS, 2, KLO, HEAD_DIM)
    return s.transpose(0, 5, 4, 3, 1, 2).reshape(g, HEAD_DIM, KLO, 128)


def _state_from_scan(s, n_batch):
    g = n_batch // LANE_GROUP_BATCH
    s = s.reshape(g, HEAD_DIM, KLO, 2, LANE_GROUP_BATCH, HEADS)
    return s.transpose(0, 4, 5, 3, 2, 1).reshape(n_batch, HEADS, HEAD_DIM, HEAD_DIM)


def _merge_body(att_ref, y_ref, g_ref, cb_ref, ga_ref, gb_ref, x_ref, lnw_ref, gpost_ref, bd_ref,
                wa_ref, wr_ref, wo_ref, o_ref):
    bd = bd_ref[...]
    y = y_ref[...]
    d = y - _segsum(y, bd) * (1.0 / HEAD_DIM)
    var = _segsum(d * d, bd) * (1.0 / HEAD_DIM)
    rw = ((d * lax.rsqrt(var + GN_EPS) * lnw_ref[...] + cb_ref[...]) * g_ref[...]).astype(BF16)
    rw_o = _dot(rw, wr_ref[...])
    att_o = _dot(att_ref[...], wa_ref[...])
    mixed = (_sigmoid(ga_ref[...]) * att_o + _sigmoid(gb_ref[...]) * rw_o).astype(BF16)
    z = _dot(mixed, wo_ref[...])
    o_ref[...] = x_ref[...] + z * _rms_scale(z) * gpost_ref[...]


def _merge(att, y, g, cb, proj, x, ln_w, g_post, ones_bd, w_att, w_rwkv, w_out):
    m = x.shape[0]
    tm = min(MERGE_TM, m)
    tok = lambda w: pl.BlockSpec((tm, w), lambda i: (i, 0))
    const = lambda a: pl.BlockSpec(a.shape, lambda i: (0,) * a.ndim)
    return pl.pallas_call(
        _merge_body,
        grid=(m // tm,),
        in_specs=[
            tok(WIDTH), tok(WIDTH), tok(WIDTH), tok(WIDTH),
            pl.BlockSpec((tm, D_MODEL), lambda i: (i, COL_GA // D_MODEL)),
            pl.BlockSpec((tm, D_MODEL), lambda i: (i, COL_GB // D_MODEL)),
            tok(D_MODEL), const(ln_w), const(g_post), const(ones_bd),
            const(w_att), const(w_rwkv), const(w_out),
        ],
        out_specs=tok(D_MODEL),
        out_shape=jax.ShapeDtypeStruct((m, D_MODEL), F32),
        compiler_params=pltpu.CompilerParams(
            dimension_semantics=("parallel",), vmem_limit_bytes=VMEM_LIMIT),
        name="merge",
    )(att, y, g, cb, proj, proj, x, ln_w, g_post, ones_bd, w_att, w_rwkv, w_out)


def _pad_cols(x, width):
    return jnp.pad(x, ((0, 0), (0, width - x.shape[1])))


def _pad_rows(x, height):
    return jnp.pad(x, ((0, height - x.shape[0]), (0, 0)))


def _reorder_shift_cols(x):
    o = 3 * WIDTH
    wd = x[..., o:o + DECAY_LORA]
    ad = x[..., o + DECAY_LORA:o + DECAY_LORA + ICLR_LORA]
    gd = x[..., o + DECAY_LORA + ICLR_LORA:]
    pad = [(0, 0)] * (x.ndim - 1)
    lora = jnp.concatenate([jnp.pad(wd, pad + [(0, LORA_PAD - DECAY_LORA)]),
                            jnp.pad(ad, pad + [(0, LORA_PAD - ICLR_LORA)]), gd], axis=-1)
    return x[..., :o], lora


def _shift_row_from_proj(rows):
    lo = rows[:, COL_LORA:]
    return jnp.concatenate([rows[:, COL_RKV:COL_RKV + 3 * WIDTH], lo[:, :DECAY_LORA],
                            lo[:, LORA_PAD:LORA_PAD + ICLR_LORA], lo[:, 2 * LORA_PAD:]], axis=-1)


def _bias_diagonals(rel_bias):
    u = np.arange(ATT_DIAG)[None, :]
    s = np.arange(ATT_LEAD + 1)[:, None]
    idx = np.clip(ATT_KEYS - u - ATT_Q * s, -MAX_REL, MAX_REL) + MAX_REL
    tab = jnp.take(rel_bias, jnp.asarray(idx.reshape(-1), jnp.int32), axis=1)
    return tab.reshape(HEADS, ATT_LEAD + 1, ATT_DIAG).transpose(1, 0, 2)


def _bias_sample(rel_bias, t, n_past):
    rel = np.arange(t)[:, None] + n_past - np.arange(n_past + t)[None, :]
    idx = np.clip(rel, -MAX_REL, MAX_REL) + MAX_REL
    tab = jnp.take(rel_bias, jnp.asarray(idx.reshape(-1), jnp.int32), axis=1).reshape(HEADS, t, n_past + t)
    return tab[:, :, :n_past], tab[:, :, n_past:]


def _layer(x, n_batch, t, attend, first_rows, prep_tm, wkv0, scan_tb, wts):
    m = n_batch * t
    x = _ffn(x, *wts["ffn1"])
    proj = _proj(x, *wts["proj"])
    att = attend(proj)

    prep = _prep(proj, 0, m, prep_tm, *first_rows(proj), wts["prep"])
    r, wdec, k2, v, kk, kka, g, cb = prep
    ops = (wdec, kka, k2, r, kk)
    if n_batch == LANE_GROUP_BATCH and scan_tb % 64 == 0:
        tok = lambda a: a.reshape(n_batch, t, WIDTH)
        y, s_t = _scan_tok([tok(a) for a in ops], tok(v), _state_to_scan(wkv0, n_batch), scan_tb)
        y = y.reshape(m, WIDTH)
    else:
        rows = [_rows_to_scan(a, n_batch, t) for a in ops]
        y, s_t = _scan(rows, _rows_to_scan(v, n_batch, t), _state_to_scan(wkv0, n_batch), scan_tb)
        y = _rows_from_scan(y, n_batch, t)

    x = _merge(att, y, g, cb, proj, x, *wts["merge"])
    x = _ffn(x, *wts["ffn2"])

    last = jnp.concatenate([proj[(b + 1) * t - 1:(b + 1) * t] for b in range(n_batch)])
    return x, proj, _state_from_scan(s_t, n_batch), _shift_row_from_proj(last)[:, None]


def kernel(x_prompt, x_sample, cache_k, cache_v, state_wkv, state_shift, norm_ffn1_pre, norm_ffn1_post, w_ffn1_in, w_ffn1_down, norm_mix_pre, norm_mix_post, w_in, rel_bias, w_att_out, rwkv_mu, rwkv_w0, rwkv_w_up, rwkv_a0, rwkv_a_up, rwkv_g_up, rwkv_k_k, rwkv_k_a, rwkv_r_k, rwkv_ln_w, rwkv_ln_b, w_rwkv_out, w_out, norm_ffn2_pre, norm_ffn2_post, w_ffn2_in, w_ffn2_down):
    depth = w_in.shape[0]
    batch, seq, _ = x_prompt.shape
    dec_batch, dec_seq, _ = x_sample.shape
    n_past = cache_k.shape[2]
    m_p, m_s = batch * seq, dec_batch * dec_seq
    keep = min(PAST_BAND, seq)
    assert n_past == PAST_BAND and seq % ATT_Q == 0 and seq // ATT_Q > ATT_LEAD
    assert m_p % FFN_TM == 0 and m_p % PROJ_TM == 0 and m_p % MERGE_TM == 0 and seq % PREP_TM == 0
    assert seq % SCAN_TB == 0 and batch == LANE_GROUP_BATCH and dec_batch % LANE_GROUP_BATCH == 0

    x_p = x_prompt.reshape(m_p, D_MODEL)
    x_s = x_sample.reshape(m_s, D_MODEL)
    ones_bd = jnp.asarray(np.kron(np.eye(2), np.ones((HEAD_DIM, HEAD_DIM))), BF16)
    row = lambda v: v.reshape(1, -1)
    heads = lambda a, n, t: a.reshape(n, t, HEADS, HEAD_DIM)

    outs = {k: [] for k in ("kp", "vp", "wp", "sp", "ks", "vs", "ws", "ss")}
    for l in range(depth):
        w = w_in[l].astype(BF16)
        o_lora = COL_GA
        o_ga = o_lora + DECAY_LORA + ICLR_LORA + GATE_LORA
        w_proj = jnp.concatenate([
            w[:, :o_lora], w[:, o_ga:],
            _pad_cols(w[:, o_lora:o_lora + DECAY_LORA], LORA_PAD),
            _pad_cols(w[:, o_lora + DECAY_LORA:o_lora + DECAY_LORA + ICLR_LORA], LORA_PAD),
            w[:, o_lora + DECAY_LORA + ICLR_LORA:o_ga]], axis=1)
        mu_rkv, mu_lora = _reorder_shift_cols(rwkv_mu[l][None])
        wts = {
            "ffn1": (row(norm_ffn1_pre[l]), row(norm_ffn1_post[l]),
                     w_ffn1_in[l].astype(BF16), w_ffn1_down[l].astype(BF16)),
            "proj": (row(norm_mix_pre[l]), w_proj),
            "prep": (mu_rkv, mu_lora, row(rwkv_w0[l]), row(rwkv_a0[l]), row(rwkv_k_k[l]), row(rwkv_k_a[l]),
                     row(rwkv_r_k[l]), row(rwkv_ln_b[l]),
                     *_split_bf16(_pad_rows(rwkv_w_up[l], LORA_PAD)),
                     *_split_bf16(_pad_rows(rwkv_a_up[l], LORA_PAD)),
                     *_split_bf16(rwkv_g_up[l]), ones_bd),
            "merge": (row(rwkv_ln_w[l]), row(norm_mix_post[l]), ones_bd,
                      w_att_out[l].astype(BF16), w_rwkv_out[l].astype(BF16), w_out[l].astype(BF16)),
            "ffn2": (row(norm_ffn2_pre[l]), row(norm_ffn2_post[l]),
                     w_ffn2_in[l].astype(BF16), w_ffn2_down[l].astype(BF16)),
        }

        diag = _bias_diagonals(rel_bias[l])
        starts_seq = (jnp.arange(m_p // PREP_TM) % (seq // PREP_TM) == 0)[:, None]

        def first_rows_prompt(proj):
            last_of_tile = proj.reshape(m_p // PREP_TM, PREP_TM, PROJ_COLS)[:, PREP_TM - 1]
            first = jnp.where(starts_seq, 0.0, jnp.roll(last_of_tile, 1, axis=0))[:, None]
            return first[..., COL_RKV:COL_RKV + 3 * WIDTH], first[..., COL_LORA:]

        x_p, proj_p, wkv_p, sh_p = _layer(
            x_p, batch, seq, lambda proj: _attn_prompt(proj, diag, batch, seq), first_rows_prompt, PREP_TM,
            jnp.zeros((batch, HEADS, HEAD_DIM, HEAD_DIM), F32), SCAN_TB, wts)
        tail = lambda col: jnp.stack([proj_p[(b + 1) * seq - keep:(b + 1) * seq, col:col + WIDTH]
                                      for b in range(batch)])
        outs["kp"].append(heads(tail(COL_K), batch, keep))
        outs["vp"].append(heads(tail(COL_V), batch, keep))
        outs["wp"].append(wkv_p.astype(state_wkv.dtype))
        outs["sp"].append(sh_p)

        ck = cache_k[l].reshape(dec_batch, n_past, WIDTH)
        cv = cache_v[l].reshape(dec_batch, n_past, WIDTH)
        bias_c, bias_n = _bias_sample(rel_bias[l], dec_seq, n_past)
        x_s, proj_s, wkv_s, sh_s = _layer(
            x_s, dec_batch, dec_seq,
            lambda proj: _attn_sample(proj, 0, ck, cv, bias_c, bias_n, dec_batch, dec_seq),
            lambda proj: _reorder_shift_cols(state_shift[l]), dec_seq,
            state_wkv[l].astype(F32), dec_seq, wts)
        outs["ks"].append(heads(proj_s[:, COL_K:COL_K + WIDTH], dec_batch, dec_seq))
        outs["vs"].append(heads(proj_s[:, COL_V:COL_V + WIDTH], dec_batch, dec_seq))
        outs["ws"].append(wkv_s.astype(state_wkv.dtype))
        outs["ss"].append(sh_s)

    st = lambda k: jnp.stack(outs[k])
    return (x_p.reshape(batch, seq, D_MODEL), x_s.reshape(dec_batch, dec_seq, D_MODEL),
            st("kp"), st("vp"), st("wp"), st("sp"), st("ks"), st("vs"), st("ws"), st("ss"))
```

```python
import functools

import numpy as np
import jax
import jax.numpy as jnp
from jax import lax
from jax.experimental import pallas as pl
from jax.experimental.pallas import tpu as pltpu

F32 = jnp.float32
BF16 = jnp.bfloat16

D_MODEL = 2048
CHUNK = 64
LEFT_CHUNKS = 8
PAST_BAND = LEFT_CHUNKS * CHUNK
BAND = PAST_BAND + CHUNK
HEAD_DIM = 64
HEADS = 16
WIDTH = HEADS * HEAD_DIM
DECAY_LORA = 96
ICLR_LORA = 96
GATE_LORA = 256
LORA_PAD = 128
LORA_COLS = 2 * LORA_PAD + GATE_LORA
D_FF = 5632
MAX_REL = 128
RMS_EPS = 1e-6
GN_EPS = 64e-5
NEG_INF = -1e30

COL_Q, COL_K, COL_V = 0, WIDTH, 2 * WIDTH
COL_RKV = 3 * WIDTH
COL_GA = 6 * WIDTH
COL_GB = COL_GA + D_MODEL
COL_LORA = COL_GB + D_MODEL
PROJ_COLS = COL_LORA + LORA_COLS

VMEM_LIMIT = 56 * 1024 * 1024
VMEM_LIMIT_MAX = 60 * 1024 * 1024

FFN_TM = 512
FFN_TF = 512
PROJ_TM = 1024
PROJ_TN = 1536
MERGE_TM = 256
PREP_TM = 256
SCAN_TB = 128
ATT_Q = 2 * CHUNK
ATT_KEYS = PAST_BAND + ATT_Q
ATT_LEAD = PAST_BAND // ATT_Q
ATT_DIAG = ATT_KEYS + ATT_Q
LANE_GROUP_BATCH = 4
KLO = HEAD_DIM // 2
STEP_PITCH = KLO + 8
XT_PITCH = HEAD_DIM + 8


def _sigmoid(x):
    return 1.0 / (1.0 + jnp.exp(-x))


def _split_bf16(x):
    hi = x.astype(BF16)
    lo = (x - hi.astype(F32)).astype(BF16)
    return hi, lo


def _dot(a, b):
    return jnp.dot(a, b, preferred_element_type=F32)


def _dot_f32(a, b_hi, b_lo):
    a_hi, a_lo = _split_bf16(a)
    return _dot(a_hi, b_hi) + _dot(a_lo, b_hi) + _dot(a_hi, b_lo)


def _segsum(x, ones_bd):
    hi, lo = _split_bf16(x)
    outs = []
    for j in range(x.shape[-1] // 128):
        sl = slice(j * 128, (j + 1) * 128)
        outs.append(_dot(hi[:, sl], ones_bd) + _dot(lo[:, sl], ones_bd))
    return jnp.concatenate(outs, axis=-1)


def _rms_scale(x):
    return lax.rsqrt(jnp.mean(x * x, axis=-1, keepdims=True) + RMS_EPS)


def _ffn_body(x_ref, gpre_ref, gpost_ref, wg_ref, wu_ref, wd_ref, o_ref, h_ref, *, n_f):
    f = pl.program_id(1)

    @pl.when(f == 0)
    def _():
        x = x_ref[...]
        h_ref[...] = (x * _rms_scale(x) * gpre_ref[...]).astype(BF16)
        o_ref[...] = jnp.zeros_like(o_ref)

    h = h_ref[...]
    g = _dot(h, wg_ref[...])
    u = _dot(h, wu_ref[...])
    act = (g * _sigmoid(g) * u).astype(BF16)
    o_ref[...] += _dot(act, wd_ref[...])

    @pl.when(f == n_f - 1)
    def _():
        y = o_ref[...]
        o_ref[...] = x_ref[...] + 0.5 * (y * _rms_scale(y) * gpost_ref[...])


def _ffn(x, g_pre, g_post, w_in, w_down):
    m = x.shape[0]
    tm = min(FFN_TM, m)
    n_m, n_f = m // tm, D_FF // FFN_TF
    return pl.pallas_call(
        functools.partial(_ffn_body, n_f=n_f),
        grid=(n_m, n_f),
        in_specs=[
            pl.BlockSpec((tm, D_MODEL), lambda i, f: (i, 0)),
            pl.BlockSpec((1, D_MODEL), lambda i, f: (0, 0)),
            pl.BlockSpec((1, D_MODEL), lambda i, f: (0, 0)),
            pl.BlockSpec((D_MODEL, FFN_TF), lambda i, f: (0, f)),
            pl.BlockSpec((D_MODEL, FFN_TF), lambda i, f: (0, f + n_f)),
            pl.BlockSpec((FFN_TF, D_MODEL), lambda i, f: (f, 0)),
        ],
        out_specs=pl.BlockSpec((tm, D_MODEL), lambda i, f: (i, 0)),
        out_shape=jax.ShapeDtypeStruct((m, D_MODEL), F32),
        scratch_shapes=[pltpu.VMEM((tm, D_MODEL), BF16)],
        compiler_params=pltpu.CompilerParams(
            dimension_semantics=("parallel", "arbitrary"), vmem_limit_bytes=VMEM_LIMIT),
        name="ffn",
    )(x, g_pre, g_post, w_in, w_in, w_down)


def _repack_body(w_ref, o_ref):
    o_lora = COL_GA
    o_ga = o_lora + DECAY_LORA + ICLR_LORA + GATE_LORA
    rows = w_ref.shape[0]

    def padded(lo, width):
        return jnp.concatenate([w_ref[:, lo:lo + width], jnp.zeros((rows, LORA_PAD - width), F32)], axis=1)

    o_ref[:, 0:COL_GA] = w_ref[:, 0:o_lora].astype(BF16)
    o_ref[:, COL_GA:COL_LORA] = w_ref[:, o_ga:].astype(BF16)
    o_ref[:, COL_LORA:COL_LORA + LORA_PAD] = padded(o_lora, DECAY_LORA).astype(BF16)
    o_ref[:, COL_LORA + LORA_PAD:COL_LORA + 2 * LORA_PAD] = padded(o_lora + DECAY_LORA, ICLR_LORA).astype(BF16)
    o_ref[:, COL_LORA + 2 * LORA_PAD:] = w_ref[:, o_lora + DECAY_LORA + ICLR_LORA:o_ga].astype(BF16)


def _repack_proj_weight(w):
    k, n = w.shape
    tk = 256
    return pl.pallas_call(
        _repack_body,
        grid=(k // tk,),
        in_specs=[pl.BlockSpec((tk, n), lambda i: (i, 0))],
        out_specs=pl.BlockSpec((tk, PROJ_COLS), lambda i: (i, 0)),
        out_shape=jax.ShapeDtypeStruct((k, PROJ_COLS), BF16),
        compiler_params=pltpu.CompilerParams(
            dimension_semantics=("parallel",), vmem_limit_bytes=VMEM_LIMIT),
        name="repack_proj_weight",
    )(w)


def _proj_body(x_ref, g_ref, w_ref, o_ref, h_ref):
    @pl.when(pl.program_id(1) == 0)
    def _():
        x = x_ref[...]
        h_ref[...] = (x * _rms_scale(x) * g_ref[...]).astype(BF16)

    o_ref[...] = _dot(h_ref[...], w_ref[...])


def _proj(x, g, w):
    m = x.shape[0]
    tm = min(PROJ_TM, m)
    return pl.pallas_call(
        _proj_body,
        grid=(m // tm, PROJ_COLS // PROJ_TN),
        in_specs=[
            pl.BlockSpec((tm, D_MODEL), lambda i, n: (i, 0)),
            pl.BlockSpec((1, D_MODEL), lambda i, n: (0, 0)),
            pl.BlockSpec((D_MODEL, PROJ_TN), lambda i, n: (0, n)),
        ],
        out_specs=pl.BlockSpec((tm, PROJ_TN), lambda i, n: (i, n)),
        out_shape=jax.ShapeDtypeStruct((m, PROJ_COLS), F32),
        scratch_shapes=[pltpu.VMEM((tm, D_MODEL), BF16)],
        compiler_params=pltpu.CompilerParams(
            dimension_semantics=("parallel", "arbitrary"), vmem_limit_bytes=VMEM_LIMIT),
        name="proj",
    )(x, g, w)


def _softmax_pv(parts):
    m = functools.reduce(jnp.maximum, [jnp.max(s, axis=-1, keepdims=True) for s, _ in parts])
    num, den = 0.0, 0.0
    for s, v in parts:
        p = jnp.exp(s - m)
        den = den + jnp.sum(p, axis=-1, keepdims=True)
        num = num + _dot(p.astype(BF16), v)
    return num / den


def _qk(q, k):
    return lax.dot_general(q, k, (((1,), (1,)), ((), ())), preferred_element_type=F32)


def _attn_prompt_body(q_ref, k_ref, v_ref, diag_ref, o_ref, kbf_ref, vbf_ref, bias_ref):
    c = pl.program_id(1)
    shift = jnp.maximum(ATT_LEAD - c, 0)

    @pl.when(c == 0)
    def _():
        kbf_ref[...] = k_ref[...].astype(BF16)
        vbf_ref[...] = v_ref[...].astype(BF16)

    @pl.when((pl.program_id(0) == 0) & (c <= ATT_LEAD))
    def _():
        key = lax.broadcasted_iota(jnp.int32, (ATT_KEYS, ATT_Q), 0) // CHUNK
        qry = lax.broadcasted_iota(jnp.int32, (ATT_KEYS, ATT_Q), 1) // CHUNK
        dist = (LEFT_CHUNKS + qry) - (key + shift * (ATT_Q // CHUNK))
        in_band = (dist >= 0) & (dist <= LEFT_CHUNKS)
        for h in range(HEADS):
            diag = jnp.broadcast_to(diag_ref[shift, pl.ds(h, 1), :], (ATT_Q, ATT_DIAG))
            toep = pltpu.roll(diag, ATT_DIAG - ATT_Q, axis=1, stride=1, stride_axis=0)
            bias_ref[shift, h] = jnp.where(in_band, toep[:, :ATT_KEYS].T, NEG_INF)

    start = pl.multiple_of(jnp.maximum(c - ATT_LEAD, 0) * ATT_Q, ATT_Q)
    lane = lax.broadcasted_iota(jnp.int32, (ATT_Q, 128), 1)
    for pair in range(HEADS // 2):
        sl = slice(pair * 128, (pair + 1) * 128)
        kp = kbf_ref[pl.ds(start, ATT_KEYS), sl]
        vp = vbf_ref[pl.ds(start, ATT_KEYS), sl]
        qp = q_ref[:, sl] * (HEAD_DIM ** -0.5)
        halves = []
        for half in range(2):
            qm = jnp.where((lane >= HEAD_DIM) == bool(half), qp, 0.0).astype(BF16)
            s = _qk(kp, qm) + bias_ref[shift, 2 * pair + half]
            p = jnp.exp(s - jnp.max(s, axis=0, keepdims=True))
            p = (p * (1.0 / jnp.sum(p, axis=0, keepdims=True))).astype(BF16)
            halves.append(lax.dot_general(p, vp, (((0,), (0,)), ((), ())),
                                          preferred_element_type=F32))
        o_ref[:, sl] = jnp.where(lane < HEAD_DIM, halves[0], halves[1]).astype(BF16)


def _attn_prompt(proj, diag, batch, seq):
    n_c = seq // ATT_Q
    once = dict(pipeline_mode=pl.Buffered(1))
    return pl.pallas_call(
        _attn_prompt_body,
        grid=(batch, n_c),
        in_specs=[
            pl.BlockSpec((ATT_Q, WIDTH), lambda b, c: (b * n_c + c, COL_Q // WIDTH)),
            pl.BlockSpec((seq, WIDTH), lambda b, c: (b, COL_K // WIDTH), **once),
            pl.BlockSpec((seq, WIDTH), lambda b, c: (b, COL_V // WIDTH), **once),
            pl.BlockSpec(diag.shape, lambda b, c: (0, 0, 0)),
        ],
        out_specs=pl.BlockSpec((ATT_Q, WIDTH), lambda b, c: (b * n_c + c, 0)),
        out_shape=jax.ShapeDtypeStruct((batch * seq, WIDTH), BF16),
        scratch_shapes=[pltpu.VMEM((seq, WIDTH), BF16), pltpu.VMEM((seq, WIDTH), BF16),
                        pltpu.VMEM((ATT_LEAD + 1, HEADS, ATT_KEYS, ATT_Q), F32)],
        compiler_params=pltpu.CompilerParams(
            dimension_semantics=("arbitrary", "arbitrary"), vmem_limit_bytes=VMEM_LIMIT_MAX),
        name="attn_prompt",
    )(proj, proj, proj, diag)


def _attn_sample_body(q_ref, kn_ref, vn_ref, ck_ref, cv_ref, bias_c_ref, bias_n_ref, o_ref):
    t = q_ref.shape[0]
    lane = lax.broadcasted_iota(jnp.int32, (t, 128), 1)
    slabs = [slice(pair * 128, (pair + 1) * 128) for pair in range(HEADS // 2)]

    s_c, s_n = [], []
    for h in range(HEADS):
        sl = slabs[h // 2]
        qm = jnp.where((lane >= HEAD_DIM) == bool(h % 2), q_ref[:, sl] * (HEAD_DIM ** -0.5), 0.0).astype(BF16)
        s_c.append(_qk(qm, ck_ref[0, :, sl].astype(BF16)) + bias_c_ref[h])
        s_n.append(_qk(qm, kn_ref[:, sl].astype(BF16)) + bias_n_ref[h])
    s_c = jnp.concatenate(s_c, axis=0)
    s_n = jnp.concatenate(s_n, axis=0)
    m = jnp.maximum(jnp.max(s_c, axis=-1, keepdims=True), jnp.max(s_n, axis=-1, keepdims=True))
    p_c = jnp.exp(s_c - m)
    p_n = jnp.exp(s_n - m)
    inv = 1.0 / (jnp.sum(p_c, axis=-1, keepdims=True) + jnp.sum(p_n, axis=-1, keepdims=True))
    p_c = (p_c * inv).astype(BF16)
    p_n = (p_n * inv).astype(BF16)
    for pair, sl in enumerate(slabs):
        cv = cv_ref[0, :, sl].astype(BF16)
        vn = vn_ref[:, sl].astype(BF16)
        halves = []
        for h in (2 * pair, 2 * pair + 1):
            rows = slice(h * t, (h + 1) * t)
            halves.append(_dot(p_c[rows], cv) + _dot(p_n[rows], vn))
        o_ref[:, sl] = jnp.where(lane < HEAD_DIM, halves[0], halves[1]).astype(BF16)


def _attn_sample(proj, row0, cache_k, cache_v, bias_c, bias_n, batch, t):
    n_past = cache_k.shape[1]
    blk0 = row0 // t
    return pl.pallas_call(
        _attn_sample_body,
        grid=(batch,),
        in_specs=[
            pl.BlockSpec((t, WIDTH), lambda b: (blk0 + b, COL_Q // WIDTH)),
            pl.BlockSpec((t, WIDTH), lambda b: (blk0 + b, COL_K // WIDTH)),
            pl.BlockSpec((t, WIDTH), lambda b: (blk0 + b, COL_V // WIDTH)),
            pl.BlockSpec((1, n_past, WIDTH), lambda b: (b, 0, 0)),
            pl.BlockSpec((1, n_past, WIDTH), lambda b: (b, 0, 0)),
            pl.BlockSpec((HEADS, t, n_past), lambda b: (0, 0, 0)),
            pl.BlockSpec((HEADS, t, t), lambda b: (0, 0, 0)),
        ],
        out_specs=pl.BlockSpec((t, WIDTH), lambda b: (b, 0)),
        out_shape=jax.ShapeDtypeStruct((batch * t, WIDTH), BF16),
        compiler_params=pltpu.CompilerParams(
            dimension_semantics=("parallel",), vmem_limit_bytes=VMEM_LIMIT),
        name="attn_sample",
    )(proj, proj, proj, cache_k, cache_v, bias_c, bias_n)


def _prep_body(p_ref, pl_ref, bp_ref, bl_ref, mu_ref, mul_ref, w0_ref, a0_ref, kk_ref, ka_ref, rk_ref,
               lnb_ref, wup_hi, wup_lo, aup_hi, aup_lo, gup_hi, gup_lo, bd_ref,
               r_o, w_o, k_o, v_o, kk_o, kka_o, g_o, cb_o):
    def shifted(p, first_row, mu):
        row = lax.broadcasted_iota(jnp.int32, p.shape, 0)
        prev = jnp.where(row == 0, first_row, pltpu.roll(p, 1, axis=0))
        return p + (prev - p) * mu

    xm = shifted(p_ref[...], bp_ref[0], mu_ref[...])
    xl = shifted(pl_ref[...], bl_ref[0], mul_ref[...])
    r = xm[:, 0:WIDTH]
    k = xm[:, WIDTH:2 * WIDTH]
    v = xm[:, 2 * WIDTH:3 * WIDTH]
    wd = xl[:, 0:LORA_PAD]
    ad = xl[:, LORA_PAD:2 * LORA_PAD]
    gd = xl[:, 2 * LORA_PAD:]

    z = -(w0_ref[...] + _dot_f32(jnp.tanh(wd), wup_hi[...], wup_lo[...]))
    softplus = jnp.maximum(z, 0.0) + jnp.log(1.0 + jnp.exp(-jnp.abs(z)))
    decay = jnp.exp(-jnp.exp(-softplus - 0.5))
    a = _sigmoid(a0_ref[...] + _dot_f32(ad, aup_hi[...], aup_lo[...]))
    g = _dot_f32(_sigmoid(gd), gup_hi[...], gup_lo[...])

    bd = bd_ref[...]
    kk = k * kk_ref[...]
    kk = kk / jnp.maximum(jnp.sqrt(_segsum(kk * kk, bd)), 1e-12)
    k2 = k * (1.0 + (a - 1.0) * ka_ref[...])
    bonus = _segsum(r * k2 * rk_ref[...], bd) * v

    r_o[...] = r
    w_o[...] = decay
    k_o[...] = k2
    v_o[...] = v
    kk_o[...] = kk
    kka_o[...] = kk * a
    g_o[...] = g
    cb_o[...] = lnb_ref[...] + bonus


def _prep(proj, row0, n_rows, tm, first_rkv, first_lora, params):
    n_t = n_rows // tm
    blk0 = row0 // tm
    vec = lambda w: pl.BlockSpec((1, w), lambda i: (0, 0))
    mat = lambda r: pl.BlockSpec((r, WIDTH), lambda i: (0, 0))
    out = pl.BlockSpec((tm, WIDTH), lambda i: (i, 0))
    return pl.pallas_call(
        _prep_body,
        grid=(n_t,),
        in_specs=[
            pl.BlockSpec((tm, 3 * WIDTH), lambda i: (blk0 + i, COL_RKV // (3 * WIDTH))),
            pl.BlockSpec((tm, LORA_COLS), lambda i: (blk0 + i, COL_LORA // LORA_COLS)),
            pl.BlockSpec((1, 1, 3 * WIDTH), lambda i: (i, 0, 0)),
            pl.BlockSpec((1, 1, LORA_COLS), lambda i: (i, 0, 0)),
            vec(3 * WIDTH), vec(LORA_COLS), vec(WIDTH), vec(WIDTH), vec(WIDTH), vec(WIDTH), vec(WIDTH),
            vec(WIDTH), mat(LORA_PAD), mat(LORA_PAD), mat(LORA_PAD), mat(LORA_PAD),
            mat(GATE_LORA), mat(GATE_LORA),
            pl.BlockSpec((128, 128), lambda i: (0, 0)),
        ],
        out_specs=[out] * 8,
        out_shape=[jax.ShapeDtypeStruct((n_rows, WIDTH), F32)] * 8,
        compiler_params=pltpu.CompilerParams(
            dimension_semantics=("parallel",), vmem_limit_bytes=VMEM_LIMIT),
        name="rwkv_prep",
    )(proj, proj, first_rkv, first_lora, *params)


def _scan_steps(s_ref, row, get_v, put_y, n_steps):
    W, KKA, K, R, KK = range(5)
    zero = jnp.zeros((KLO, 128), F32)

    sa0 = [zero, zero]
    for k in range(HEAD_DIM):
        sa0[k % 2] = sa0[k % 2] - s_ref[k] * row(KK, 0, k)

    def step(t, sa):
        v = get_v(t)
        t_next = jnp.minimum(t + 1, n_steps - 1)
        y = [zero, zero]
        sa_next = [zero, zero]
        for k in range(HEAD_DIM):
            s_new = s_ref[k] * row(W, t, k) + sa * row(KKA, t, k) + v * row(K, t, k)
            s_ref[k] = s_new
            y[k % 2] = y[k % 2] + s_new * row(R, t, k)
            sa_next[k % 2] = sa_next[k % 2] - s_new * row(KK, t_next, k)
        put_y(t, y[0] + y[1])
        return sa_next[0] + sa_next[1]

    lax.fori_loop(0, n_steps, step, sa0[0] + sa0[1])


def _scan_body(w_ref, kka_ref, k_ref, r_ref, kk_ref, v_ref, s0_ref, y_ref, st_ref, s_ref, rows_ref, *, tb):
    @pl.when(pl.program_id(1) == 0)
    def _():
        s_ref[...] = s0_ref[0]

    lane = lax.broadcasted_iota(jnp.int32, (tb * KLO, 128), 1)
    for a, ref in enumerate((w_ref, kka_ref, k_ref, r_ref, kk_ref)):
        x = ref[0].reshape(tb * KLO, 128)
        swapped = pltpu.roll(x, HEAD_DIM, axis=1)
        rows_ref[a, :, 0:KLO, :] = jnp.where(lane < HEAD_DIM, x, swapped).reshape(tb, KLO, 128)
        rows_ref[a, :, KLO:, :] = jnp.where(lane < HEAD_DIM, swapped, x).reshape(tb, KLO, 128)

    def row(a, t, k):
        return jnp.broadcast_to(rows_ref[a, t, pl.ds(k, 1), :], (KLO, 128))

    def put_y(t, y):
        y_ref[0, t] = y

    _scan_steps(s_ref, row, lambda t: v_ref[0, t], put_y, tb)

    @pl.when(pl.program_id(1) == pl.num_programs(1) - 1)
    def _():
        st_ref[0] = s_ref[...]


def _scan_tok_body(w_hbm, kka_hbm, k_hbm, r_hbm, kk_hbm, v_hbm, s0_ref, y_ref, st_ref,
                   s_ref, rows_ref, xt_ref, vt_ref, yt_ref, in_ref, in_sem, *, tb):
    step = pl.program_id(0)
    operands = (w_hbm, kka_hbm, k_hbm, r_hbm, kk_hbm, v_hbm)

    def block_copy(a, blk):
        return pltpu.make_async_copy(operands[a].at[:, pl.ds(blk * tb, tb), :], in_ref.at[a], in_sem.at[a])

    @pl.when(step == 0)
    def _():
        for a in range(len(operands)):
            block_copy(a, 0).start()
        s_ref[...] = s0_ref[0]
        xt_ref[...] = jnp.zeros_like(xt_ref)

    def head_rows(c):
        return pl.ds(c, HEADS, stride=XT_PITCH)

    def channel_major(ref):
        for b in range(LANE_GROUP_BATCH):
            xt = ref[b].T
            for h in range(HEADS):
                xt_ref[b, h * XT_PITCH:h * XT_PITCH + HEAD_DIM, 0:tb] = xt[h * HEAD_DIM:(h + 1) * HEAD_DIM]

    def lane_tile(chan):
        parts = [xt_ref[b, head_rows(c), :] for c in chan for b in range(LANE_GROUP_BATCH)]
        return jnp.concatenate(parts, axis=0).T[0:tb]

    for a in range(5):
        block_copy(a, step).wait()
        channel_major(in_ref.at[a])
        for k in range(HEAD_DIM):
            rows_ref[a, k] = lane_tile((k, k))
    block_copy(5, step).wait()
    channel_major(in_ref.at[5])
    for j in range(KLO):
        vt_ref[pl.ds(j, tb, stride=STEP_PITCH), :] = lane_tile((j, KLO + j))

    @pl.when(step + 1 < pl.num_programs(0))
    def _():
        for a in range(len(operands)):
            block_copy(a, step + 1).start()

    def row(a, t, k):
        return jnp.broadcast_to(rows_ref[a, k, pl.ds(t, 1), :], (KLO, 128))

    def step_rows(t):
        return pl.ds(pl.multiple_of(t * STEP_PITCH, 8), KLO)

    def put_y(t, y):
        yt_ref[step_rows(t), :] = y

    _scan_steps(s_ref, row, lambda t: vt_ref[step_rows(t), :], put_y, tb)

    for j in range(KLO):
        steps = yt_ref[pl.ds(j, tb, stride=STEP_PITCH), :]
        if tb < 128:
            steps = jnp.concatenate([steps, jnp.zeros((128 - tb, 128), F32)], axis=0)
        tile = steps.T
        for half in range(2):
            for b in range(LANE_GROUP_BATCH):
                r0 = (half * LANE_GROUP_BATCH + b) * HEADS
                xt_ref[b, head_rows(half * KLO + j), :] = tile[r0:r0 + HEADS]
    for b in range(LANE_GROUP_BATCH):
        heads = [xt_ref[b, h * XT_PITCH:h * XT_PITCH + HEAD_DIM, 0:tb] for h in range(HEADS)]
        y_ref[b] = jnp.concatenate(heads, axis=0).T

    @pl.when(pl.program_id(0) == pl.num_programs(0) - 1)
    def _():
        st_ref[0] = s_ref[...]


def _scan_tok(ops, v, s0, tb):
    nb, t, _ = v.shape
    tok_spec = pl.BlockSpec((nb, tb, WIDTH), lambda j: (0, j, 0))
    st_spec = pl.BlockSpec((1, HEAD_DIM, KLO, 128), lambda j: (0, 0, 0, 0))
    return pl.pallas_call(
        functools.partial(_scan_tok_body, tb=tb),
        grid=(t // tb,),
        in_specs=[pl.BlockSpec(memory_space=pl.ANY)] * 6 + [st_spec],
        out_specs=[tok_spec, st_spec],
        out_shape=[jax.ShapeDtypeStruct((nb, t, WIDTH), F32),
                   jax.ShapeDtypeStruct((1, HEAD_DIM, KLO, 128), F32)],
        scratch_shapes=[pltpu.VMEM((HEAD_DIM, KLO, 128), F32),
                        pltpu.VMEM((5, HEAD_DIM, tb, 128), F32),
                        pltpu.VMEM((nb, HEADS * XT_PITCH, 128), F32),
                        pltpu.VMEM((tb * STEP_PITCH, 128), F32),
                        pltpu.VMEM((tb * STEP_PITCH, 128), F32),
                        pltpu.VMEM((6, nb, tb, WIDTH), F32),
                        pltpu.SemaphoreType.DMA((6,))],
        compiler_params=pltpu.CompilerParams(
            dimension_semantics=("arbitrary",), vmem_limit_bytes=VMEM_LIMIT),
        name="rwkv_scan_tok",
    )(*ops, v, s0)


def _scan(rows, v, s0, tb):
    g, t = v.shape[0], v.shape[1]
    row_spec = pl.BlockSpec((1, tb, KLO, 128), lambda i, j: (i, j, 0, 0))
    st_spec = pl.BlockSpec((1, HEAD_DIM, KLO, 128), lambda i, j: (i, 0, 0, 0))
    return pl.pallas_call(
        functools.partial(_scan_body, tb=tb),
        grid=(g, t // tb),
        in_specs=[row_spec] * 6 + [st_spec],
        out_specs=[row_spec, st_spec],
        out_shape=[jax.ShapeDtypeStruct((g, t, KLO, 128), F32),
                   jax.ShapeDtypeStruct((g, HEAD_DIM, KLO, 128), F32)],
        scratch_shapes=[pltpu.VMEM((HEAD_DIM, KLO, 128), F32),
                        pltpu.VMEM((5, tb, HEAD_DIM, 128), F32)],
        compiler_params=pltpu.CompilerParams(
            dimension_semantics=("parallel", "arbitrary"), vmem_limit_bytes=VMEM_LIMIT),
        name="rwkv_scan",
    )(*rows, v, s0)


def _rows_to_scan(x, n_batch, t):
    g = n_batch // LANE_GROUP_BATCH
    x = x.reshape(g, LANE_GROUP_BATCH, t, HEADS, 2, KLO)
    return x.transpose(0, 2, 5, 4, 1, 3).reshape(g, t, KLO, 128)


def _rows_from_scan(y, n_batch, t):
    g = n_batch // LANE_GROUP_BATCH
    y = y.reshape(g, t, KLO, 2, LANE_GROUP_BATCH, HEADS).transpose(0, 4, 1, 5, 3, 2)
    return y.reshape(n_batch * t, WIDTH)


def _state_to_scan(s, n_batch):
    g = n_batch // LANE_GROUP_BATCH
    s = s.reshape(g, LANE_GROUP_BATCH, HEADS, 2, KLO, HEAD_DIM)
    return s.transpose(0, 5, 4, 3, 1, 2).reshape(g, HEAD_DIM, KLO, 128)


def _state_from_scan(s, n_batch):
    g = n_batch // LANE_GROUP_BATCH
    s = s.reshape(g, HEAD_DIM, KLO, 2, LANE_GROUP_BATCH, HEADS)
    return s.transpose(0, 4, 5, 3, 2, 1).reshape(n_batch, HEADS, HEAD_DIM, HEAD_DIM)


def _merge_body(att_ref, y_ref, g_ref, cb_ref, ga_ref, gb_ref, x_ref, lnw_ref, gpost_ref, bd_ref,
                wa_ref, wr_ref, wo_ref, o_ref):
    bd = bd_ref[...]
    y = y_ref[...]
    d = y - _segsum(y, bd) * (1.0 / HEAD_DIM)
    var = _segsum(d * d, bd) * (1.0 / HEAD_DIM)
    rw = ((d * lax.rsqrt(var + GN_EPS) * lnw_ref[...] + cb_ref[...]) * g_ref[...]).astype(BF16)
    rw_o = _dot(rw, wr_ref[...])
    att_o = _dot(att_ref[...], wa_ref[...])
    mixed = (_sigmoid(ga_ref[...]) * att_o + _sigmoid(gb_ref[...]) * rw_o).astype(BF16)
    z = _dot(mixed, wo_ref[...])
    o_ref[...] = x_ref[...] + z * _rms_scale(z) * gpost_ref[...]


def _merge(att, y, g, cb, proj, x, ln_w, g_post, ones_bd, w_att, w_rwkv, w_out):
    m = x.shape[0]
    tm = min(MERGE_TM, m)
    tok = lambda w: pl.BlockSpec((tm, w), lambda i: (i, 0))
    const = lambda a: pl.BlockSpec(a.shape, lambda i: (0,) * a.ndim)
    return pl.pallas_call(
        _merge_body,
        grid=(m // tm,),
        in_specs=[
            tok(WIDTH), tok(WIDTH), tok(WIDTH), tok(WIDTH),
            pl.BlockSpec((tm, D_MODEL), lambda i: (i, COL_GA // D_MODEL)),
            pl.BlockSpec((tm, D_MODEL), lambda i: (i, COL_GB // D_MODEL)),
            tok(D_MODEL), const(ln_w), const(g_post), const(ones_bd),
            const(w_att), const(w_rwkv), const(w_out),
        ],
        out_specs=tok(D_MODEL),
        out_shape=jax.ShapeDtypeStruct((m, D_MODEL), F32),
        compiler_params=pltpu.CompilerParams(
            dimension_semantics=("parallel",), vmem_limit_bytes=VMEM_LIMIT),
        name="merge",
    )(att, y, g, cb, proj, proj, x, ln_w, g_post, ones_bd, w_att, w_rwkv, w_out)


def _pad_cols(x, width):
    return jnp.pad(x, ((0, 0), (0, width - x.shape[1])))


def _pad_rows(x, height):
    return jnp.pad(x, ((0, height - x.shape[0]), (0, 0)))


def _reorder_shift_cols(x):
    o = 3 * WIDTH
    wd = x[..., o:o + DECAY_LORA]
    ad = x[..., o + DECAY_LORA:o + DECAY_LORA + ICLR_LORA]
    gd = x[..., o + DECAY_LORA + ICLR_LORA:]
    pad = [(0, 0)] * (x.ndim - 1)
    lora = jnp.concatenate([jnp.pad(wd, pad + [(0, LORA_PAD - DECAY_LORA)]),
                            jnp.pad(ad, pad + [(0, LORA_PAD - ICLR_LORA)]), gd], axis=-1)
    return x[..., :o], lora


def _shift_row_from_proj(rows):
    lo = rows[:, COL_LORA:]
    return jnp.concatenate([rows[:, COL_RKV:COL_RKV + 3 * WIDTH], lo[:, :DECAY_LORA],
                            lo[:, LORA_PAD:LORA_PAD + ICLR_LORA], lo[:, 2 * LORA_PAD:]], axis=-1)


def _bias_diagonals(rel_bias):
    u = np.arange(ATT_DIAG)[None, :]
    s = np.arange(ATT_LEAD + 1)[:, None]
    idx = np.clip(ATT_KEYS - u - ATT_Q * s, -MAX_REL, MAX_REL) + MAX_REL
    tab = jnp.take(rel_bias, jnp.asarray(idx.reshape(-1), jnp.int32), axis=1)
    return tab.reshape(HEADS, ATT_LEAD + 1, ATT_DIAG).transpose(1, 0, 2)


def _bias_sample(rel_bias, t, n_past):
    rel = np.arange(t)[:, None] + n_past - np.arange(n_past + t)[None, :]
    idx = np.clip(rel, -MAX_REL, MAX_REL) + MAX_REL
    tab = jnp.take(rel_bias, jnp.asarray(idx.reshape(-1), jnp.int32), axis=1).reshape(HEADS, t, n_past + t)
    return tab[:, :, :n_past], tab[:, :, n_past:]


def _layer(x, n_batch, t, attend, first_rows, prep_tm, wkv0, scan_tb, wts):
    m = n_batch * t
    x = _ffn(x, *wts["ffn1"])
    proj = _proj(x, *wts["proj"])
    att = attend(proj)

    prep = _prep(proj, 0, m, prep_tm, *first_rows(proj), wts["prep"])
    r, wdec, k2, v, kk, kka, g, cb = prep
    ops = (wdec, kka, k2, r, kk)
    if n_batch == LANE_GROUP_BATCH and scan_tb % 64 == 0:
        tok = lambda a: a.reshape(n_batch, t, WIDTH)
        y, s_t = _scan_tok([tok(a) for a in ops], tok(v), _state_to_scan(wkv0, n_batch), scan_tb)
        y = y.reshape(m, WIDTH)
    else:
        rows = [_rows_to_scan(a, n_batch, t) for a in ops]
        y, s_t = _scan(rows, _rows_to_scan(v, n_batch, t), _state_to_scan(wkv0, n_batch), scan_tb)
        y = _rows_from_scan(y, n_batch, t)

    x = _merge(att, y, g, cb, proj, x, *wts["merge"])
    x = _ffn(x, *wts["ffn2"])

    last = jnp.concatenate([proj[(b + 1) * t - 1:(b + 1) * t] for b in range(n_batch)])
    return x, proj, _state_from_scan(s_t, n_batch), _shift_row_from_proj(last)[:, None]


def kernel(x_prompt, x_sample, cache_k, cache_v, state_wkv, state_shift, norm_ffn1_pre, norm_ffn1_post, w_ffn1_in, w_ffn1_down, norm_mix_pre, norm_mix_post, w_in, rel_bias, w_att_out, rwkv_mu, rwkv_w0, rwkv_w_up, rwkv_a0, rwkv_a_up, rwkv_g_up, rwkv_k_k, rwkv_k_a, rwkv_r_k, rwkv_ln_w, rwkv_ln_b, w_rwkv_out, w_out, norm_ffn2_pre, norm_ffn2_post, w_ffn2_in, w_ffn2_down):
    depth = w_in.shape[0]
    batch, seq, _ = x_prompt.shape
    dec_batch, dec_seq, _ = x_sample.shape
    n_past = cache_k.shape[2]
    m_p, m_s = batch * seq, dec_batch * dec_seq
    keep = min(PAST_BAND, seq)
    assert n_past == PAST_BAND and seq % ATT_Q == 0 and seq // ATT_Q > ATT_LEAD
    assert m_p % FFN_TM == 0 and m_p % PROJ_TM == 0 and m_p % MERGE_TM == 0 and seq % PREP_TM == 0
    assert seq % SCAN_TB == 0 and batch == LANE_GROUP_BATCH and dec_batch % LANE_GROUP_BATCH == 0

    x_p = x_prompt.reshape(m_p, D_MODEL)
    x_s = x_sample.reshape(m_s, D_MODEL)
    ones_bd = jnp.asarray(np.kron(np.eye(2), np.ones((HEAD_DIM, HEAD_DIM))), BF16)
    row = lambda v: v.reshape(1, -1)
    heads = lambda a, n, t: a.reshape(n, t, HEADS, HEAD_DIM)

    outs = {k: [] for k in ("kp", "vp", "wp", "sp", "ks", "vs", "ws", "ss")}
    for l in range(depth):
        w_proj = _repack_proj_weight(w_in[l])
        mu_rkv, mu_lora = _reorder_shift_cols(rwkv_mu[l][None])
        wts = {
            "ffn1": (row(norm_ffn1_pre[l]), row(norm_ffn1_post[l]),
                     w_ffn1_in[l].astype(BF16), w_ffn1_down[l].astype(BF16)),
            "proj": (row(norm_mix_pre[l]), w_proj),
            "prep": (mu_rkv, mu_lora, row(rwkv_w0[l]), row(rwkv_a0[l]), row(rwkv_k_k[l]), row(rwkv_k_a[l]),
                     row(rwkv_r_k[l]), row(rwkv_ln_b[l]),
                     *_split_bf16(_pad_rows(rwkv_w_up[l], LORA_PAD)),
                     *_split_bf16(_pad_rows(rwkv_a_up[l], LORA_PAD)),
                     *_split_bf16(rwkv_g_up[l]), ones_bd),
            "merge": (row(rwkv_ln_w[l]), row(norm_mix_post[l]), ones_bd,
                      w_att_out[l].astype(BF16), w_rwkv_out[l].astype(BF16), w_out[l].astype(BF16)),
            "ffn2": (row(norm_ffn2_pre[l]), row(norm_ffn2_post[l]),
                     w_ffn2_in[l].astype(BF16), w_ffn2_down[l].astype(BF16)),
        }

        diag = _bias_diagonals(rel_bias[l])
        starts_seq = (jnp.arange(m_p // PREP_TM) % (seq // PREP_TM) == 0)[:, None]

        def first_rows_prompt(proj):
            last_of_tile = proj.reshape(m_p // PREP_TM, PREP_TM, PROJ_COLS)[:, PREP_TM - 1]
            first = jnp.where(starts_seq, 0.0, jnp.roll(last_of_tile, 1, axis=0))[:, None]
            return first[..., COL_RKV:COL_RKV + 3 * WIDTH], first[..., COL_LORA:]

        x_p, proj_p, wkv_p, sh_p = _layer(
            x_p, batch, seq, lambda proj: _attn_prompt(proj, diag, batch, seq), first_rows_prompt, PREP_TM,
            jnp.zeros((batch, HEADS, HEAD_DIM, HEAD_DIM), F32), SCAN_TB, wts)
        tail = lambda col: jnp.stack([proj_p[(b + 1) * seq - keep:(b + 1) * seq, col:col + WIDTH]
                                      for b in range(batch)])
        outs["kp"].append(heads(tail(COL_K), batch, keep))
        outs["vp"].append(heads(tail(COL_V), batch, keep))
        outs["wp"].append(wkv_p.astype(state_wkv.dtype))
        outs["sp"].append(sh_p)

        ck = cache_k[l].reshape(dec_batch, n_past, WIDTH)
        cv = cache_v[l].reshape(dec_batch, n_past, WIDTH)
        bias_c, bias_n = _bias_sample(rel_bias[l], dec_seq, n_past)
        x_s, proj_s, wkv_s, sh_s = _layer(
            x_s, dec_batch, dec_seq,
            lambda proj: _attn_sample(proj, 0, ck, cv, bias_c, bias_n, dec_batch, dec_seq),
            lambda proj: _reorder_shift_cols(state_shift[l]), dec_seq,
            state_wkv[l].astype(F32), dec_seq, wts)
        outs["ks"].append(heads(proj_s[:, COL_K:COL_K + WIDTH], dec_batch, dec_seq))
        outs["vs"].append(heads(proj_s[:, COL_V:COL_V + WIDTH], dec_batch, dec_seq))
        outs["ws"].append(wkv_s.astype(state_wkv.dtype))
        outs["ss"].append(sh_s)

    st = lambda k: jnp.stack(outs[k])
    return (x_p.reshape(batch, seq, D_MODEL), x_s.reshape(dec_batch, dec_seq, D_MODEL),
            st("kp"), st("vp"), st("wp"), st("sp"), st("ks"), st("vs"), st("ws"), st("ss"))
```

```python
import functools

import numpy as np
import jax
import jax.numpy as jnp
from jax import lax
from jax.experimental import pallas as pl
from jax.experimental.pallas import tpu as pltpu

F32 = jnp.float32
BF16 = jnp.bfloat16

D_MODEL = 2048
CHUNK = 64
LEFT_CHUNKS = 8
PAST_BAND = LEFT_CHUNKS * CHUNK
BAND = PAST_BAND + CHUNK
HEAD_DIM = 64
HEADS = 16
WIDTH = HEADS * HEAD_DIM
DECAY_LORA = 96
ICLR_LORA = 96
GATE_LORA = 256
LORA_PAD = 128
LORA_COLS = 2 * LORA_PAD + GATE_LORA
D_FF = 5632
MAX_REL = 128
RMS_EPS = 1e-6
GN_EPS = 64e-5
NEG_INF = -1e30

COL_Q, COL_K, COL_V = 0, WIDTH, 2 * WIDTH
COL_RKV = 3 * WIDTH
COL_GA = 6 * WIDTH
COL_GB = COL_GA + D_MODEL
COL_LORA = COL_GB + D_MODEL
PROJ_COLS = COL_LORA + LORA_COLS

VMEM_LIMIT = 56 * 1024 * 1024
VMEM_LIMIT_MAX = 60 * 1024 * 1024

FFN_TM = 512
FFN_TF = 512
PROJ_TM = 1024
PROJ_TN = 1536
REPACK_ROWS = 768
MERGE_TM = 256
PREP_TM = 256
SCAN_TB = 128
ATT_Q = 2 * CHUNK
ATT_KEYS = PAST_BAND + ATT_Q
ATT_LEAD = PAST_BAND // ATT_Q
ATT_DIAG = ATT_KEYS + ATT_Q
LANE_GROUP_BATCH = 4
KLO = HEAD_DIM // 2
STEP_PITCH = KLO + 8
XT_PITCH = HEAD_DIM + 8


def _sigmoid(x):
    return 1.0 / (1.0 + jnp.exp(-x))


def _split_bf16(x):
    hi = x.astype(BF16)
    lo = (x - hi.astype(F32)).astype(BF16)
    return hi, lo


def _dot(a, b):
    return jnp.dot(a, b, preferred_element_type=F32)


def _dot_f32(a, b_hi, b_lo):
    a_hi, a_lo = _split_bf16(a)
    return _dot(a_hi, b_hi) + _dot(a_lo, b_hi) + _dot(a_hi, b_lo)


def _segsum(x, ones_bd):
    hi, lo = _split_bf16(x)
    outs = []
    for j in range(x.shape[-1] // 128):
        sl = slice(j * 128, (j + 1) * 128)
        outs.append(_dot(hi[:, sl], ones_bd) + _dot(lo[:, sl], ones_bd))
    return jnp.concatenate(outs, axis=-1)


def _rms_scale(x):
    return lax.rsqrt(jnp.mean(x * x, axis=-1, keepdims=True) + RMS_EPS)


def _ffn_body(x_ref, gpre_ref, gpost_ref, wg_ref, wu_ref, wd_ref, o_ref, h_ref, *, n_f):
    f = pl.program_id(1)

    @pl.when(f == 0)
    def _():
        x = x_ref[...]
        h_ref[...] = (x * _rms_scale(x) * gpre_ref[...]).astype(BF16)
        o_ref[...] = jnp.zeros_like(o_ref)

    h = h_ref[...]
    g = _dot(h, wg_ref[...])
    u = _dot(h, wu_ref[...])
    act = (g * _sigmoid(g) * u).astype(BF16)
    o_ref[...] += _dot(act, wd_ref[...])

    @pl.when(f == n_f - 1)
    def _():
        y = o_ref[...]
        o_ref[...] = x_ref[...] + 0.5 * (y * _rms_scale(y) * gpost_ref[...])


def _ffn(x, g_pre, g_post, w_in, w_down):
    m = x.shape[0]
    tm = min(FFN_TM, m)
    n_m, n_f = m // tm, D_FF // FFN_TF
    return pl.pallas_call(
        functools.partial(_ffn_body, n_f=n_f),
        grid=(n_m, n_f),
        in_specs=[
            pl.BlockSpec((tm, D_MODEL), lambda i, f: (i, 0)),
            pl.BlockSpec((1, D_MODEL), lambda i, f: (0, 0)),
            pl.BlockSpec((1, D_MODEL), lambda i, f: (0, 0)),
            pl.BlockSpec((D_MODEL, FFN_TF), lambda i, f: (0, f)),
            pl.BlockSpec((D_MODEL, FFN_TF), lambda i, f: (0, f + n_f)),
            pl.BlockSpec((FFN_TF, D_MODEL), lambda i, f: (f, 0)),
        ],
        out_specs=pl.BlockSpec((tm, D_MODEL), lambda i, f: (i, 0)),
        out_shape=jax.ShapeDtypeStruct((m, D_MODEL), F32),
        scratch_shapes=[pltpu.VMEM((tm, D_MODEL), BF16)],
        compiler_params=pltpu.CompilerParams(
            dimension_semantics=("parallel", "arbitrary"), vmem_limit_bytes=VMEM_LIMIT),
        name="ffn",
    )(x, g_pre, g_post, w_in, w_in, w_down)


def _repack_body(wt_hbm, o_ref, buf_ref, sem, *, n_head):
    j = pl.program_id(0)
    last = pl.num_programs(0) - 1
    o_ga = COL_GA + DECAY_LORA + ICLR_LORA + GATE_LORA

    def row_copy(src_row, n_rows, dst_row, slot):
        return pltpu.make_async_copy(wt_hbm.at[pl.ds(src_row, n_rows), :],
                                     buf_ref.at[pl.ds(dst_row, n_rows), :], sem.at[slot])

    @pl.when(j < last)
    def _():
        src = jnp.where(j < n_head, j * REPACK_ROWS, o_ga + (j - n_head) * REPACK_ROWS)
        copy = row_copy(src, REPACK_ROWS, 0, 0)
        copy.start()
        copy.wait()

    @pl.when(j == last)
    def _():
        n_gate = COL_LORA - (PROJ_COLS - REPACK_ROWS)
        pieces = [(o_ga + COL_LORA - COL_GA - n_gate, n_gate, 0),
                  (COL_GA, DECAY_LORA, n_gate),
                  (COL_GA + DECAY_LORA, ICLR_LORA, n_gate + LORA_PAD),
                  (COL_GA + DECAY_LORA + ICLR_LORA, GATE_LORA, n_gate + 2 * LORA_PAD)]
        for dst, width in ((n_gate, DECAY_LORA), (n_gate + LORA_PAD, ICLR_LORA)):
            buf_ref[dst + width:dst + LORA_PAD, :] = jnp.zeros((LORA_PAD - width, D_MODEL), F32)
        copies = [row_copy(src, n, dst, slot) for slot, (src, n, dst) in enumerate(pieces)]
        for copy in copies:
            copy.start()
        for copy in copies:
            copy.wait()

    o_ref[...] = buf_ref[...].astype(BF16)


def _repack_proj_weight(wt):
    assert COL_GA % REPACK_ROWS == 0 and PROJ_COLS % REPACK_ROWS == 0
    assert PROJ_COLS - REPACK_ROWS <= COL_LORA <= PROJ_COLS - LORA_COLS
    return pl.pallas_call(
        functools.partial(_repack_body, n_head=COL_GA // REPACK_ROWS),
        grid=(PROJ_COLS // REPACK_ROWS,),
        in_specs=[pl.BlockSpec(memory_space=pl.ANY)],
        out_specs=pl.BlockSpec((REPACK_ROWS, D_MODEL), lambda j: (j, 0)),
        out_shape=jax.ShapeDtypeStruct((PROJ_COLS, D_MODEL), BF16),
        scratch_shapes=[pltpu.VMEM((REPACK_ROWS, D_MODEL), F32), pltpu.SemaphoreType.DMA((4,))],
        compiler_params=pltpu.CompilerParams(
            dimension_semantics=("arbitrary",), vmem_limit_bytes=VMEM_LIMIT),
        name="repack_proj_weight",
    )(wt)


def _proj_body(x_ref, g_ref, w_ref, o_ref, h_ref):
    @pl.when(pl.program_id(1) == 0)
    def _():
        x = x_ref[...]
        h_ref[...] = (x * _rms_scale(x) * g_ref[...]).astype(BF16)

    o_ref[...] = _qk(h_ref[...], w_ref[...])


def _proj(x, g, w):
    m = x.shape[0]
    tm = min(PROJ_TM, m)
    return pl.pallas_call(
        _proj_body,
        grid=(m // tm, PROJ_COLS // PROJ_TN),
        in_specs=[
            pl.BlockSpec((tm, D_MODEL), lambda i, n: (i, 0)),
            pl.BlockSpec((1, D_MODEL), lambda i, n: (0, 0)),
            pl.BlockSpec((PROJ_TN, D_MODEL), lambda i, n: (n, 0)),
        ],
        out_specs=pl.BlockSpec((tm, PROJ_TN), lambda i, n: (i, n)),
        out_shape=jax.ShapeDtypeStruct((m, PROJ_COLS), F32),
        scratch_shapes=[pltpu.VMEM((tm, D_MODEL), BF16)],
        compiler_params=pltpu.CompilerParams(
            dimension_semantics=("parallel", "arbitrary"), vmem_limit_bytes=VMEM_LIMIT),
        name="proj",
    )(x, g, w)


def _softmax_pv(parts):
    m = functools.reduce(jnp.maximum, [jnp.max(s, axis=-1, keepdims=True) for s, _ in parts])
    num, den = 0.0, 0.0
    for s, v in parts:
        p = jnp.exp(s - m)
        den = den + jnp.sum(p, axis=-1, keepdims=True)
        num = num + _dot(p.astype(BF16), v)
    return num / den


def _qk(q, k):
    return lax.dot_general(q, k, (((1,), (1,)), ((), ())), preferred_element_type=F32)


def _attn_prompt_body(q_ref, k_ref, v_ref, diag_ref, o_ref, kbf_ref, vbf_ref, bias_ref):
    c = pl.program_id(1)
    shift = jnp.maximum(ATT_LEAD - c, 0)

    @pl.when(c == 0)
    def _():
        kbf_ref[...] = k_ref[...].astype(BF16)
        vbf_ref[...] = v_ref[...].astype(BF16)

    @pl.when((pl.program_id(0) == 0) & (c <= ATT_LEAD))
    def _():
        key = lax.broadcasted_iota(jnp.int32, (ATT_KEYS, ATT_Q), 0) // CHUNK
        qry = lax.broadcasted_iota(jnp.int32, (ATT_KEYS, ATT_Q), 1) // CHUNK
        dist = (LEFT_CHUNKS + qry) - (key + shift * (ATT_Q // CHUNK))
        in_band = (dist >= 0) & (dist <= LEFT_CHUNKS)
        for h in range(HEADS):
            diag = jnp.broadcast_to(diag_ref[shift, pl.ds(h, 1), :], (ATT_Q, ATT_DIAG))
            toep = pltpu.roll(diag, ATT_DIAG - ATT_Q, axis=1, stride=1, stride_axis=0)
            bias_ref[shift, h] = jnp.where(in_band, toep[:, :ATT_KEYS].T, NEG_INF)

    start = pl.multiple_of(jnp.maximum(c - ATT_LEAD, 0) * ATT_Q, ATT_Q)
    lane = lax.broadcasted_iota(jnp.int32, (ATT_Q, 128), 1)
    for pair in range(HEADS // 2):
        sl = slice(pair * 128, (pair + 1) * 128)
        kp = kbf_ref[pl.ds(start, ATT_KEYS), sl]
        vp = vbf_ref[pl.ds(start, ATT_KEYS), sl]
        qp = q_ref[:, sl] * (HEAD_DIM ** -0.5)
        halves = []
        for half in range(2):
            qm = jnp.where((lane >= HEAD_DIM) == bool(half), qp, 0.0).astype(BF16)
            s = _qk(kp, qm) + bias_ref[shift, 2 * pair + half]
            p = jnp.exp(s - jnp.max(s, axis=0, keepdims=True))
            p = (p * (1.0 / jnp.sum(p, axis=0, keepdims=True))).astype(BF16)
            halves.append(lax.dot_general(p, vp, (((0,), (0,)), ((), ())),
                                          preferred_element_type=F32))
        o_ref[:, sl] = jnp.where(lane < HEAD_DIM, halves[0], halves[1]).astype(BF16)


def _attn_prompt(proj, diag, batch, seq):
    n_c = seq // ATT_Q
    once = dict(pipeline_mode=pl.Buffered(1))
    return pl.pallas_call(
        _attn_prompt_body,
        grid=(batch, n_c),
        in_specs=[
            pl.BlockSpec((ATT_Q, WIDTH), lambda b, c: (b * n_c + c, COL_Q // WIDTH)),
            pl.BlockSpec((seq, WIDTH), lambda b, c: (b, COL_K // WIDTH), **once),
            pl.BlockSpec((seq, WIDTH), lambda b, c: (b, COL_V // WIDTH), **once),
            pl.BlockSpec(diag.shape, lambda b, c: (0, 0, 0)),
        ],
        out_specs=pl.BlockSpec((ATT_Q, WIDTH), lambda b, c: (b * n_c + c, 0)),
        out_shape=jax.ShapeDtypeStruct((batch * seq, WIDTH), BF16),
        scratch_shapes=[pltpu.VMEM((seq, WIDTH), BF16), pltpu.VMEM((seq, WIDTH), BF16),
                        pltpu.VMEM((ATT_LEAD + 1, HEADS, ATT_KEYS, ATT_Q), F32)],
        compiler_params=pltpu.CompilerParams(
            dimension_semantics=("arbitrary", "arbitrary"), vmem_limit_bytes=VMEM_LIMIT_MAX),
        name="attn_prompt",
    )(proj, proj, proj, diag)


def _attn_sample_body(q_ref, kn_ref, vn_ref, ck_ref, cv_ref, bias_c_ref, bias_n_ref, o_ref):
    t = q_ref.shape[0]
    lane = lax.broadcasted_iota(jnp.int32, (t, 128), 1)
    slabs = [slice(pair * 128, (pair + 1) * 128) for pair in range(HEADS // 2)]

    s_c, s_n = [], []
    for h in range(HEADS):
        sl = slabs[h // 2]
        qm = jnp.where((lane >= HEAD_DIM) == bool(h % 2), q_ref[:, sl] * (HEAD_DIM ** -0.5), 0.0).astype(BF16)
        s_c.append(_dot(qm, ck_ref[0, sl, :].astype(BF16)) + bias_c_ref[h])
        s_n.append(_qk(qm, kn_ref[:, sl].astype(BF16)) + bias_n_ref[h])
    s_c = jnp.concatenate(s_c, axis=0)
    s_n = jnp.concatenate(s_n, axis=0)
    m = jnp.maximum(jnp.max(s_c, axis=-1, keepdims=True), jnp.max(s_n, axis=-1, keepdims=True))
    p_c = jnp.exp(s_c - m)
    p_n = jnp.exp(s_n - m)
    inv = 1.0 / (jnp.sum(p_c, axis=-1, keepdims=True) + jnp.sum(p_n, axis=-1, keepdims=True))
    p_c = (p_c * inv).astype(BF16)
    p_n = (p_n * inv).astype(BF16)
    for pair, sl in enumerate(slabs):
        cv = cv_ref[0, sl, :].astype(BF16)
        vn = vn_ref[:, sl].astype(BF16)
        halves = []
        for h in (2 * pair, 2 * pair + 1):
            rows = slice(h * t, (h + 1) * t)
            halves.append(_qk(p_c[rows], cv) + _dot(p_n[rows], vn))
        o_ref[:, sl] = jnp.where(lane < HEAD_DIM, halves[0], halves[1]).astype(BF16)


def _attn_sample(proj, row0, cache_k, cache_v, bias_c, bias_n, batch, t):
    n_past = cache_k.shape[2]
    blk0 = row0 // t
    return pl.pallas_call(
        _attn_sample_body,
        grid=(batch,),
        in_specs=[
            pl.BlockSpec((t, WIDTH), lambda b: (blk0 + b, COL_Q // WIDTH)),
            pl.BlockSpec((t, WIDTH), lambda b: (blk0 + b, COL_K // WIDTH)),
            pl.BlockSpec((t, WIDTH), lambda b: (blk0 + b, COL_V // WIDTH)),
            pl.BlockSpec((1, WIDTH, n_past), lambda b: (b, 0, 0)),
            pl.BlockSpec((1, WIDTH, n_past), lambda b: (b, 0, 0)),
            pl.BlockSpec((HEADS, t, n_past), lambda b: (0, 0, 0)),
            pl.BlockSpec((HEADS, t, t), lambda b: (0, 0, 0)),
        ],
        out_specs=pl.BlockSpec((t, WIDTH), lambda b: (b, 0)),
        out_shape=jax.ShapeDtypeStruct((batch * t, WIDTH), BF16),
        compiler_params=pltpu.CompilerParams(
            dimension_semantics=("parallel",), vmem_limit_bytes=VMEM_LIMIT),
        name="attn_sample",
    )(proj, proj, proj, cache_k, cache_v, bias_c, bias_n)


def _prep_body(p_ref, pl_ref, bp_ref, bl_ref, mu_ref, mul_ref, w0_ref, a0_ref, kk_ref, ka_ref, rk_ref,
               lnb_ref, wup_hi, wup_lo, aup_hi, aup_lo, gup_hi, gup_lo, bd_ref,
               r_o, w_o, k_o, v_o, kk_o, kka_o, g_o, cb_o):
    def shifted(p, first_row, mu):
        row = lax.broadcasted_iota(jnp.int32, p.shape, 0)
        prev = jnp.where(row == 0, first_row, pltpu.roll(p, 1, axis=0))
        return p + (prev - p) * mu

    xm = shifted(p_ref[...], bp_ref[0], mu_ref[...])
    xl = shifted(pl_ref[...], bl_ref[0], mul_ref[...])
    r = xm[:, 0:WIDTH]
    k = xm[:, WIDTH:2 * WIDTH]
    v = xm[:, 2 * WIDTH:3 * WIDTH]
    wd = xl[:, 0:LORA_PAD]
    ad = xl[:, LORA_PAD:2 * LORA_PAD]
    gd = xl[:, 2 * LORA_PAD:]

    z = -(w0_ref[...] + _dot_f32(jnp.tanh(wd), wup_hi[...], wup_lo[...]))
    softplus = jnp.maximum(z, 0.0) + jnp.log(1.0 + jnp.exp(-jnp.abs(z)))
    decay = jnp.exp(-jnp.exp(-softplus - 0.5))
    a = _sigmoid(a0_ref[...] + _dot_f32(ad, aup_hi[...], aup_lo[...]))
    g = _dot_f32(_sigmoid(gd), gup_hi[...], gup_lo[...])

    bd = bd_ref[...]
    kk = k * kk_ref[...]
    kk = kk / jnp.maximum(jnp.sqrt(_segsum(kk * kk, bd)), 1e-12)
    k2 = k * (1.0 + (a - 1.0) * ka_ref[...])
    bonus = _segsum(r * k2 * rk_ref[...], bd) * v

    r_o[...] = r
    w_o[...] = decay
    k_o[...] = k2
    v_o[...] = v
    kk_o[...] = kk
    kka_o[...] = kk * a
    g_o[...] = g
    cb_o[...] = lnb_ref[...] + bonus


def _prep(proj, row0, n_rows, tm, first_rkv, first_lora, params):
    n_t = n_rows // tm
    blk0 = row0 // tm
    vec = lambda w: pl.BlockSpec((1, w), lambda i: (0, 0))
    mat = lambda r: pl.BlockSpec((r, WIDTH), lambda i: (0, 0))
    out = pl.BlockSpec((tm, WIDTH), lambda i: (i, 0))
    return pl.pallas_call(
        _prep_body,
        grid=(n_t,),
        in_specs=[
            pl.BlockSpec((tm, 3 * WIDTH), lambda i: (blk0 + i, COL_RKV // (3 * WIDTH))),
            pl.BlockSpec((tm, LORA_COLS), lambda i: (blk0 + i, COL_LORA // LORA_COLS)),
            pl.BlockSpec((1, 1, 3 * WIDTH), lambda i: (i, 0, 0)),
            pl.BlockSpec((1, 1, LORA_COLS), lambda i: (i, 0, 0)),
            vec(3 * WIDTH), vec(LORA_COLS), vec(WIDTH), vec(WIDTH), vec(WIDTH), vec(WIDTH), vec(WIDTH),
            vec(WIDTH), mat(LORA_PAD), mat(LORA_PAD), mat(LORA_PAD), mat(LORA_PAD),
            mat(GATE_LORA), mat(GATE_LORA),
            pl.BlockSpec((128, 128), lambda i: (0, 0)),
        ],
        out_specs=[out] * 8,
        out_shape=[jax.ShapeDtypeStruct((n_rows, WIDTH), F32)] * 8,
        compiler_params=pltpu.CompilerParams(
            dimension_semantics=("parallel",), vmem_limit_bytes=VMEM_LIMIT),
        name="rwkv_prep",
    )(proj, proj, first_rkv, first_lora, *params)


def _scan_steps(s_ref, row, get_v, put_y, n_steps):
    W, KKA, K, R, KK = range(5)
    zero = jnp.zeros((KLO, 128), F32)

    sa0 = [zero, zero]
    for k in range(HEAD_DIM):
        sa0[k % 2] = sa0[k % 2] - s_ref[k] * row(KK, 0, k)

    def step(t, sa):
        v = get_v(t)
        t_next = jnp.minimum(t + 1, n_steps - 1)
        y = [zero, zero]
        sa_next = [zero, zero]
        for k in range(HEAD_DIM):
            s_new = s_ref[k] * row(W, t, k) + sa * row(KKA, t, k) + v * row(K, t, k)
            s_ref[k] = s_new
            y[k % 2] = y[k % 2] + s_new * row(R, t, k)
            sa_next[k % 2] = sa_next[k % 2] - s_new * row(KK, t_next, k)
        put_y(t, y[0] + y[1])
        return sa_next[0] + sa_next[1]

    lax.fori_loop(0, n_steps, step, sa0[0] + sa0[1])


def _scan_body(w_ref, kka_ref, k_ref, r_ref, kk_ref, v_ref, s0_ref, y_ref, st_ref, s_ref, rows_ref, *, tb):
    @pl.when(pl.program_id(1) == 0)
    def _():
        s_ref[...] = s0_ref[0]

    lane = lax.broadcasted_iota(jnp.int32, (tb * KLO, 128), 1)
    for a, ref in enumerate((w_ref, kka_ref, k_ref, r_ref, kk_ref)):
        x = ref[0].reshape(tb * KLO, 128)
        swapped = pltpu.roll(x, HEAD_DIM, axis=1)
        rows_ref[a, :, 0:KLO, :] = jnp.where(lane < HEAD_DIM, x, swapped).reshape(tb, KLO, 128)
        rows_ref[a, :, KLO:, :] = jnp.where(lane < HEAD_DIM, swapped, x).reshape(tb, KLO, 128)

    def row(a, t, k):
        return jnp.broadcast_to(rows_ref[a, t, pl.ds(k, 1), :], (KLO, 128))

    def put_y(t, y):
        y_ref[0, t] = y

    _scan_steps(s_ref, row, lambda t: v_ref[0, t], put_y, tb)

    @pl.when(pl.program_id(1) == pl.num_programs(1) - 1)
    def _():
        st_ref[0] = s_ref[...]


def _scan_tok_body(w_hbm, kka_hbm, k_hbm, r_hbm, kk_hbm, v_hbm, s0_ref, y_ref, st_ref,
                   s_ref, rows_ref, xt_ref, vt_ref, yt_ref, in_ref, in_sem, *, tb):
    step = pl.program_id(0)
    operands = (w_hbm, kka_hbm, k_hbm, r_hbm, kk_hbm, v_hbm)

    def block_copy(a, blk):
        return pltpu.make_async_copy(operands[a].at[:, pl.ds(blk * tb, tb), :], in_ref.at[a], in_sem.at[a])

    @pl.when(step == 0)
    def _():
        for a in range(len(operands)):
            block_copy(a, 0).start()
        s_ref[...] = s0_ref[0]
        xt_ref[...] = jnp.zeros_like(xt_ref)

    def head_rows(c):
        return pl.ds(c, HEADS, stride=XT_PITCH)

    def channel_major(ref):
        for b in range(LANE_GROUP_BATCH):
            xt = ref[b].T
            for h in range(HEADS):
                xt_ref[b, h * XT_PITCH:h * XT_PITCH + HEAD_DIM, 0:tb] = xt[h * HEAD_DIM:(h + 1) * HEAD_DIM]

    def lane_tile(chan):
        parts = [xt_ref[b, head_rows(c), :] for c in chan for b in range(LANE_GROUP_BATCH)]
        return jnp.concatenate(parts, axis=0).T[0:tb]

    for a in range(5):
        block_copy(a, step).wait()
        channel_major(in_ref.at[a])
        for k in range(HEAD_DIM):
            rows_ref[a, k] = lane_tile((k, k))
    block_copy(5, step).wait()
    channel_major(in_ref.at[5])
    for j in range(KLO):
        vt_ref[pl.ds(j, tb, stride=STEP_PITCH), :] = lane_tile((j, KLO + j))

    @pl.when(step + 1 < pl.num_programs(0))
    def _():
        for a in range(len(operands)):
            block_copy(a, step + 1).start()

    def row(a, t, k):
        return jnp.broadcast_to(rows_ref[a, k, pl.ds(t, 1), :], (KLO, 128))

    def step_rows(t):
        return pl.ds(pl.multiple_of(t * STEP_PITCH, 8), KLO)

    def put_y(t, y):
        yt_ref[step_rows(t), :] = y

    _scan_steps(s_ref, row, lambda t: vt_ref[step_rows(t), :], put_y, tb)

    for j in range(KLO):
        steps = yt_ref[pl.ds(j, tb, stride=STEP_PITCH), :]
        if tb < 128:
            steps = jnp.concatenate([steps, jnp.zeros((128 - tb, 128), F32)], axis=0)
        tile = steps.T
        for half in range(2):
            for b in range(LANE_GROUP_BATCH):
                r0 = (half * LANE_GROUP_BATCH + b) * HEADS
                xt_ref[b, head_rows(half * KLO + j), :] = tile[r0:r0 + HEADS]
    for b in range(LANE_GROUP_BATCH):
        heads = [xt_ref[b, h * XT_PITCH:h * XT_PITCH + HEAD_DIM, 0:tb] for h in range(HEADS)]
        y_ref[b] = jnp.concatenate(heads, axis=0).T

    @pl.when(pl.program_id(0) == pl.num_programs(0) - 1)
    def _():
        st_ref[0] = s_ref[...]


def _scan_tok(ops, v, s0, tb):
    nb, t, _ = v.shape
    tok_spec = pl.BlockSpec((nb, tb, WIDTH), lambda j: (0, j, 0))
    st_spec = pl.BlockSpec((1, HEAD_DIM, KLO, 128), lambda j: (0, 0, 0, 0))
    return pl.pallas_call(
        functools.partial(_scan_tok_body, tb=tb),
        grid=(t // tb,),
        in_specs=[pl.BlockSpec(memory_space=pl.ANY)] * 6 + [st_spec],
        out_specs=[tok_spec, st_spec],
        out_shape=[jax.ShapeDtypeStruct((nb, t, WIDTH), F32),
                   jax.ShapeDtypeStruct((1, HEAD_DIM, KLO, 128), F32)],
        scratch_shapes=[pltpu.VMEM((HEAD_DIM, KLO, 128), F32),
                        pltpu.VMEM((5, HEAD_DIM, tb, 128), F32),
                        pltpu.VMEM((nb, HEADS * XT_PITCH, 128), F32),
                        pltpu.VMEM((tb * STEP_PITCH, 128), F32),
                        pltpu.VMEM((tb * STEP_PITCH, 128), F32),
                        pltpu.VMEM((6, nb, tb, WIDTH), F32),
                        pltpu.SemaphoreType.DMA((6,))],
        compiler_params=pltpu.CompilerParams(
            dimension_semantics=("arbitrary",), vmem_limit_bytes=VMEM_LIMIT),
        name="rwkv_scan_tok",
    )(*ops, v, s0)


def _scan(rows, v, s0, tb):
    g, t = v.shape[0], v.shape[1]
    row_spec = pl.BlockSpec((1, tb, KLO, 128), lambda i, j: (i, j, 0, 0))
    st_spec = pl.BlockSpec((1, HEAD_DIM, KLO, 128), lambda i, j: (i, 0, 0, 0))
    return pl.pallas_call(
        functools.partial(_scan_body, tb=tb),
        grid=(g, t // tb),
        in_specs=[row_spec] * 6 + [st_spec],
        out_specs=[row_spec, st_spec],
        out_shape=[jax.ShapeDtypeStruct((g, t, KLO, 128), F32),
                   jax.ShapeDtypeStruct((g, HEAD_DIM, KLO, 128), F32)],
        scratch_shapes=[pltpu.VMEM((HEAD_DIM, KLO, 128), F32),
                        pltpu.VMEM((5, tb, HEAD_DIM, 128), F32)],
        compiler_params=pltpu.CompilerParams(
            dimension_semantics=("parallel", "arbitrary"), vmem_limit_bytes=VMEM_LIMIT),
        name="rwkv_scan",
    )(*rows, v, s0)


def _rows_to_scan(x, n_batch, t):
    g = n_batch // LANE_GROUP_BATCH
    x = x.reshape(g, LANE_GROUP_BATCH, t, HEADS, 2, KLO)
    return x.transpose(0, 2, 5, 4, 1, 3).reshape(g, t, KLO, 128)


def _rows_from_scan(y, n_batch, t):
    g = n_batch // LANE_GROUP_BATCH
    y = y.reshape(g, t, KLO, 2, LANE_GROUP_BATCH, HEADS).transpose(0, 4, 1, 5, 3, 2)
    return y.reshape(n_batch * t, WIDTH)


def _state_to_scan(s, n_batch):
    g = n_batch // LANE_GROUP_BATCH
    s = s.reshape(g, LANE_GROUP_BATCH, HEADS, 2, KLO, HEAD_DIM)
    return s.transpose(0, 5, 4, 3, 1, 2).reshape(g, HEAD_DIM, KLO, 128)


def _state_from_scan(s, n_batch):
    g = n_batch // LANE_GROUP_BATCH
    s = s.reshape(g, HEAD_DIM, KLO, 2, LANE_GROUP_BATCH, HEADS)
    return s.transpose(0, 4, 5, 3, 2, 1).reshape(n_batch, HEADS, HEAD_DIM, HEAD_DIM)


def _merge_body(att_ref, y_ref, g_ref, cb_ref, ga_ref, gb_ref, x_ref, lnw_ref, gpost_ref, bd_ref,
                wa_ref, wr_ref, wo_ref, o_ref):
    bd = bd_ref[...]
    y = y_ref[...]
    d = y - _segsum(y, bd) * (1.0 / HEAD_DIM)
    var = _segsum(d * d, bd) * (1.0 / HEAD_DIM)
    rw = ((d * lax.rsqrt(var + GN_EPS) * lnw_ref[...] + cb_ref[...]) * g_ref[...]).astype(BF16)
    rw_o = _dot(rw, wr_ref[...])
    att_o = _dot(att_ref[...], wa_ref[...])
    mixed = (_sigmoid(ga_ref[...]) * att_o + _sigmoid(gb_ref[...]) * rw_o).astype(BF16)
    z = _dot(mixed, wo_ref[...])
    o_ref[...] = x_ref[...] + z * _rms_scale(z) * gpost_ref[...]


def _merge(att, y, g, cb, proj, x, ln_w, g_post, ones_bd, w_att, w_rwkv, w_out):
    m = x.shape[0]
    tm = min(MERGE_TM, m)
    tok = lambda w: pl.BlockSpec((tm, w), lambda i: (i, 0))
    const = lambda a: pl.BlockSpec(a.shape, lambda i: (0,) * a.ndim)
    return pl.pallas_call(
        _merge_body,
        grid=(m // tm,),
        in_specs=[
            tok(WIDTH), tok(WIDTH), tok(WIDTH), tok(WIDTH),
            pl.BlockSpec((tm, D_MODEL), lambda i: (i, COL_GA // D_MODEL)),
            pl.BlockSpec((tm, D_MODEL), lambda i: (i, COL_GB // D_MODEL)),
            tok(D_MODEL), const(ln_w), const(g_post), const(ones_bd),
            const(w_att), const(w_rwkv), const(w_out),
        ],
        out_specs=tok(D_MODEL),
        out_shape=jax.ShapeDtypeStruct((m, D_MODEL), F32),
        compiler_params=pltpu.CompilerParams(
            dimension_semantics=("parallel",), vmem_limit_bytes=VMEM_LIMIT),
        name="merge",
    )(att, y, g, cb, proj, proj, x, ln_w, g_post, ones_bd, w_att, w_rwkv, w_out)


def _pad_cols(x, width):
    return jnp.pad(x, ((0, 0), (0, width - x.shape[1])))


def _pad_rows(x, height):
    return jnp.pad(x, ((0, height - x.shape[0]), (0, 0)))


def _reorder_shift_cols(x):
    o = 3 * WIDTH
    wd = x[..., o:o + DECAY_LORA]
    ad = x[..., o + DECAY_LORA:o + DECAY_LORA + ICLR_LORA]
    gd = x[..., o + DECAY_LORA + ICLR_LORA:]
    pad = [(0, 0)] * (x.ndim - 1)
    lora = jnp.concatenate([jnp.pad(wd, pad + [(0, LORA_PAD - DECAY_LORA)]),
                            jnp.pad(ad, pad + [(0, LORA_PAD - ICLR_LORA)]), gd], axis=-1)
    return x[..., :o], lora


def _shift_row_from_proj(rows):
    lo = rows[:, COL_LORA:]
    return jnp.concatenate([rows[:, COL_RKV:COL_RKV + 3 * WIDTH], lo[:, :DECAY_LORA],
                            lo[:, LORA_PAD:LORA_PAD + ICLR_LORA], lo[:, 2 * LORA_PAD:]], axis=-1)


def _bias_diagonals(rel_bias):
    u = np.arange(ATT_DIAG)[None, :]
    s = np.arange(ATT_LEAD + 1)[:, None]
    idx = np.clip(ATT_KEYS - u - ATT_Q * s, -MAX_REL, MAX_REL) + MAX_REL
    tab = jnp.take(rel_bias, jnp.asarray(idx.reshape(-1), jnp.int32), axis=1)
    return tab.reshape(HEADS, ATT_LEAD + 1, ATT_DIAG).transpose(1, 0, 2)


def _bias_sample(rel_bias, t, n_past):
    rel = np.arange(t)[:, None] + n_past - np.arange(n_past + t)[None, :]
    idx = np.clip(rel, -MAX_REL, MAX_REL) + MAX_REL
    tab = jnp.take(rel_bias, jnp.asarray(idx.reshape(-1), jnp.int32), axis=1).reshape(HEADS, t, n_past + t)
    return tab[:, :, :n_past], tab[:, :, n_past:]


def _layer(x, n_batch, t, attend, first_rows, prep_tm, wkv0, scan_tb, wts):
    m = n_batch * t
    x = _ffn(x, *wts["ffn1"])
    proj = _proj(x, *wts["proj"])
    att = attend(proj)

    prep = _prep(proj, 0, m, prep_tm, *first_rows(proj), wts["prep"])
    r, wdec, k2, v, kk, kka, g, cb = prep
    ops = (wdec, kka, k2, r, kk)
    if n_batch == LANE_GROUP_BATCH and scan_tb % 64 == 0:
        tok = lambda a: a.reshape(n_batch, t, WIDTH)
        y, s_t = _scan_tok([tok(a) for a in ops], tok(v), _state_to_scan(wkv0, n_batch), scan_tb)
        y = y.reshape(m, WIDTH)
    else:
        rows = [_rows_to_scan(a, n_batch, t) for a in ops]
        y, s_t = _scan(rows, _rows_to_scan(v, n_batch, t), _state_to_scan(wkv0, n_batch), scan_tb)
        y = _rows_from_scan(y, n_batch, t)

    x = _merge(att, y, g, cb, proj, x, *wts["merge"])
    x = _ffn(x, *wts["ffn2"])

    last = jnp.concatenate([proj[(b + 1) * t - 1:(b + 1) * t] for b in range(n_batch)])
    return x, proj, _state_from_scan(s_t, n_batch), _shift_row_from_proj(last)[:, None]


def kernel(x_prompt, x_sample, cache_k, cache_v, state_wkv, state_shift, norm_ffn1_pre, norm_ffn1_post, w_ffn1_in, w_ffn1_down, norm_mix_pre, norm_mix_post, w_in, rel_bias, w_att_out, rwkv_mu, rwkv_w0, rwkv_w_up, rwkv_a0, rwkv_a_up, rwkv_g_up, rwkv_k_k, rwkv_k_a, rwkv_r_k, rwkv_ln_w, rwkv_ln_b, w_rwkv_out, w_out, norm_ffn2_pre, norm_ffn2_post, w_ffn2_in, w_ffn2_down):
    depth = w_in.shape[0]
    batch, seq, _ = x_prompt.shape
    dec_batch, dec_seq, _ = x_sample.shape
    n_past = cache_k.shape[2]
    m_p, m_s = batch * seq, dec_batch * dec_seq
    keep = min(PAST_BAND, seq)
    assert n_past == PAST_BAND and seq % ATT_Q == 0 and seq // ATT_Q > ATT_LEAD
    assert m_p % FFN_TM == 0 and m_p % PROJ_TM == 0 and m_p % MERGE_TM == 0 and seq % PREP_TM == 0
    assert seq % SCAN_TB == 0 and batch == LANE_GROUP_BATCH and dec_batch % LANE_GROUP_BATCH == 0

    x_p = x_prompt.reshape(m_p, D_MODEL)
    x_s = x_sample.reshape(m_s, D_MODEL)
    ones_bd = jnp.asarray(np.kron(np.eye(2), np.ones((HEAD_DIM, HEAD_DIM))), BF16)
    row = lambda v: v.reshape(1, -1)
    heads = lambda a, n, t: a.reshape(n, t, HEADS, HEAD_DIM)

    outs = {k: [] for k in ("kp", "vp", "wp", "sp", "ks", "vs", "ws", "ss")}
    for l in range(depth):
        w_proj = _repack_proj_weight(w_in[l].T)
        mu_rkv, mu_lora = _reorder_shift_cols(rwkv_mu[l][None])
        wts = {
            "ffn1": (row(norm_ffn1_pre[l]), row(norm_ffn1_post[l]),
                     w_ffn1_in[l].astype(BF16), w_ffn1_down[l].astype(BF16)),
            "proj": (row(norm_mix_pre[l]), w_proj),
            "prep": (mu_rkv, mu_lora, row(rwkv_w0[l]), row(rwkv_a0[l]), row(rwkv_k_k[l]), row(rwkv_k_a[l]),
                     row(rwkv_r_k[l]), row(rwkv_ln_b[l]),
                     *_split_bf16(_pad_rows(rwkv_w_up[l], LORA_PAD)),
                     *_split_bf16(_pad_rows(rwkv_a_up[l], LORA_PAD)),
                     *_split_bf16(rwkv_g_up[l]), ones_bd),
            "merge": (row(rwkv_ln_w[l]), row(norm_mix_post[l]), ones_bd,
                      w_att_out[l].astype(BF16), w_rwkv_out[l].astype(BF16), w_out[l].astype(BF16)),
            "ffn2": (row(norm_ffn2_pre[l]), row(norm_ffn2_post[l]),
                     w_ffn2_in[l].astype(BF16), w_ffn2_down[l].astype(BF16)),
        }

        diag = _bias_diagonals(rel_bias[l])
        starts_seq = (jnp.arange(m_p // PREP_TM) % (seq // PREP_TM) == 0)[:, None]

        def first_rows_prompt(proj):
            last_of_tile = proj.reshape(m_p // PREP_TM, PREP_TM, PROJ_COLS)[:, PREP_TM - 1]
            first = jnp.where(starts_seq, 0.0, jnp.roll(last_of_tile, 1, axis=0))[:, None]
            return first[..., COL_RKV:COL_RKV + 3 * WIDTH], first[..., COL_LORA:]

        x_p, proj_p, wkv_p, sh_p = _layer(
            x_p, batch, seq, lambda proj: _attn_prompt(proj, diag, batch, seq), first_rows_prompt, PREP_TM,
            jnp.zeros((batch, HEADS, HEAD_DIM, HEAD_DIM), F32), SCAN_TB, wts)
        tail = lambda col: jnp.stack([proj_p[(b + 1) * seq - keep:(b + 1) * seq, col:col + WIDTH]
                                      for b in range(batch)])
        outs["kp"].append(heads(tail(COL_K), batch, keep))
        outs["vp"].append(heads(tail(COL_V), batch, keep))
        outs["wp"].append(wkv_p.astype(state_wkv.dtype))
        outs["sp"].append(sh_p)

        chan_major = lambda c: c.transpose(0, 2, 3, 1).reshape(dec_batch, WIDTH, n_past)
        ck, cv = chan_major(cache_k[l]), chan_major(cache_v[l])
        bias_c, bias_n = _bias_sample(rel_bias[l], dec_seq, n_past)
        x_s, proj_s, wkv_s, sh_s = _layer(
            x_s, dec_batch, dec_seq,
            lambda proj: _attn_sample(proj, 0, ck, cv, bias_c, bias_n, dec_batch, dec_seq),
            lambda proj: _reorder_shift_cols(state_shift[l]), dec_seq,
            state_wkv[l].astype(F32), dec_seq, wts)
        outs["ks"].append(heads(proj_s[:, COL_K:COL_K + WIDTH], dec_batch, dec_seq))
        outs["vs"].append(heads(proj_s[:, COL_V:COL_V + WIDTH], dec_batch, dec_seq))
        outs["ws"].append(wkv_s.astype(state_wkv.dtype))
        outs["ss"].append(sh_s)

    st = lambda k: jnp.stack(outs[k])
    return (x_p.reshape(batch, seq, D_MODEL), x_s.reshape(dec_batch, dec_seq, D_MODEL),
            st("kp"), st("vp"), st("wp"), st("sp"), st("ks"), st("vs"), st("ws"), st("ss"))
```

```python
import functools

import numpy as np
import jax
import jax.numpy as jnp
from jax import lax
from jax.experimental import pallas as pl
from jax.experimental.pallas import tpu as pltpu

F32 = jnp.float32
BF16 = jnp.bfloat16

D_MODEL = 2048
CHUNK = 64
LEFT_CHUNKS = 8
PAST_BAND = LEFT_CHUNKS * CHUNK
BAND = PAST_BAND + CHUNK
HEAD_DIM = 64
HEADS = 16
WIDTH = HEADS * HEAD_DIM
DECAY_LORA = 96
ICLR_LORA = 96
GATE_LORA = 256
LORA_PAD = 128
LORA_COLS = 2 * LORA_PAD + GATE_LORA
D_FF = 5632
MAX_REL = 128
RMS_EPS = 1e-6
GN_EPS = 64e-5
NEG_INF = -1e30

COL_Q, COL_K, COL_V = 0, WIDTH, 2 * WIDTH
COL_RKV = 3 * WIDTH
COL_GA = 6 * WIDTH
COL_GB = COL_GA + D_MODEL
COL_LORA = COL_GB + D_MODEL
PROJ_COLS = COL_LORA + LORA_COLS

VMEM_LIMIT = 56 * 1024 * 1024
VMEM_LIMIT_MAX = 60 * 1024 * 1024

FFN_TM = 512
FFN_TF = 512
PROJ_TM = 1024
PROJ_TN = 1536
REPACK_ROWS = 768
SEG_HEADS = 4
MERGE_TM = 256
PREP_TM = 256
SCAN_TB = 128
ATT_Q = 2 * CHUNK
ATT_KEYS = PAST_BAND + ATT_Q
ATT_LEAD = PAST_BAND // ATT_Q
ATT_DIAG = ATT_KEYS + ATT_Q
LANE_GROUP_BATCH = 4
KLO = HEAD_DIM // 2
STEP_PITCH = KLO + 8
XT_PITCH = HEAD_DIM + 8


def _sigmoid(x):
    return 1.0 / (1.0 + jnp.exp(-x))


def _split_bf16(x):
    hi = x.astype(BF16)
    lo = (x - hi.astype(F32)).astype(BF16)
    return hi, lo


def _dot(a, b):
    return jnp.dot(a, b, preferred_element_type=F32)


def _dot_f32(a, b_hi, b_lo):
    a_hi, a_lo = _split_bf16(a)
    return _dot(a_hi, b_hi) + _dot(a_lo, b_hi) + _dot(a_hi, b_lo)


def _segsum(x, ones_bd):
    hi, lo = _split_bf16(x)
    width = ones_bd.shape[0]
    outs = []
    for j in range(x.shape[-1] // width):
        sl = slice(j * width, (j + 1) * width)
        outs.append(_dot(hi[:, sl], ones_bd) + _dot(lo[:, sl], ones_bd))
    return jnp.concatenate(outs, axis=-1)


def _rms_scale(x):
    return lax.rsqrt(jnp.mean(x * x, axis=-1, keepdims=True) + RMS_EPS)


def _ffn_body(x_ref, gpre_ref, gpost_ref, wg_ref, wu_ref, wd_ref, o_ref, h_ref, *, n_f):
    f = pl.program_id(1)

    @pl.when(f == 0)
    def _():
        x = x_ref[...]
        h_ref[...] = (x * _rms_scale(x) * gpre_ref[...]).astype(BF16)
        o_ref[...] = jnp.zeros_like(o_ref)

    h = h_ref[...]
    g = _dot(h, wg_ref[...])
    u = _dot(h, wu_ref[...])
    act = (g * _sigmoid(g) * u).astype(BF16)
    o_ref[...] += _dot(act, wd_ref[...])

    @pl.when(f == n_f - 1)
    def _():
        y = o_ref[...]
        o_ref[...] = x_ref[...] + 0.5 * (y * _rms_scale(y) * gpost_ref[...])


def _ffn(x, g_pre, g_post, w_in, w_down):
    m = x.shape[0]
    tm = min(FFN_TM, m)
    n_m, n_f = m // tm, D_FF // FFN_TF
    return pl.pallas_call(
        functools.partial(_ffn_body, n_f=n_f),
        grid=(n_m, n_f),
        in_specs=[
            pl.BlockSpec((tm, D_MODEL), lambda i, f: (i, 0)),
            pl.BlockSpec((1, D_MODEL), lambda i, f: (0, 0)),
            pl.BlockSpec((1, D_MODEL), lambda i, f: (0, 0)),
            pl.BlockSpec((D_MODEL, FFN_TF), lambda i, f: (0, f)),
            pl.BlockSpec((D_MODEL, FFN_TF), lambda i, f: (0, f + n_f)),
            pl.BlockSpec((FFN_TF, D_MODEL), lambda i, f: (f, 0)),
        ],
        out_specs=pl.BlockSpec((tm, D_MODEL), lambda i, f: (i, 0)),
        out_shape=jax.ShapeDtypeStruct((m, D_MODEL), F32),
        scratch_shapes=[pltpu.VMEM((tm, D_MODEL), BF16)],
        compiler_params=pltpu.CompilerParams(
            dimension_semantics=("parallel", "arbitrary"), vmem_limit_bytes=VMEM_LIMIT),
        name="ffn",
    )(x, g_pre, g_post, w_in, w_in, w_down)


def _repack_body(wt_hbm, o_ref, buf_ref, sem, *, n_head):
    j = pl.program_id(0)
    last = pl.num_programs(0) - 1
    o_ga = COL_GA + DECAY_LORA + ICLR_LORA + GATE_LORA

    def row_copy(src_row, n_rows, dst_row, slot):
        return pltpu.make_async_copy(wt_hbm.at[pl.ds(src_row, n_rows), :],
                                     buf_ref.at[pl.ds(dst_row, n_rows), :], sem.at[slot])

    @pl.when(j < last)
    def _():
        src = jnp.where(j < n_head, j * REPACK_ROWS, o_ga + (j - n_head) * REPACK_ROWS)
        copy = row_copy(src, REPACK_ROWS, 0, 0)
        copy.start()
        copy.wait()

    @pl.when(j == last)
    def _():
        n_gate = COL_LORA - (PROJ_COLS - REPACK_ROWS)
        pieces = [(o_ga + COL_LORA - COL_GA - n_gate, n_gate, 0),
                  (COL_GA, DECAY_LORA, n_gate),
                  (COL_GA + DECAY_LORA, ICLR_LORA, n_gate + LORA_PAD),
                  (COL_GA + DECAY_LORA + ICLR_LORA, GATE_LORA, n_gate + 2 * LORA_PAD)]
        for dst, width in ((n_gate, DECAY_LORA), (n_gate + LORA_PAD, ICLR_LORA)):
            buf_ref[dst + width:dst + LORA_PAD, :] = jnp.zeros((LORA_PAD - width, D_MODEL), F32)
        copies = [row_copy(src, n, dst, slot) for slot, (src, n, dst) in enumerate(pieces)]
        for copy in copies:
            copy.start()
        for copy in copies:
            copy.wait()

    o_ref[...] = buf_ref[...].astype(BF16)


def _repack_proj_weight(wt):
    assert COL_GA % REPACK_ROWS == 0 and PROJ_COLS % REPACK_ROWS == 0
    assert PROJ_COLS - REPACK_ROWS <= COL_LORA <= PROJ_COLS - LORA_COLS
    return pl.pallas_call(
        functools.partial(_repack_body, n_head=COL_GA // REPACK_ROWS),
        grid=(PROJ_COLS // REPACK_ROWS,),
        in_specs=[pl.BlockSpec(memory_space=pl.ANY)],
        out_specs=pl.BlockSpec((REPACK_ROWS, D_MODEL), lambda j: (j, 0)),
        out_shape=jax.ShapeDtypeStruct((PROJ_COLS, D_MODEL), BF16),
        scratch_shapes=[pltpu.VMEM((REPACK_ROWS, D_MODEL), F32), pltpu.SemaphoreType.DMA((4,))],
        compiler_params=pltpu.CompilerParams(
            dimension_semantics=("arbitrary",), vmem_limit_bytes=VMEM_LIMIT),
        name="repack_proj_weight",
    )(wt)


def _proj_body(x_ref, g_ref, w_ref, o_ref, h_ref):
    @pl.when(pl.program_id(1) == 0)
    def _():
        x = x_ref[...]
        h_ref[...] = (x * _rms_scale(x) * g_ref[...]).astype(BF16)

    o_ref[...] = _qk(h_ref[...], w_ref[...])


def _proj(x, g, w):
    m = x.shape[0]
    tm = min(PROJ_TM, m)
    return pl.pallas_call(
        _proj_body,
        grid=(m // tm, PROJ_COLS // PROJ_TN),
        in_specs=[
            pl.BlockSpec((tm, D_MODEL), lambda i, n: (i, 0)),
            pl.BlockSpec((1, D_MODEL), lambda i, n: (0, 0)),
            pl.BlockSpec((PROJ_TN, D_MODEL), lambda i, n: (n, 0)),
        ],
        out_specs=pl.BlockSpec((tm, PROJ_TN), lambda i, n: (i, n)),
        out_shape=jax.ShapeDtypeStruct((m, PROJ_COLS), F32),
        scratch_shapes=[pltpu.VMEM((tm, D_MODEL), BF16)],
        compiler_params=pltpu.CompilerParams(
            dimension_semantics=("parallel", "arbitrary"), vmem_limit_bytes=VMEM_LIMIT),
        name="proj",
    )(x, g, w)


def _softmax_pv(parts):
    m = functools.reduce(jnp.maximum, [jnp.max(s, axis=-1, keepdims=True) for s, _ in parts])
    num, den = 0.0, 0.0
    for s, v in parts:
        p = jnp.exp(s - m)
        den = den + jnp.sum(p, axis=-1, keepdims=True)
        num = num + _dot(p.astype(BF16), v)
    return num / den


def _qk(q, k):
    return lax.dot_general(q, k, (((1,), (1,)), ((), ())), preferred_element_type=F32)


def _attn_prompt_body(q_ref, k_ref, v_ref, diag_ref, o_ref, kbf_ref, vbf_ref, bias_ref):
    c = pl.program_id(1)
    shift = jnp.maximum(ATT_LEAD - c, 0)

    @pl.when(c == 0)
    def _():
        kbf_ref[...] = k_ref[...].astype(BF16)
        vbf_ref[...] = v_ref[...].T.astype(BF16)

    @pl.when((pl.program_id(0) == 0) & (c <= ATT_LEAD))
    def _():
        key = lax.broadcasted_iota(jnp.int32, (ATT_KEYS, ATT_Q), 0) // CHUNK
        qry = lax.broadcasted_iota(jnp.int32, (ATT_KEYS, ATT_Q), 1) // CHUNK
        dist = (LEFT_CHUNKS + qry) - (key + shift * (ATT_Q // CHUNK))
        in_band = (dist >= 0) & (dist <= LEFT_CHUNKS)
        for h in range(HEADS):
            diag = jnp.broadcast_to(diag_ref[shift, pl.ds(h, 1), :], (ATT_Q, ATT_DIAG))
            toep = pltpu.roll(diag, ATT_DIAG - ATT_Q, axis=1, stride=1, stride_axis=0)
            bias_ref[shift, h] = jnp.where(in_band, toep[:, :ATT_KEYS].T, NEG_INF)

    start = pl.multiple_of(jnp.maximum(c - ATT_LEAD, 0) * ATT_Q, ATT_Q)
    lane = lax.broadcasted_iota(jnp.int32, (ATT_Q, 128), 1)
    for pair in range(HEADS // 2):
        sl = slice(pair * 128, (pair + 1) * 128)
        kp = kbf_ref[pl.ds(start, ATT_KEYS), sl]
        vp = vbf_ref[sl, pl.ds(start, ATT_KEYS)]
        qp = q_ref[:, sl] * (HEAD_DIM ** -0.5)
        halves = []
        for half in range(2):
            qm = jnp.where((lane >= HEAD_DIM) == bool(half), qp, 0.0).astype(BF16)
            s = _qk(kp, qm) + bias_ref[shift, 2 * pair + half]
            p = jnp.exp(s - jnp.max(s, axis=0, keepdims=True))
            p = (p * (1.0 / jnp.sum(p, axis=0, keepdims=True))).astype(BF16)
            halves.append(_dot(vp, p))
        chan = lax.broadcasted_iota(jnp.int32, (128, ATT_Q), 0)
        o_ref[:, sl] = jnp.where(chan < HEAD_DIM, halves[0], halves[1]).T.astype(BF16)


def _attn_prompt(proj, diag, batch, seq):
    n_c = seq // ATT_Q
    once = dict(pipeline_mode=pl.Buffered(1))
    return pl.pallas_call(
        _attn_prompt_body,
        grid=(batch, n_c),
        in_specs=[
            pl.BlockSpec((ATT_Q, WIDTH), lambda b, c: (b * n_c + c, COL_Q // WIDTH)),
            pl.BlockSpec((seq, WIDTH), lambda b, c: (b, COL_K // WIDTH), **once),
            pl.BlockSpec((seq, WIDTH), lambda b, c: (b, COL_V // WIDTH), **once),
            pl.BlockSpec(diag.shape, lambda b, c: (0, 0, 0)),
        ],
        out_specs=pl.BlockSpec((ATT_Q, WIDTH), lambda b, c: (b * n_c + c, 0)),
        out_shape=jax.ShapeDtypeStruct((batch * seq, WIDTH), BF16),
        scratch_shapes=[pltpu.VMEM((seq, WIDTH), BF16), pltpu.VMEM((WIDTH, seq), BF16),
                        pltpu.VMEM((ATT_LEAD + 1, HEADS, ATT_KEYS, ATT_Q), F32)],
        compiler_params=pltpu.CompilerParams(
            dimension_semantics=("arbitrary", "arbitrary"), vmem_limit_bytes=VMEM_LIMIT_MAX),
        name="attn_prompt",
    )(proj, proj, proj, diag)


def _attn_sample_body(q_ref, kn_ref, vn_ref, ck_ref, cv_ref, bias_c_ref, bias_n_ref, o_ref):
    t = q_ref.shape[0]
    lane = lax.broadcasted_iota(jnp.int32, (t, 128), 1)
    slabs = [slice(pair * 128, (pair + 1) * 128) for pair in range(HEADS // 2)]

    s_c, s_n = [], []
    for h in range(HEADS):
        sl = slabs[h // 2]
        qm = jnp.where((lane >= HEAD_DIM) == bool(h % 2), q_ref[:, sl] * (HEAD_DIM ** -0.5), 0.0).astype(BF16)
        s_c.append(_dot(qm, ck_ref[0, sl, :].astype(BF16)) + bias_c_ref[h])
        s_n.append(_qk(qm, kn_ref[:, sl].astype(BF16)) + bias_n_ref[h])
    s_c = jnp.concatenate(s_c, axis=0)
    s_n = jnp.concatenate(s_n, axis=0)
    m = jnp.maximum(jnp.max(s_c, axis=-1, keepdims=True), jnp.max(s_n, axis=-1, keepdims=True))
    p_c = jnp.exp(s_c - m)
    p_n = jnp.exp(s_n - m)
    inv = 1.0 / (jnp.sum(p_c, axis=-1, keepdims=True) + jnp.sum(p_n, axis=-1, keepdims=True))
    p_c = (p_c * inv).astype(BF16)
    p_n = (p_n * inv).astype(BF16)
    for pair, sl in enumerate(slabs):
        cv = cv_ref[0, sl, :].astype(BF16)
        vn = vn_ref[:, sl].astype(BF16)
        halves = []
        for h in (2 * pair, 2 * pair + 1):
            rows = slice(h * t, (h + 1) * t)
            halves.append(_qk(p_c[rows], cv) + _dot(p_n[rows], vn))
        o_ref[:, sl] = jnp.where(lane < HEAD_DIM, halves[0], halves[1]).astype(BF16)


def _attn_sample(proj, row0, cache_k, cache_v, bias_c, bias_n, batch, t):
    n_past = cache_k.shape[2]
    blk0 = row0 // t
    return pl.pallas_call(
        _attn_sample_body,
        grid=(batch,),
        in_specs=[
            pl.BlockSpec((t, WIDTH), lambda b: (blk0 + b, COL_Q // WIDTH)),
            pl.BlockSpec((t, WIDTH), lambda b: (blk0 + b, COL_K // WIDTH)),
            pl.BlockSpec((t, WIDTH), lambda b: (blk0 + b, COL_V // WIDTH)),
            pl.BlockSpec((1, WIDTH, n_past), lambda b: (b, 0, 0)),
            pl.BlockSpec((1, WIDTH, n_past), lambda b: (b, 0, 0)),
            pl.BlockSpec((HEADS, t, n_past), lambda b: (0, 0, 0)),
            pl.BlockSpec((HEADS, t, t), lambda b: (0, 0, 0)),
        ],
        out_specs=pl.BlockSpec((t, WIDTH), lambda b: (b, 0)),
        out_shape=jax.ShapeDtypeStruct((batch * t, WIDTH), BF16),
        compiler_params=pltpu.CompilerParams(
            dimension_semantics=("parallel",), vmem_limit_bytes=VMEM_LIMIT),
        name="attn_sample",
    )(proj, proj, proj, cache_k, cache_v, bias_c, bias_n)


def _prep_body(p_ref, pl_ref, bp_ref, bl_ref, mu_ref, mul_ref, w0_ref, a0_ref, kk_ref, ka_ref, rk_ref,
               lnb_ref, wup_hi, wup_lo, aup_hi, aup_lo, gup_hi, gup_lo, bd_ref,
               r_o, w_o, k_o, v_o, kk_o, kka_o, g_o, cb_o):
    def shifted(p, first_row, mu):
        row = lax.broadcasted_iota(jnp.int32, p.shape, 0)
        prev = jnp.where(row == 0, first_row, pltpu.roll(p, 1, axis=0))
        return p + (prev - p) * mu

    xm = shifted(p_ref[...], bp_ref[0], mu_ref[...])
    xl = shifted(pl_ref[...], bl_ref[0], mul_ref[...])
    r = xm[:, 0:WIDTH]
    k = xm[:, WIDTH:2 * WIDTH]
    v = xm[:, 2 * WIDTH:3 * WIDTH]
    wd = xl[:, 0:LORA_PAD]
    ad = xl[:, LORA_PAD:2 * LORA_PAD]
    gd = xl[:, 2 * LORA_PAD:]

    z = -(w0_ref[...] + _dot_f32(jnp.tanh(wd), wup_hi[...], wup_lo[...]))
    softplus = jnp.maximum(z, 0.0) + jnp.log(1.0 + jnp.exp(-jnp.abs(z)))
    decay = jnp.exp(-jnp.exp(-softplus - 0.5))
    a = _sigmoid(a0_ref[...] + _dot_f32(ad, aup_hi[...], aup_lo[...]))
    g = _dot_f32(_sigmoid(gd), gup_hi[...], gup_lo[...])

    bd = bd_ref[...]
    kk = k * kk_ref[...]
    kk = kk / jnp.maximum(jnp.sqrt(_segsum(kk * kk, bd)), 1e-12)
    k2 = k * (1.0 + (a - 1.0) * ka_ref[...])
    bonus = _segsum(r * k2 * rk_ref[...], bd) * v

    r_o[...] = r
    w_o[...] = decay
    k_o[...] = k2
    v_o[...] = v
    kk_o[...] = kk
    kka_o[...] = kk * a
    g_o[...] = g
    cb_o[...] = lnb_ref[...] + bonus


def _prep(proj, row0, n_rows, tm, first_rkv, first_lora, params):
    n_t = n_rows // tm
    blk0 = row0 // tm
    vec = lambda w: pl.BlockSpec((1, w), lambda i: (0, 0))
    mat = lambda r: pl.BlockSpec((r, WIDTH), lambda i: (0, 0))
    out = pl.BlockSpec((tm, WIDTH), lambda i: (i, 0))
    return pl.pallas_call(
        _prep_body,
        grid=(n_t,),
        in_specs=[
            pl.BlockSpec((tm, 3 * WIDTH), lambda i: (blk0 + i, COL_RKV // (3 * WIDTH))),
            pl.BlockSpec((tm, LORA_COLS), lambda i: (blk0 + i, COL_LORA // LORA_COLS)),
            pl.BlockSpec((1, 1, 3 * WIDTH), lambda i: (i, 0, 0)),
            pl.BlockSpec((1, 1, LORA_COLS), lambda i: (i, 0, 0)),
            vec(3 * WIDTH), vec(LORA_COLS), vec(WIDTH), vec(WIDTH), vec(WIDTH), vec(WIDTH), vec(WIDTH),
            vec(WIDTH), mat(LORA_PAD), mat(LORA_PAD), mat(LORA_PAD), mat(LORA_PAD),
            mat(GATE_LORA), mat(GATE_LORA),
            pl.BlockSpec((SEG_HEADS * HEAD_DIM, SEG_HEADS * HEAD_DIM), lambda i: (0, 0)),
        ],
        out_specs=[out] * 8,
        out_shape=[jax.ShapeDtypeStruct((n_rows, WIDTH), F32)] * 8,
        compiler_params=pltpu.CompilerParams(
            dimension_semantics=("parallel",), vmem_limit_bytes=VMEM_LIMIT),
        name="rwkv_prep",
    )(proj, proj, first_rkv, first_lora, *params)


def _scan_steps(s_ref, row, get_v, put_y, n_steps):
    W, KKA, K, R, KK = range(5)
    zero = jnp.zeros((KLO, 128), F32)

    sa0 = [zero, zero]
    for k in range(HEAD_DIM):
        sa0[k % 2] = sa0[k % 2] - s_ref[k] * row(KK, 0, k)

    def step(t, sa):
        v = get_v(t)
        t_next = jnp.minimum(t + 1, n_steps - 1)
        y = [zero, zero]
        sa_next = [zero, zero]
        for k in range(HEAD_DIM):
            s_new = s_ref[k] * row(W, t, k) + sa * row(KKA, t, k) + v * row(K, t, k)
            s_ref[k] = s_new
            y[k % 2] = y[k % 2] + s_new * row(R, t, k)
            sa_next[k % 2] = sa_next[k % 2] - s_new * row(KK, t_next, k)
        put_y(t, y[0] + y[1])
        return sa_next[0] + sa_next[1]

    lax.fori_loop(0, n_steps, step, sa0[0] + sa0[1])


def _scan_body(w_ref, kka_ref, k_ref, r_ref, kk_ref, v_ref, s0_ref, y_ref, st_ref, s_ref, rows_ref, *, tb):
    @pl.when(pl.program_id(1) == 0)
    def _():
        s_ref[...] = s0_ref[0]

    lane = lax.broadcasted_iota(jnp.int32, (tb * KLO, 128), 1)
    for a, ref in enumerate((w_ref, kka_ref, k_ref, r_ref, kk_ref)):
        x = ref[0].reshape(tb * KLO, 128)
        swapped = pltpu.roll(x, HEAD_DIM, axis=1)
        rows_ref[a, :, 0:KLO, :] = jnp.where(lane < HEAD_DIM, x, swapped).reshape(tb, KLO, 128)
        rows_ref[a, :, KLO:, :] = jnp.where(lane < HEAD_DIM, swapped, x).reshape(tb, KLO, 128)

    def row(a, t, k):
        return jnp.broadcast_to(rows_ref[a, t, pl.ds(k, 1), :], (KLO, 128))

    def put_y(t, y):
        y_ref[0, t] = y

    _scan_steps(s_ref, row, lambda t: v_ref[0, t], put_y, tb)

    @pl.when(pl.program_id(1) == pl.num_programs(1) - 1)
    def _():
        st_ref[0] = s_ref[...]


def _scan_tok_body(w_hbm, kka_hbm, k_hbm, r_hbm, kk_hbm, v_hbm, s0_ref, y_ref, st_ref,
                   s_ref, rows_ref, xt_ref, vt_ref, yt_ref, in_ref, in_sem, *, tb):
    step = pl.program_id(0)
    operands = (w_hbm, kka_hbm, k_hbm, r_hbm, kk_hbm, v_hbm)

    def block_copy(a, blk):
        return pltpu.make_async_copy(operands[a].at[:, pl.ds(blk * tb, tb), :], in_ref.at[a], in_sem.at[a])

    @pl.when(step == 0)
    def _():
        for a in range(len(operands)):
            block_copy(a, 0).start()
        s_ref[...] = s0_ref[0]
        xt_ref[...] = jnp.zeros_like(xt_ref)

    def head_rows(c):
        return pl.ds(c, HEADS, stride=XT_PITCH)

    def channel_major(ref):
        for b in range(LANE_GROUP_BATCH):
            xt = ref[b].T
            for h in range(HEADS):
                xt_ref[b, h * XT_PITCH:h * XT_PITCH + HEAD_DIM, 0:tb] = xt[h * HEAD_DIM:(h + 1) * HEAD_DIM]

    def lane_tile(chan):
        parts = [xt_ref[b, head_rows(c), :] for c in chan for b in range(LANE_GROUP_BATCH)]
        return jnp.concatenate(parts, axis=0).T[0:tb]

    for a in range(5):
        block_copy(a, step).wait()
        channel_major(in_ref.at[a])
        for k in range(HEAD_DIM):
            rows_ref[a, k] = lane_tile((k, k))
    block_copy(5, step).wait()
    channel_major(in_ref.at[5])
    for j in range(KLO):
        vt_ref[pl.ds(j, tb, stride=STEP_PITCH), :] = lane_tile((j, KLO + j))

    @pl.when(step + 1 < pl.num_programs(0))
    def _():
        for a in range(len(operands)):
            block_copy(a, step + 1).start()

    def row(a, t, k):
        return jnp.broadcast_to(rows_ref[a, k, pl.ds(t, 1), :], (KLO, 128))

    def step_rows(t):
        return pl.ds(pl.multiple_of(t * STEP_PITCH, 8), KLO)

    def put_y(t, y):
        yt_ref[step_rows(t), :] = y

    _scan_steps(s_ref, row, lambda t: vt_ref[step_rows(t), :], put_y, tb)

    for j in range(KLO):
        steps = yt_ref[pl.ds(j, tb, stride=STEP_PITCH), :]
        if tb < 128:
            steps = jnp.concatenate([steps, jnp.zeros((128 - tb, 128), F32)], axis=0)
        tile = steps.T
        for half in range(2):
            for b in range(LANE_GROUP_BATCH):
                r0 = (half * LANE_GROUP_BATCH + b) * HEADS
                xt_ref[b, head_rows(half * KLO + j), :] = tile[r0:r0 + HEADS]
    for b in range(LANE_GROUP_BATCH):
        heads = [xt_ref[b, h * XT_PITCH:h * XT_PITCH + HEAD_DIM, 0:tb] for h in range(HEADS)]
        y_ref[b] = jnp.concatenate(heads, axis=0).T

    @pl.when(pl.program_id(0) == pl.num_programs(0) - 1)
    def _():
        st_ref[0] = s_ref[...]


def _scan_tok(ops, v, s0, tb):
    nb, t, _ = v.shape
    tok_spec = pl.BlockSpec((nb, tb, WIDTH), lambda j: (0, j, 0))
    st_spec = pl.BlockSpec((1, HEAD_DIM, KLO, 128), lambda j: (0, 0, 0, 0))
    return pl.pallas_call(
        functools.partial(_scan_tok_body, tb=tb),
        grid=(t // tb,),
        in_specs=[pl.BlockSpec(memory_space=pl.ANY)] * 6 + [st_spec],
        out_specs=[tok_spec, st_spec],
        out_shape=[jax.ShapeDtypeStruct((nb, t, WIDTH), F32),
                   jax.ShapeDtypeStruct((1, HEAD_DIM, KLO, 128), F32)],
        scratch_shapes=[pltpu.VMEM((HEAD_DIM, KLO, 128), F32),
                        pltpu.VMEM((5, HEAD_DIM, tb, 128), F32),
                        pltpu.VMEM((nb, HEADS * XT_PITCH, 128), F32),
                        pltpu.VMEM((tb * STEP_PITCH, 128), F32),
                        pltpu.VMEM((tb * STEP_PITCH, 128), F32),
                        pltpu.VMEM((6, nb, tb, WIDTH), F32),
                        pltpu.SemaphoreType.DMA((6,))],
        compiler_params=pltpu.CompilerParams(
            dimension_semantics=("arbitrary",), vmem_limit_bytes=VMEM_LIMIT),
        name="rwkv_scan_tok",
    )(*ops, v, s0)


def _scan(rows, v, s0, tb):
    g, t = v.shape[0], v.shape[1]
    row_spec = pl.BlockSpec((1, tb, KLO, 128), lambda i, j: (i, j, 0, 0))
    st_spec = pl.BlockSpec((1, HEAD_DIM, KLO, 128), lambda i, j: (i, 0, 0, 0))
    return pl.pallas_call(
        functools.partial(_scan_body, tb=tb),
        grid=(g, t // tb),
        in_specs=[row_spec] * 6 + [st_spec],
        out_specs=[row_spec, st_spec],
        out_shape=[jax.ShapeDtypeStruct((g, t, KLO, 128), F32),
                   jax.ShapeDtypeStruct((g, HEAD_DIM, KLO, 128), F32)],
        scratch_shapes=[pltpu.VMEM((HEAD_DIM, KLO, 128), F32),
                        pltpu.VMEM((5, tb, HEAD_DIM, 128), F32)],
        compiler_params=pltpu.CompilerParams(
            dimension_semantics=("parallel", "arbitrary"), vmem_limit_bytes=VMEM_LIMIT),
        name="rwkv_scan",
    )(*rows, v, s0)


def _rows_to_scan(x, n_batch, t):
    g = n_batch // LANE_GROUP_BATCH
    x = x.reshape(g, LANE_GROUP_BATCH, t, HEADS, 2, KLO)
    return x.transpose(0, 2, 5, 4, 1, 3).reshape(g, t, KLO, 128)


def _rows_from_scan(y, n_batch, t):
    g = n_batch // LANE_GROUP_BATCH
    y = y.reshape(g, t, KLO, 2, LANE_GROUP_BATCH, HEADS).transpose(0, 4, 1, 5, 3, 2)
    return y.reshape(n_batch * t, WIDTH)


def _state_to_scan(s, n_batch):
    g = n_batch // LANE_GROUP_BATCH
    s = s.reshape(g, LANE_GROUP_BATCH, HEADS, 2, KLO, HEAD_DIM)
    return s.transpose(0, 5, 4, 3, 1, 2).reshape(g, HEAD_DIM, KLO, 128)


def _state_from_scan(s, n_batch):
    g = n_batch // LANE_GROUP_BATCH
    s = s.reshape(g, HEAD_DIM, KLO, 2, LANE_GROUP_BATCH, HEADS)
    return s.transpose(0, 4, 5, 3, 2, 1).reshape(n_batch, HEADS, HEAD_DIM, HEAD_DIM)


def _merge_body(att_ref, y_ref, g_ref, cb_ref, ga_ref, gb_ref, x_ref, lnw_ref, gpost_ref, bd_ref,
                wa_ref, wr_ref, wo_ref, o_ref):
    bd = bd_ref[...]
    y = y_ref[...]
    d = y - _segsum(y, bd) * (1.0 / HEAD_DIM)
    var = _segsum(d * d, bd) * (1.0 / HEAD_DIM)
    rw = ((d * lax.rsqrt(var + GN_EPS) * lnw_ref[...] + cb_ref[...]) * g_ref[...]).astype(BF16)
    rw_o = _dot(rw, wr_ref[...])
    att_o = _dot(att_ref[...], wa_ref[...])
    mixed = (_sigmoid(ga_ref[...]) * att_o + _sigmoid(gb_ref[...]) * rw_o).astype(BF16)
    z = _dot(mixed, wo_ref[...])
    o_ref[...] = x_ref[...] + z * _rms_scale(z) * gpost_ref[...]


def _merge(att, y, g, cb, proj, x, ln_w, g_post, ones_bd, w_att, w_rwkv, w_out):
    m = x.shape[0]
    tm = min(MERGE_TM, m)
    tok = lambda w: pl.BlockSpec((tm, w), lambda i: (i, 0))
    const = lambda a: pl.BlockSpec(a.shape, lambda i: (0,) * a.ndim)
    return pl.pallas_call(
        _merge_body,
        grid=(m // tm,),
        in_specs=[
            tok(WIDTH), tok(WIDTH), tok(WIDTH), tok(WIDTH),
            pl.BlockSpec((tm, D_MODEL), lambda i: (i, COL_GA // D_MODEL)),
            pl.BlockSpec((tm, D_MODEL), lambda i: (i, COL_GB // D_MODEL)),
            tok(D_MODEL), const(ln_w), const(g_post), const(ones_bd),
            const(w_att), const(w_rwkv), const(w_out),
        ],
        out_specs=tok(D_MODEL),
        out_shape=jax.ShapeDtypeStruct((m, D_MODEL), F32),
        compiler_params=pltpu.CompilerParams(
            dimension_semantics=("parallel",), vmem_limit_bytes=VMEM_LIMIT),
        name="merge",
    )(att, y, g, cb, proj, proj, x, ln_w, g_post, ones_bd, w_att, w_rwkv, w_out)


def _pad_cols(x, width):
    return jnp.pad(x, ((0, 0), (0, width - x.shape[1])))


def _pad_rows(x, height):
    return jnp.pad(x, ((0, height - x.shape[0]), (0, 0)))


def _reorder_shift_cols(x):
    o = 3 * WIDTH
    wd = x[..., o:o + DECAY_LORA]
    ad = x[..., o + DECAY_LORA:o + DECAY_LORA + ICLR_LORA]
    gd = x[..., o + DECAY_LORA + ICLR_LORA:]
    pad = [(0, 0)] * (x.ndim - 1)
    lora = jnp.concatenate([jnp.pad(wd, pad + [(0, LORA_PAD - DECAY_LORA)]),
                            jnp.pad(ad, pad + [(0, LORA_PAD - ICLR_LORA)]), gd], axis=-1)
    return x[..., :o], lora


def _shift_row_from_proj(rows):
    lo = rows[:, COL_LORA:]
    return jnp.concatenate([rows[:, COL_RKV:COL_RKV + 3 * WIDTH], lo[:, :DECAY_LORA],
                            lo[:, LORA_PAD:LORA_PAD + ICLR_LORA], lo[:, 2 * LORA_PAD:]], axis=-1)


def _bias_diagonals(rel_bias):
    u = np.arange(ATT_DIAG)[None, :]
    s = np.arange(ATT_LEAD + 1)[:, None]
    idx = np.clip(ATT_KEYS - u - ATT_Q * s, -MAX_REL, MAX_REL) + MAX_REL
    tab = jnp.take(rel_bias, jnp.asarray(idx.reshape(-1), jnp.int32), axis=1)
    return tab.reshape(HEADS, ATT_LEAD + 1, ATT_DIAG).transpose(1, 0, 2)


def _bias_sample(rel_bias, t, n_past):
    rel = np.arange(t)[:, None] + n_past - np.arange(n_past + t)[None, :]
    idx = np.clip(rel, -MAX_REL, MAX_REL) + MAX_REL
    tab = jnp.take(rel_bias, jnp.asarray(idx.reshape(-1), jnp.int32), axis=1).reshape(HEADS, t, n_past + t)
    return tab[:, :, :n_past], tab[:, :, n_past:]


def _layer(x, n_batch, t, attend, first_rows, prep_tm, wkv0, scan_tb, wts):
    m = n_batch * t
    x = _ffn(x, *wts["ffn1"])
    proj = _proj(x, *wts["proj"])
    att = attend(proj)

    prep = _prep(proj, 0, m, prep_tm, *first_rows(proj), wts["prep"])
    r, wdec, k2, v, kk, kka, g, cb = prep
    ops = (wdec, kka, k2, r, kk)
    if n_batch == LANE_GROUP_BATCH and scan_tb % 64 == 0:
        tok = lambda a: a.reshape(n_batch, t, WIDTH)
        y, s_t = _scan_tok([tok(a) for a in ops], tok(v), _state_to_scan(wkv0, n_batch), scan_tb)
        y = y.reshape(m, WIDTH)
    else:
        rows = [_rows_to_scan(a, n_batch, t) for a in ops]
        y, s_t = _scan(rows, _rows_to_scan(v, n_batch, t), _state_to_scan(wkv0, n_batch), scan_tb)
        y = _rows_from_scan(y, n_batch, t)

    x = _merge(att, y, g, cb, proj, x, *wts["merge"])
    x = _ffn(x, *wts["ffn2"])

    last = jnp.concatenate([proj[(b + 1) * t - 1:(b + 1) * t] for b in range(n_batch)])
    return x, proj, _state_from_scan(s_t, n_batch), _shift_row_from_proj(last)[:, None]


def kernel(x_prompt, x_sample, cache_k, cache_v, state_wkv, state_shift, norm_ffn1_pre, norm_ffn1_post, w_ffn1_in, w_ffn1_down, norm_mix_pre, norm_mix_post, w_in, rel_bias, w_att_out, rwkv_mu, rwkv_w0, rwkv_w_up, rwkv_a0, rwkv_a_up, rwkv_g_up, rwkv_k_k, rwkv_k_a, rwkv_r_k, rwkv_ln_w, rwkv_ln_b, w_rwkv_out, w_out, norm_ffn2_pre, norm_ffn2_post, w_ffn2_in, w_ffn2_down):
    depth = w_in.shape[0]
    batch, seq, _ = x_prompt.shape
    dec_batch, dec_seq, _ = x_sample.shape
    n_past = cache_k.shape[2]
    m_p, m_s = batch * seq, dec_batch * dec_seq
    keep = min(PAST_BAND, seq)
    assert n_past == PAST_BAND and seq % ATT_Q == 0 and seq // ATT_Q > ATT_LEAD
    assert m_p % FFN_TM == 0 and m_p % PROJ_TM == 0 and m_p % MERGE_TM == 0 and seq % PREP_TM == 0
    assert seq % SCAN_TB == 0 and batch == LANE_GROUP_BATCH and dec_batch % LANE_GROUP_BATCH == 0

    x_p = x_prompt.reshape(m_p, D_MODEL)
    x_s = x_sample.reshape(m_s, D_MODEL)
    ones_bd = jnp.asarray(np.kron(np.eye(SEG_HEADS), np.ones((HEAD_DIM, HEAD_DIM))), BF16)
    row = lambda v: v.reshape(1, -1)
    heads = lambda a, n, t: a.reshape(n, t, HEADS, HEAD_DIM)

    outs = {k: [] for k in ("kp", "vp", "wp", "sp", "ks", "vs", "ws", "ss")}
    for l in range(depth):
        w_proj = _repack_proj_weight(w_in[l].T)
        mu_rkv, mu_lora = _reorder_shift_cols(rwkv_mu[l][None])
        wts = {
            "ffn1": (row(norm_ffn1_pre[l]), row(norm_ffn1_post[l]),
                     w_ffn1_in[l].astype(BF16), w_ffn1_down[l].astype(BF16)),
            "proj": (row(norm_mix_pre[l]), w_proj),
            "prep": (mu_rkv, mu_lora, row(rwkv_w0[l]), row(rwkv_a0[l]), row(rwkv_k_k[l]), row(rwkv_k_a[l]),
                     row(rwkv_r_k[l]), row(rwkv_ln_b[l]),
                     *_split_bf16(_pad_rows(rwkv_w_up[l], LORA_PAD)),
                     *_split_bf16(_pad_rows(rwkv_a_up[l], LORA_PAD)),
                     *_split_bf16(rwkv_g_up[l]), ones_bd),
            "merge": (row(rwkv_ln_w[l]), row(norm_mix_post[l]), ones_bd,
                      w_att_out[l].astype(BF16), w_rwkv_out[l].astype(BF16), w_out[l].astype(BF16)),
            "ffn2": (row(norm_ffn2_pre[l]), row(norm_ffn2_post[l]),
                     w_ffn2_in[l].astype(BF16), w_ffn2_down[l].astype(BF16)),
        }

        diag = _bias_diagonals(rel_bias[l])
        starts_seq = (jnp.arange(m_p // PREP_TM) % (seq // PREP_TM) == 0)[:, None]

        def first_rows_prompt(proj):
            last_of_tile = proj.reshape(m_p // PREP_TM, PREP_TM, PROJ_COLS)[:, PREP_TM - 1]
            first = jnp.where(starts_seq, 0.0, jnp.roll(last_of_tile, 1, axis=0))[:, None]
            return first[..., COL_RKV:COL_RKV + 3 * WIDTH], first[..., COL_LORA:]

        x_p, proj_p, wkv_p, sh_p = _layer(
            x_p, batch, seq, lambda proj: _attn_prompt(proj, diag, batch, seq), first_rows_prompt, PREP_TM,
            jnp.zeros((batch, HEADS, HEAD_DIM, HEAD_DIM), F32), SCAN_TB, wts)
        tail = lambda col: jnp.stack([proj_p[(b + 1) * seq - keep:(b + 1) * seq, col:col + WIDTH]
                                      for b in range(batch)])
        outs["kp"].append(heads(tail(COL_K), batch, keep))
        outs["vp"].append(heads(tail(COL_V), batch, keep))
        outs["wp"].append(wkv_p.astype(state_wkv.dtype))
        outs["sp"].append(sh_p)

        chan_major = lambda c: c.transpose(0, 2, 3, 1).reshape(dec_batch, WIDTH, n_past)
        ck, cv = chan_major(cache_k[l]), chan_major(cache_v[l])
        bias_c, bias_n = _bias_sample(rel_bias[l], dec_seq, n_past)
        x_s, proj_s, wkv_s, sh_s = _layer(
            x_s, dec_batch, dec_seq,
            lambda proj: _attn_sample(proj, 0, ck, cv, bias_c, bias_n, dec_batch, dec_seq),
            lambda proj: _reorder_shift_cols(state_shift[l]), dec_seq,
            state_wkv[l].astype(F32), dec_seq, wts)
        outs["ks"].append(heads(proj_s[:, COL_K:COL_K + WIDTH], dec_batch, dec_seq))
        outs["vs"].append(heads(proj_s[:, COL_V:COL_V + WIDTH], dec_batch, dec_seq))
        outs["ws"].append(wkv_s.astype(state_wkv.dtype))
        outs["ss"].append(sh_s)

    st = lambda k: jnp.stack(outs[k])
    return (x_p.reshape(batch, seq, D_MODEL), x_s.reshape(dec_batch, dec_seq, D_MODEL),
            st("kp"), st("vp"), st("wp"), st("sp"), st("ks"), st("vs"), st("ws"), st("ss"))
```

```python
import functools

import numpy as np
import jax
import jax.numpy as jnp
from jax import lax
from jax.experimental import pallas as pl
from jax.experimental.pallas import tpu as pltpu

F32 = jnp.float32
BF16 = jnp.bfloat16

D_MODEL = 2048
CHUNK = 64
LEFT_CHUNKS = 8
PAST_BAND = LEFT_CHUNKS * CHUNK
BAND = PAST_BAND + CHUNK
HEAD_DIM = 64
HEADS = 16
WIDTH = HEADS * HEAD_DIM
DECAY_LORA = 96
ICLR_LORA = 96
GATE_LORA = 256
LORA_PAD = 128
LORA_COLS = 2 * LORA_PAD + GATE_LORA
D_FF = 5632
MAX_REL = 128
RMS_EPS = 1e-6
GN_EPS = 64e-5
NEG_INF = -1e30

COL_Q, COL_K, COL_V = 0, WIDTH, 2 * WIDTH
COL_RKV = 3 * WIDTH
COL_GA = 6 * WIDTH
COL_GB = COL_GA + D_MODEL
COL_LORA = COL_GB + D_MODEL
PROJ_COLS = COL_LORA + LORA_COLS

VMEM_LIMIT = 56 * 1024 * 1024
VMEM_LIMIT_MAX = 60 * 1024 * 1024

FFN_TM = 1024
FFN_TF = 256
PROJ_TM = 1024
PROJ_TN = 1536
REPACK_ROWS = 768
SEG_HEADS = 4
MERGE_TM = 256
PREP_TM = 256
SCAN_TB = 128
ATT_Q = 2 * CHUNK
ATT_KEYS = PAST_BAND + ATT_Q
ATT_LEAD = PAST_BAND // ATT_Q
ATT_DIAG = ATT_KEYS + ATT_Q
LANE_GROUP_BATCH = 4
KLO = HEAD_DIM // 2
STEP_PITCH = KLO + 8
XT_PITCH = HEAD_DIM + 8


def _sigmoid(x):
    return 1.0 / (1.0 + jnp.exp(-x))


def _split_bf16(x):
    hi = x.astype(BF16)
    lo = (x - hi.astype(F32)).astype(BF16)
    return hi, lo


def _dot(a, b):
    return jnp.dot(a, b, preferred_element_type=F32)


def _dot_f32(a, b_hi, b_lo):
    a_hi, a_lo = _split_bf16(a)
    return _dot(a_hi, b_hi) + _dot(a_lo, b_hi) + _dot(a_hi, b_lo)


def _segsum(x, ones_bd):
    hi, lo = _split_bf16(x)
    width = ones_bd.shape[0]
    outs = []
    for j in range(x.shape[-1] // width):
        sl = slice(j * width, (j + 1) * width)
        outs.append(_dot(hi[:, sl], ones_bd) + _dot(lo[:, sl], ones_bd))
    return jnp.concatenate(outs, axis=-1)


def _rms_scale(x):
    return lax.rsqrt(jnp.mean(x * x, axis=-1, keepdims=True) + RMS_EPS)


def _ffn_body(x_ref, gpre_ref, gpost_ref, wg_ref, wu_ref, wd_ref, o_ref, h_ref, *, n_f):
    f = pl.program_id(1)

    @pl.when(f == 0)
    def _():
        x = x_ref[...]
        h_ref[...] = (x * _rms_scale(x) * gpre_ref[...]).astype(BF16)
        o_ref[...] = jnp.zeros_like(o_ref)

    h = h_ref[...]
    g = _dot(h, wg_ref[...].astype(BF16))
    u = _dot(h, wu_ref[...].astype(BF16))
    act = (g * _sigmoid(g) * u).astype(BF16)
    o_ref[...] += _dot(act, wd_ref[...].astype(BF16))

    @pl.when(f == n_f - 1)
    def _():
        y = o_ref[...]
        o_ref[...] = x_ref[...] + 0.5 * (y * _rms_scale(y) * gpost_ref[...])


def _ffn(x, g_pre, g_post, w_in, w_down):
    m = x.shape[0]
    tm = min(FFN_TM, m)
    n_m, n_f = m // tm, D_FF // FFN_TF
    return pl.pallas_call(
        functools.partial(_ffn_body, n_f=n_f),
        grid=(n_m, n_f),
        in_specs=[
            pl.BlockSpec((tm, D_MODEL), lambda i, f: (i, 0)),
            pl.BlockSpec((1, D_MODEL), lambda i, f: (0, 0)),
            pl.BlockSpec((1, D_MODEL), lambda i, f: (0, 0)),
            pl.BlockSpec((D_MODEL, FFN_TF), lambda i, f: (0, f)),
            pl.BlockSpec((D_MODEL, FFN_TF), lambda i, f: (0, f + n_f)),
            pl.BlockSpec((FFN_TF, D_MODEL), lambda i, f: (f, 0)),
        ],
        out_specs=pl.BlockSpec((tm, D_MODEL), lambda i, f: (i, 0), pipeline_mode=pl.Buffered(1)),
        out_shape=jax.ShapeDtypeStruct((m, D_MODEL), F32),
        scratch_shapes=[pltpu.VMEM((tm, D_MODEL), BF16)],
        compiler_params=pltpu.CompilerParams(
            dimension_semantics=("parallel", "arbitrary"), vmem_limit_bytes=VMEM_LIMIT),
        name="ffn",
    )(x, g_pre, g_post, w_in, w_in, w_down)


def _repack_body(wt_hbm, o_ref, buf_ref, sem, *, n_head):
    j = pl.program_id(0)
    last = pl.num_programs(0) - 1
    o_ga = COL_GA + DECAY_LORA + ICLR_LORA + GATE_LORA

    def row_copy(src_row, n_rows, dst_row, slot):
        return pltpu.make_async_copy(wt_hbm.at[pl.ds(src_row, n_rows), :],
                                     buf_ref.at[pl.ds(dst_row, n_rows), :], sem.at[slot])

    @pl.when(j < last)
    def _():
        src = jnp.where(j < n_head, j * REPACK_ROWS, o_ga + (j - n_head) * REPACK_ROWS)
        copy = row_copy(src, REPACK_ROWS, 0, 0)
        copy.start()
        copy.wait()

    @pl.when(j == last)
    def _():
        n_gate = COL_LORA - (PROJ_COLS - REPACK_ROWS)
        pieces = [(o_ga + COL_LORA - COL_GA - n_gate, n_gate, 0),
                  (COL_GA, DECAY_LORA, n_gate),
                  (COL_GA + DECAY_LORA, ICLR_LORA, n_gate + LORA_PAD),
                  (COL_GA + DECAY_LORA + ICLR_LORA, GATE_LORA, n_gate + 2 * LORA_PAD)]
        for dst, width in ((n_gate, DECAY_LORA), (n_gate + LORA_PAD, ICLR_LORA)):
            buf_ref[dst + width:dst + LORA_PAD, :] = jnp.zeros((LORA_PAD - width, D_MODEL), F32)
        copies = [row_copy(src, n, dst, slot) for slot, (src, n, dst) in enumerate(pieces)]
        for copy in copies:
            copy.start()
        for copy in copies:
            copy.wait()

    o_ref[...] = buf_ref[...].astype(BF16)


def _repack_proj_weight(wt):
    assert COL_GA % REPACK_ROWS == 0 and PROJ_COLS % REPACK_ROWS == 0
    assert PROJ_COLS - REPACK_ROWS <= COL_LORA <= PROJ_COLS - LORA_COLS
    return pl.pallas_call(
        functools.partial(_repack_body, n_head=COL_GA // REPACK_ROWS),
        grid=(PROJ_COLS // REPACK_ROWS,),
        in_specs=[pl.BlockSpec(memory_space=pl.ANY)],
        out_specs=pl.BlockSpec((REPACK_ROWS, D_MODEL), lambda j: (j, 0)),
        out_shape=jax.ShapeDtypeStruct((PROJ_COLS, D_MODEL), BF16),
        scratch_shapes=[pltpu.VMEM((REPACK_ROWS, D_MODEL), F32), pltpu.SemaphoreType.DMA((4,))],
        compiler_params=pltpu.CompilerParams(
            dimension_semantics=("arbitrary",), vmem_limit_bytes=VMEM_LIMIT),
        name="repack_proj_weight",
    )(wt)


def _proj_body(x_ref, g_ref, w_ref, o_ref, h_ref):
    @pl.when(pl.program_id(1) == 0)
    def _():
        x = x_ref[...]
        h_ref[...] = (x * _rms_scale(x) * g_ref[...]).astype(BF16)

    o_ref[...] = _qk(h_ref[...], w_ref[...])


def _proj(x, g, w):
    m = x.shape[0]
    tm = min(PROJ_TM, m)
    return pl.pallas_call(
        _proj_body,
        grid=(m // tm, PROJ_COLS // PROJ_TN),
        in_specs=[
            pl.BlockSpec((tm, D_MODEL), lambda i, n: (i, 0)),
            pl.BlockSpec((1, D_MODEL), lambda i, n: (0, 0)),
            pl.BlockSpec((PROJ_TN, D_MODEL), lambda i, n: (n, 0)),
        ],
        out_specs=pl.BlockSpec((tm, PROJ_TN), lambda i, n: (i, n)),
        out_shape=jax.ShapeDtypeStruct((m, PROJ_COLS), F32),
        scratch_shapes=[pltpu.VMEM((tm, D_MODEL), BF16)],
        compiler_params=pltpu.CompilerParams(
            dimension_semantics=("parallel", "arbitrary"), vmem_limit_bytes=VMEM_LIMIT),
        name="proj",
    )(x, g, w)


def _softmax_pv(parts):
    m = functools.reduce(jnp.maximum, [jnp.max(s, axis=-1, keepdims=True) for s, _ in parts])
    num, den = 0.0, 0.0
    for s, v in parts:
        p = jnp.exp(s - m)
        den = den + jnp.sum(p, axis=-1, keepdims=True)
        num = num + _dot(p.astype(BF16), v)
    return num / den


def _qk(q, k):
    return lax.dot_general(q, k, (((1,), (1,)), ((), ())), preferred_element_type=F32)


def _attn_prompt_body(q_ref, k_ref, v_ref, diag_ref, o_ref, kbf_ref, vbf_ref, bias_ref):
    c = pl.program_id(1)
    shift = jnp.maximum(ATT_LEAD - c, 0)

    @pl.when(c == 0)
    def _():
        kbf_ref[...] = k_ref[...].astype(BF16)
        vbf_ref[...] = v_ref[...].T.astype(BF16)

    @pl.when((pl.program_id(0) == 0) & (c <= ATT_LEAD))
    def _():
        key = lax.broadcasted_iota(jnp.int32, (ATT_KEYS, ATT_Q), 0) // CHUNK
        qry = lax.broadcasted_iota(jnp.int32, (ATT_KEYS, ATT_Q), 1) // CHUNK
        dist = (LEFT_CHUNKS + qry) - (key + shift * (ATT_Q // CHUNK))
        in_band = (dist >= 0) & (dist <= LEFT_CHUNKS)
        for h in range(HEADS):
            diag = jnp.broadcast_to(diag_ref[shift, pl.ds(h, 1), :], (ATT_Q, ATT_DIAG))
            toep = pltpu.roll(diag, ATT_DIAG - ATT_Q, axis=1, stride=1, stride_axis=0)
            bias_ref[shift, h] = jnp.where(in_band, toep[:, :ATT_KEYS].T, NEG_INF)

    start = pl.multiple_of(jnp.maximum(c - ATT_LEAD, 0) * ATT_Q, ATT_Q)
    lane = lax.broadcasted_iota(jnp.int32, (ATT_Q, 128), 1)
    for pair in range(HEADS // 2):
        sl = slice(pair * 128, (pair + 1) * 128)
        kp = kbf_ref[pl.ds(start, ATT_KEYS), sl]
        vp = vbf_ref[sl, pl.ds(start, ATT_KEYS)]
        qp = q_ref[:, sl] * (HEAD_DIM ** -0.5)
        halves = []
        for half in range(2):
            qm = jnp.where((lane >= HEAD_DIM) == bool(half), qp, 0.0).astype(BF16)
            s = _qk(kp, qm) + bias_ref[shift, 2 * pair + half]
            p = jnp.exp(s - jnp.max(s, axis=0, keepdims=True))
            p = (p * (1.0 / jnp.sum(p, axis=0, keepdims=True))).astype(BF16)
            halves.append(_dot(vp, p))
        chan = lax.broadcasted_iota(jnp.int32, (128, ATT_Q), 0)
        o_ref[:, sl] = jnp.where(chan < HEAD_DIM, halves[0], halves[1]).T.astype(BF16)


def _attn_prompt(proj, diag, batch, seq):
    n_c = seq // ATT_Q
    once = dict(pipeline_mode=pl.Buffered(1))
    return pl.pallas_call(
        _attn_prompt_body,
        grid=(batch, n_c),
        in_specs=[
            pl.BlockSpec((ATT_Q, WIDTH), lambda b, c: (b * n_c + c, COL_Q // WIDTH)),
            pl.BlockSpec((seq, WIDTH), lambda b, c: (b, COL_K // WIDTH), **once),
            pl.BlockSpec((seq, WIDTH), lambda b, c: (b, COL_V // WIDTH), **once),
            pl.BlockSpec(diag.shape, lambda b, c: (0, 0, 0)),
        ],
        out_specs=pl.BlockSpec((ATT_Q, WIDTH), lambda b, c: (b * n_c + c, 0)),
        out_shape=jax.ShapeDtypeStruct((batch * seq, WIDTH), BF16),
        scratch_shapes=[pltpu.VMEM((seq, WIDTH), BF16), pltpu.VMEM((WIDTH, seq), BF16),
                        pltpu.VMEM((ATT_LEAD + 1, HEADS, ATT_KEYS, ATT_Q), F32)],
        compiler_params=pltpu.CompilerParams(
            dimension_semantics=("arbitrary", "arbitrary"), vmem_limit_bytes=VMEM_LIMIT_MAX),
        name="attn_prompt",
    )(proj, proj, proj, diag)


def _attn_sample_body(q_ref, kn_ref, vn_ref, ck_ref, cv_ref, bias_c_ref, bias_n_ref, o_ref):
    t = q_ref.shape[0]
    lane = lax.broadcasted_iota(jnp.int32, (t, 128), 1)
    slabs = [slice(pair * 128, (pair + 1) * 128) for pair in range(HEADS // 2)]

    s_c, s_n = [], []
    for h in range(HEADS):
        sl = slabs[h // 2]
        qm = jnp.where((lane >= HEAD_DIM) == bool(h % 2), q_ref[:, sl] * (HEAD_DIM ** -0.5), 0.0).astype(BF16)
        s_c.append(_dot(qm, ck_ref[0, sl, :].astype(BF16)) + bias_c_ref[h])
        s_n.append(_qk(qm, kn_ref[:, sl].astype(BF16)) + bias_n_ref[h])
    s_c = jnp.concatenate(s_c, axis=0)
    s_n = jnp.concatenate(s_n, axis=0)
    m = jnp.maximum(jnp.max(s_c, axis=-1, keepdims=True), jnp.max(s_n, axis=-1, keepdims=True))
    p_c = jnp.exp(s_c - m)
    p_n = jnp.exp(s_n - m)
    inv = 1.0 / (jnp.sum(p_c, axis=-1, keepdims=True) + jnp.sum(p_n, axis=-1, keepdims=True))
    p_c = (p_c * inv).astype(BF16)
    p_n = (p_n * inv).astype(BF16)
    for pair, sl in enumerate(slabs):
        cv = cv_ref[0, sl, :].astype(BF16)
        vn = vn_ref[:, sl].astype(BF16)
        halves = []
        for h in (2 * pair, 2 * pair + 1):
            rows = slice(h * t, (h + 1) * t)
            halves.append(_qk(p_c[rows], cv) + _dot(p_n[rows], vn))
        o_ref[:, sl] = jnp.where(lane < HEAD_DIM, halves[0], halves[1]).astype(BF16)


def _attn_sample(proj, row0, cache_k, cache_v, bias_c, bias_n, batch, t):
    n_past = cache_k.shape[2]
    blk0 = row0 // t
    return pl.pallas_call(
        _attn_sample_body,
        grid=(batch,),
        in_specs=[
            pl.BlockSpec((t, WIDTH), lambda b: (blk0 + b, COL_Q // WIDTH)),
            pl.BlockSpec((t, WIDTH), lambda b: (blk0 + b, COL_K // WIDTH)),
            pl.BlockSpec((t, WIDTH), lambda b: (blk0 + b, COL_V // WIDTH)),
            pl.BlockSpec((1, WIDTH, n_past), lambda b: (b, 0, 0)),
            pl.BlockSpec((1, WIDTH, n_past), lambda b: (b, 0, 0)),
            pl.BlockSpec((HEADS, t, n_past), lambda b: (0, 0, 0)),
            pl.BlockSpec((HEADS, t, t), lambda b: (0, 0, 0)),
        ],
        out_specs=pl.BlockSpec((t, WIDTH), lambda b: (b, 0)),
        out_shape=jax.ShapeDtypeStruct((batch * t, WIDTH), BF16),
        compiler_params=pltpu.CompilerParams(
            dimension_semantics=("parallel",), vmem_limit_bytes=VMEM_LIMIT),
        name="attn_sample",
    )(proj, proj, proj, cache_k, cache_v, bias_c, bias_n)


def _prep_body(p_ref, pl_ref, bp_ref, bl_ref, mu_ref, mul_ref, w0_ref, a0_ref, kk_ref, ka_ref, rk_ref,
               lnb_ref, wup_hi, wup_lo, aup_hi, aup_lo, gup_hi, gup_lo, bd_ref,
               r_o, w_o, k_o, v_o, kk_o, kka_o, g_o, cb_o):
    def shifted(p, first_row, mu):
        row = lax.broadcasted_iota(jnp.int32, p.shape, 0)
        prev = jnp.where(row == 0, first_row, pltpu.roll(p, 1, axis=0))
        return p + (prev - p) * mu

    xm = shifted(p_ref[...], bp_ref[0], mu_ref[...])
    xl = shifted(pl_ref[...], bl_ref[0], mul_ref[...])
    r = xm[:, 0:WIDTH]
    k = xm[:, WIDTH:2 * WIDTH]
    v = xm[:, 2 * WIDTH:3 * WIDTH]
    wd = xl[:, 0:LORA_PAD]
    ad = xl[:, LORA_PAD:2 * LORA_PAD]
    gd = xl[:, 2 * LORA_PAD:]

    z = -(w0_ref[...] + _dot_f32(jnp.tanh(wd), wup_hi[...], wup_lo[...]))
    softplus = jnp.maximum(z, 0.0) + jnp.log(1.0 + jnp.exp(-jnp.abs(z)))
    decay = jnp.exp(-jnp.exp(-softplus - 0.5))
    a = _sigmoid(a0_ref[...] + _dot_f32(ad, aup_hi[...], aup_lo[...]))
    g = _dot_f32(_sigmoid(gd), gup_hi[...], gup_lo[...])

    bd = bd_ref[...]
    kk = k * kk_ref[...]
    kk = kk / jnp.maximum(jnp.sqrt(_segsum(kk * kk, bd)), 1e-12)
    k2 = k * (1.0 + (a - 1.0) * ka_ref[...])
    bonus = _segsum(r * k2 * rk_ref[...], bd) * v

    r_o[...] = r
    w_o[...] = decay
    k_o[...] = k2
    v_o[...] = v
    kk_o[...] = kk
    kka_o[...] = kk * a
    g_o[...] = g
    cb_o[...] = lnb_ref[...] + bonus


def _prep(proj, row0, n_rows, tm, first_rkv, first_lora, params):
    n_t = n_rows // tm
    blk0 = row0 // tm
    vec = lambda w: pl.BlockSpec((1, w), lambda i: (0, 0))
    mat = lambda r: pl.BlockSpec((r, WIDTH), lambda i: (0, 0))
    out = pl.BlockSpec((tm, WIDTH), lambda i: (i, 0))
    return pl.pallas_call(
        _prep_body,
        grid=(n_t,),
        in_specs=[
            pl.BlockSpec((tm, 3 * WIDTH), lambda i: (blk0 + i, COL_RKV // (3 * WIDTH))),
            pl.BlockSpec((tm, LORA_COLS), lambda i: (blk0 + i, COL_LORA // LORA_COLS)),
            pl.BlockSpec((1, 1, 3 * WIDTH), lambda i: (i, 0, 0)),
            pl.BlockSpec((1, 1, LORA_COLS), lambda i: (i, 0, 0)),
            vec(3 * WIDTH), vec(LORA_COLS), vec(WIDTH), vec(WIDTH), vec(WIDTH), vec(WIDTH), vec(WIDTH),
            vec(WIDTH), mat(LORA_PAD), mat(LORA_PAD), mat(LORA_PAD), mat(LORA_PAD),
            mat(GATE_LORA), mat(GATE_LORA),
            pl.BlockSpec((SEG_HEADS * HEAD_DIM, SEG_HEADS * HEAD_DIM), lambda i: (0, 0)),
        ],
        out_specs=[out] * 8,
        out_shape=[jax.ShapeDtypeStruct((n_rows, WIDTH), F32)] * 8,
        compiler_params=pltpu.CompilerParams(
            dimension_semantics=("parallel",), vmem_limit_bytes=VMEM_LIMIT),
        name="rwkv_prep",
    )(proj, proj, first_rkv, first_lora, *params)


def _scan_steps(s_ref, row, get_v, put_y, n_steps):
    W, KKA, K, R, KK = range(5)
    zero = jnp.zeros((KLO, 128), F32)

    sa0 = [zero, zero]
    for k in range(HEAD_DIM):
        sa0[k % 2] = sa0[k % 2] - s_ref[k] * row(KK, 0, k)

    def step(t, sa):
        v = get_v(t)
        t_next = jnp.minimum(t + 1, n_steps - 1)
        y = [zero, zero]
        sa_next = [zero, zero]
        for k in range(HEAD_DIM):
            s_new = s_ref[k] * row(W, t, k) + sa * row(KKA, t, k) + v * row(K, t, k)
            s_ref[k] = s_new
            y[k % 2] = y[k % 2] + s_new * row(R, t, k)
            sa_next[k % 2] = sa_next[k % 2] - s_new * row(KK, t_next, k)
        put_y(t, y[0] + y[1])
        return sa_next[0] + sa_next[1]

    lax.fori_loop(0, n_steps, step, sa0[0] + sa0[1])


def _scan_body(w_ref, kka_ref, k_ref, r_ref, kk_ref, v_ref, s0_ref, y_ref, st_ref, s_ref, rows_ref, *, tb):
    @pl.when(pl.program_id(1) == 0)
    def _():
        s_ref[...] = s0_ref[0]

    lane = lax.broadcasted_iota(jnp.int32, (tb * KLO, 128), 1)
    for a, ref in enumerate((w_ref, kka_ref, k_ref, r_ref, kk_ref)):
        x = ref[0].reshape(tb * KLO, 128)
        swapped = pltpu.roll(x, HEAD_DIM, axis=1)
        rows_ref[a, :, 0:KLO, :] = jnp.where(lane < HEAD_DIM, x, swapped).reshape(tb, KLO, 128)
        rows_ref[a, :, KLO:, :] = jnp.where(lane < HEAD_DIM, swapped, x).reshape(tb, KLO, 128)

    def row(a, t, k):
        return jnp.broadcast_to(rows_ref[a, t, pl.ds(k, 1), :], (KLO, 128))

    def put_y(t, y):
        y_ref[0, t] = y

    _scan_steps(s_ref, row, lambda t: v_ref[0, t], put_y, tb)

    @pl.when(pl.program_id(1) == pl.num_programs(1) - 1)
    def _():
        st_ref[0] = s_ref[...]


def _scan_tok_body(w_hbm, kka_hbm, k_hbm, r_hbm, kk_hbm, v_hbm, s0_ref, y_ref, st_ref,
                   s_ref, rows_ref, xt_ref, vt_ref, yt_ref, in_ref, in_sem, *, tb):
    step = pl.program_id(0)
    operands = (w_hbm, kka_hbm, k_hbm, r_hbm, kk_hbm, v_hbm)

    def block_copy(a, blk):
        return pltpu.make_async_copy(operands[a].at[:, pl.ds(blk * tb, tb), :], in_ref.at[a], in_sem.at[a])

    @pl.when(step == 0)
    def _():
        for a in range(len(operands)):
            block_copy(a, 0).start()
        s_ref[...] = s0_ref[0]
        xt_ref[...] = jnp.zeros_like(xt_ref)

    def head_rows(c):
        return pl.ds(c, HEADS, stride=XT_PITCH)

    def channel_major(ref):
        for b in range(LANE_GROUP_BATCH):
            xt = ref[b].T
            for h in range(HEADS):
                xt_ref[b, h * XT_PITCH:h * XT_PITCH + HEAD_DIM, 0:tb] = xt[h * HEAD_DIM:(h + 1) * HEAD_DIM]

    def lane_tile(chan):
        parts = [xt_ref[b, head_rows(c), :] for c in chan for b in range(LANE_GROUP_BATCH)]
        return jnp.concatenate(parts, axis=0).T[0:tb]

    for a in range(5):
        block_copy(a, step).wait()
        channel_major(in_ref.at[a])
        for k in range(HEAD_DIM):
            rows_ref[a, k] = lane_tile((k, k))
    block_copy(5, step).wait()
    channel_major(in_ref.at[5])
    for j in range(KLO):
        vt_ref[pl.ds(j, tb, stride=STEP_PITCH), :] = lane_tile((j, KLO + j))

    @pl.when(step + 1 < pl.num_programs(0))
    def _():
        for a in range(len(operands)):
            block_copy(a, step + 1).start()

    def row(a, t, k):
        return jnp.broadcast_to(rows_ref[a, k, pl.ds(t, 1), :], (KLO, 128))

    def step_rows(t):
        return pl.ds(pl.multiple_of(t * STEP_PITCH, 8), KLO)

    def put_y(t, y):
        yt_ref[step_rows(t), :] = y

    _scan_steps(s_ref, row, lambda t: vt_ref[step_rows(t), :], put_y, tb)

    for j in range(KLO):
        steps = yt_ref[pl.ds(j, tb, stride=STEP_PITCH), :]
        if tb < 128:
            steps = jnp.concatenate([steps, jnp.zeros((128 - tb, 128), F32)], axis=0)
        tile = steps.T
        for half in range(2):
            for b in range(LANE_GROUP_BATCH):
                r0 = (half * LANE_GROUP_BATCH + b) * HEADS
                xt_ref[b, head_rows(half * KLO + j), :] = tile[r0:r0 + HEADS]
    for b in range(LANE_GROUP_BATCH):
        heads = [xt_ref[b, h * XT_PITCH:h * XT_PITCH + HEAD_DIM, 0:tb] for h in range(HEADS)]
        y_ref[b] = jnp.concatenate(heads, axis=0).T

    @pl.when(pl.program_id(0) == pl.num_programs(0) - 1)
    def _():
        st_ref[0] = s_ref[...]


def _scan_tok(ops, v, s0, tb):
    nb, t, _ = v.shape
    tok_spec = pl.BlockSpec((nb, tb, WIDTH), lambda j: (0, j, 0))
    st_spec = pl.BlockSpec((1, HEAD_DIM, KLO, 128), lambda j: (0, 0, 0, 0))
    return pl.pallas_call(
        functools.partial(_scan_tok_body, tb=tb),
        grid=(t // tb,),
        in_specs=[pl.BlockSpec(memory_space=pl.ANY)] * 6 + [st_spec],
        out_specs=[tok_spec, st_spec],
        out_shape=[jax.ShapeDtypeStruct((nb, t, WIDTH), F32),
                   jax.ShapeDtypeStruct((1, HEAD_DIM, KLO, 128), F32)],
        scratch_shapes=[pltpu.VMEM((HEAD_DIM, KLO, 128), F32),
                        pltpu.VMEM((5, HEAD_DIM, tb, 128), F32),
                        pltpu.VMEM((nb, HEADS * XT_PITCH, 128), F32),
                        pltpu.VMEM((tb * STEP_PITCH, 128), F32),
                        pltpu.VMEM((tb * STEP_PITCH, 128), F32),
                        pltpu.VMEM((6, nb, tb, WIDTH), F32),
                        pltpu.SemaphoreType.DMA((6,))],
        compiler_params=pltpu.CompilerParams(
            dimension_semantics=("arbitrary",), vmem_limit_bytes=VMEM_LIMIT),
        name="rwkv_scan_tok",
    )(*ops, v, s0)


def _scan(rows, v, s0, tb):
    g, t = v.shape[0], v.shape[1]
    row_spec = pl.BlockSpec((1, tb, KLO, 128), lambda i, j: (i, j, 0, 0))
    st_spec = pl.BlockSpec((1, HEAD_DIM, KLO, 128), lambda i, j: (i, 0, 0, 0))
    return pl.pallas_call(
        functools.partial(_scan_body, tb=tb),
        grid=(g, t // tb),
        in_specs=[row_spec] * 6 + [st_spec],
        out_specs=[row_spec, st_spec],
        out_shape=[jax.ShapeDtypeStruct((g, t, KLO, 128), F32),
                   jax.ShapeDtypeStruct((g, HEAD_DIM, KLO, 128), F32)],
        scratch_shapes=[pltpu.VMEM((HEAD_DIM, KLO, 128), F32),
                        pltpu.VMEM((5, tb, HEAD_DIM, 128), F32)],
        compiler_params=pltpu.CompilerParams(
            dimension_semantics=("parallel", "arbitrary"), vmem_limit_bytes=VMEM_LIMIT),
        name="rwkv_scan",
    )(*rows, v, s0)


def _rows_to_scan(x, n_batch, t):
    g = n_batch // LANE_GROUP_BATCH
    x = x.reshape(g, LANE_GROUP_BATCH, t, HEADS, 2, KLO)
    return x.transpose(0, 2, 5, 4, 1, 3).reshape(g, t, KLO, 128)


def _rows_from_scan(y, n_batch, t):
    g = n_batch // LANE_GROUP_BATCH
    y = y.reshape(g, t, KLO, 2, LANE_GROUP_BATCH, HEADS).transpose(0, 4, 1, 5, 3, 2)
    return y.reshape(n_batch * t, WIDTH)


def _state_to_scan(s, n_batch):
    g = n_batch // LANE_GROUP_BATCH
    s = s.reshape(g, LANE_GROUP_BATCH, HEADS, 2, KLO, HEAD_DIM)
    return s.transpose(0, 5, 4, 3, 1, 2).reshape(g, HEAD_DIM, KLO, 128)


def _state_from_scan(s, n_batch):
    g = n_batch // LANE_GROUP_BATCH
    s = s.reshape(g, HEAD_DIM, KLO, 2, LANE_GROUP_BATCH, HEADS)
    return s.transpose(0, 4, 5, 3, 2, 1).reshape(n_batch, HEADS, HEAD_DIM, HEAD_DIM)


def _merge_body(att_ref, y_ref, g_ref, cb_ref, ga_ref, gb_ref, x_ref, lnw_ref, gpost_ref, bd_ref,
                wa_ref, wr_ref, wo_ref, o_ref):
    bd = bd_ref[...]
    y = y_ref[...]
    d = y - _segsum(y, bd) * (1.0 / HEAD_DIM)
    var = _segsum(d * d, bd) * (1.0 / HEAD_DIM)
    rw = ((d * lax.rsqrt(var + GN_EPS) * lnw_ref[...] + cb_ref[...]) * g_ref[...]).astype(BF16)
    rw_o = _dot(rw, wr_ref[...])
    att_o = _dot(att_ref[...], wa_ref[...])
    mixed = (_sigmoid(ga_ref[...]) * att_o + _sigmoid(gb_ref[...]) * rw_o).astype(BF16)
    z = _dot(mixed, wo_ref[...])
    o_ref[...] = x_ref[...] + z * _rms_scale(z) * gpost_ref[...]


def _merge(att, y, g, cb, proj, x, ln_w, g_post, ones_bd, w_att, w_rwkv, w_out):
    m = x.shape[0]
    tm = min(MERGE_TM, m)
    tok = lambda w: pl.BlockSpec((tm, w), lambda i: (i, 0))
    const = lambda a: pl.BlockSpec(a.shape, lambda i: (0,) * a.ndim)
    return pl.pallas_call(
        _merge_body,
        grid=(m // tm,),
        in_specs=[
            tok(WIDTH), tok(WIDTH), tok(WIDTH), tok(WIDTH),
            pl.BlockSpec((tm, D_MODEL), lambda i: (i, COL_GA // D_MODEL)),
            pl.BlockSpec((tm, D_MODEL), lambda i: (i, COL_GB // D_MODEL)),
            tok(D_MODEL), const(ln_w), const(g_post), const(ones_bd),
            const(w_att), const(w_rwkv), const(w_out),
        ],
        out_specs=tok(D_MODEL),
        out_shape=jax.ShapeDtypeStruct((m, D_MODEL), F32),
        compiler_params=pltpu.CompilerParams(
            dimension_semantics=("parallel",), vmem_limit_bytes=VMEM_LIMIT),
        name="merge",
    )(att, y, g, cb, proj, proj, x, ln_w, g_post, ones_bd, w_att, w_rwkv, w_out)


def _pad_cols(x, width):
    return jnp.pad(x, ((0, 0), (0, width - x.shape[1])))


def _pad_rows(x, height):
    return jnp.pad(x, ((0, height - x.shape[0]), (0, 0)))


def _reorder_shift_cols(x):
    o = 3 * WIDTH
    wd = x[..., o:o + DECAY_LORA]
    ad = x[..., o + DECAY_LORA:o + DECAY_LORA + ICLR_LORA]
    gd = x[..., o + DECAY_LORA + ICLR_LORA:]
    pad = [(0, 0)] * (x.ndim - 1)
    lora = jnp.concatenate([jnp.pad(wd, pad + [(0, LORA_PAD - DECAY_LORA)]),
                            jnp.pad(ad, pad + [(0, LORA_PAD - ICLR_LORA)]), gd], axis=-1)
    return x[..., :o], lora


def _shift_row_from_proj(rows):
    lo = rows[:, COL_LORA:]
    return jnp.concatenate([rows[:, COL_RKV:COL_RKV + 3 * WIDTH], lo[:, :DECAY_LORA],
                            lo[:, LORA_PAD:LORA_PAD + ICLR_LORA], lo[:, 2 * LORA_PAD:]], axis=-1)


def _bias_diagonals(rel_bias):
    u = np.arange(ATT_DIAG)[None, :]
    s = np.arange(ATT_LEAD + 1)[:, None]
    idx = np.clip(ATT_KEYS - u - ATT_Q * s, -MAX_REL, MAX_REL) + MAX_REL
    tab = jnp.take(rel_bias, jnp.asarray(idx.reshape(-1), jnp.int32), axis=1)
    return tab.reshape(HEADS, ATT_LEAD + 1, ATT_DIAG).transpose(1, 0, 2)


def _bias_sample(rel_bias, t, n_past):
    rel = np.arange(t)[:, None] + n_past - np.arange(n_past + t)[None, :]
    idx = np.clip(rel, -MAX_REL, MAX_REL) + MAX_REL
    tab = jnp.take(rel_bias, jnp.asarray(idx.reshape(-1), jnp.int32), axis=1).reshape(HEADS, t, n_past + t)
    return tab[:, :, :n_past], tab[:, :, n_past:]


def _layer(x, n_batch, t, attend, first_rows, prep_tm, wkv0, scan_tb, wts):
    m = n_batch * t
    x = _ffn(x, *wts["ffn1"])
    proj = _proj(x, *wts["proj"])
    att = attend(proj)

    prep = _prep(proj, 0, m, prep_tm, *first_rows(proj), wts["prep"])
    r, wdec, k2, v, kk, kka, g, cb = prep
    ops = (wdec, kka, k2, r, kk)
    if n_batch == LANE_GROUP_BATCH and scan_tb % 64 == 0:
        tok = lambda a: a.reshape(n_batch, t, WIDTH)
        y, s_t = _scan_tok([tok(a) for a in ops], tok(v), _state_to_scan(wkv0, n_batch), scan_tb)
        y = y.reshape(m, WIDTH)
    else:
        rows = [_rows_to_scan(a, n_batch, t) for a in ops]
        y, s_t = _scan(rows, _rows_to_scan(v, n_batch, t), _state_to_scan(wkv0, n_batch), scan_tb)
        y = _rows_from_scan(y, n_batch, t)

    x = _merge(att, y, g, cb, proj, x, *wts["merge"])
    x = _ffn(x, *wts["ffn2"])

    last = jnp.concatenate([proj[(b + 1) * t - 1:(b + 1) * t] for b in range(n_batch)])
    return x, proj, _state_from_scan(s_t, n_batch), _shift_row_from_proj(last)[:, None]


def kernel(x_prompt, x_sample, cache_k, cache_v, state_wkv, state_shift, norm_ffn1_pre, norm_ffn1_post, w_ffn1_in, w_ffn1_down, norm_mix_pre, norm_mix_post, w_in, rel_bias, w_att_out, rwkv_mu, rwkv_w0, rwkv_w_up, rwkv_a0, rwkv_a_up, rwkv_g_up, rwkv_k_k, rwkv_k_a, rwkv_r_k, rwkv_ln_w, rwkv_ln_b, w_rwkv_out, w_out, norm_ffn2_pre, norm_ffn2_post, w_ffn2_in, w_ffn2_down):
    depth = w_in.shape[0]
    batch, seq, _ = x_prompt.shape
    dec_batch, dec_seq, _ = x_sample.shape
    n_past = cache_k.shape[2]
    m_p, m_s = batch * seq, dec_batch * dec_seq
    keep = min(PAST_BAND, seq)
    assert n_past == PAST_BAND and seq % ATT_Q == 0 and seq // ATT_Q > ATT_LEAD
    assert m_p % FFN_TM == 0 and m_p % PROJ_TM == 0 and m_p % MERGE_TM == 0 and seq % PREP_TM == 0
    assert seq % SCAN_TB == 0 and batch == LANE_GROUP_BATCH and dec_batch % LANE_GROUP_BATCH == 0

    x_p = x_prompt.reshape(m_p, D_MODEL)
    x_s = x_sample.reshape(m_s, D_MODEL)
    ones_bd = jnp.asarray(np.kron(np.eye(SEG_HEADS), np.ones((HEAD_DIM, HEAD_DIM))), BF16)
    row = lambda v: v.reshape(1, -1)
    heads = lambda a, n, t: a.reshape(n, t, HEADS, HEAD_DIM)

    outs = {k: [] for k in ("kp", "vp", "wp", "sp", "ks", "vs", "ws", "ss")}
    for l in range(depth):
        w_proj = _repack_proj_weight(w_in[l].T)
        mu_rkv, mu_lora = _reorder_shift_cols(rwkv_mu[l][None])
        wts = {
            "ffn1": (row(norm_ffn1_pre[l]), row(norm_ffn1_post[l]),
                     w_ffn1_in[l], w_ffn1_down[l]),
            "proj": (row(norm_mix_pre[l]), w_proj),
            "prep": (mu_rkv, mu_lora, row(rwkv_w0[l]), row(rwkv_a0[l]), row(rwkv_k_k[l]), row(rwkv_k_a[l]),
                     row(rwkv_r_k[l]), row(rwkv_ln_b[l]),
                     *_split_bf16(_pad_rows(rwkv_w_up[l], LORA_PAD)),
                     *_split_bf16(_pad_rows(rwkv_a_up[l], LORA_PAD)),
                     *_split_bf16(rwkv_g_up[l]), ones_bd),
            "merge": (row(rwkv_ln_w[l]), row(norm_mix_post[l]), ones_bd,
                      w_att_out[l].astype(BF16), w_rwkv_out[l].astype(BF16), w_out[l].astype(BF16)),
            "ffn2": (row(norm_ffn2_pre[l]), row(norm_ffn2_post[l]),
                     w_ffn2_in[l], w_ffn2_down[l]),
        }

        diag = _bias_diagonals(rel_bias[l])
        starts_seq = (jnp.arange(m_p // PREP_TM) % (seq // PREP_TM) == 0)[:, None]

        def first_rows_prompt(proj):
            last_of_tile = proj.reshape(m_p // PREP_TM, PREP_TM, PROJ_COLS)[:, PREP_TM - 1]
            first = jnp.where(starts_seq, 0.0, jnp.roll(last_of_tile, 1, axis=0))[:, None]
            return first[..., COL_RKV:COL_RKV + 3 * WIDTH], first[..., COL_LORA:]

        x_p, proj_p, wkv_p, sh_p = _layer(
            x_p, batch, seq, lambda proj: _attn_prompt(proj, diag, batch, seq), first_rows_prompt, PREP_TM,
            jnp.zeros((batch, HEADS, HEAD_DIM, HEAD_DIM), F32), SCAN_TB, wts)
        tail = lambda col: jnp.stack([proj_p[(b + 1) * seq - keep:(b + 1) * seq, col:col + WIDTH]
                                      for b in range(batch)])
        outs["kp"].append(heads(tail(COL_K), batch, keep))
        outs["vp"].append(heads(tail(COL_V), batch, keep))
        outs["wp"].append(wkv_p.astype(state_wkv.dtype))
        outs["sp"].append(sh_p)

        chan_major = lambda c: c.transpose(0, 2, 3, 1).reshape(dec_batch, WIDTH, n_past)
        ck, cv = chan_major(cache_k[l]), chan_major(cache_v[l])
        bias_c, bias_n = _bias_sample(rel_bias[l], dec_seq, n_past)
        x_s, proj_s, wkv_s, sh_s = _layer(
            x_s, dec_batch, dec_seq,
            lambda proj: _attn_sample(proj, 0, ck, cv, bias_c, bias_n, dec_batch, dec_seq),
            lambda proj: _reorder_shift_cols(state_shift[l]), dec_seq,
            state_wkv[l].astype(F32), dec_seq, wts)
        outs["ks"].append(heads(proj_s[:, COL_K:COL_K + WIDTH], dec_batch, dec_seq))
        outs["vs"].append(heads(proj_s[:, COL_V:COL_V + WIDTH], dec_batch, dec_seq))
        outs["ws"].append(wkv_s.astype(state_wkv.dtype))
        outs["ss"].append(sh_s)

    st = lambda k: jnp.stack(outs[k])
    return (x_p.reshape(batch, seq, D_MODEL), x_s.reshape(dec_batch, dec_seq, D_MODEL),
            st("kp"), st("vp"), st("wp"), st("sp"), st("ks"), st("vs"), st("ws"), st("ss"))
```

```python
import functools

import numpy as np
import jax
import jax.numpy as jnp
from jax import lax
from jax.experimental import pallas as pl
from jax.experimental.pallas import tpu as pltpu

F32 = jnp.float32
BF16 = jnp.bfloat16

D_MODEL = 2048
CHUNK = 64
LEFT_CHUNKS = 8
PAST_BAND = LEFT_CHUNKS * CHUNK
HEAD_DIM = 64
HEADS = 16
WIDTH = HEADS * HEAD_DIM
DECAY_LORA = 96
ICLR_LORA = 96
GATE_LORA = 256
LORA_PAD = 128
LORA_COLS = 2 * LORA_PAD + GATE_LORA
D_FF = 5632
MAX_REL = 128
RMS_EPS = 1e-6
GN_EPS = 64e-5
NEG_INF = -1e30

COL_Q, COL_K, COL_V = 0, WIDTH, 2 * WIDTH
COL_RKV = 3 * WIDTH
COL_GA = 6 * WIDTH
COL_GB = COL_GA + D_MODEL
COL_LORA = COL_GB + D_MODEL
PROJ_COLS = COL_LORA + LORA_COLS

VMEM_LIMIT = 56 * 1024 * 1024
VMEM_LIMIT_MAX = 60 * 1024 * 1024

FFN_TM = 1024
FFN_TF = 256
PROJ_TM = 1024
PROJ_TN = 1536
REPACK_ROWS = 768
SEG_HEADS = 4
MERGE_TM = 256
PREP_TM = 256
SCAN_TB = 128
ATT_Q = 2 * CHUNK
ATT_KEYS = PAST_BAND + ATT_Q
ATT_LEAD = PAST_BAND // ATT_Q
ATT_DIAG = ATT_KEYS + ATT_Q
LANE_GROUP_BATCH = 4
KLO = HEAD_DIM // 2
STEP_PITCH = KLO + 8
XT_PITCH = HEAD_DIM + 8


def _sigmoid(x):
    return 1.0 / (1.0 + jnp.exp(-x))


def _split_bf16(x):
    hi = x.astype(BF16)
    lo = (x - hi.astype(F32)).astype(BF16)
    return hi, lo


def _dot(a, b):
    return jnp.dot(a, b, preferred_element_type=F32)


def _dot_f32(a, b_hi, b_lo):
    a_hi, a_lo = _split_bf16(a)
    return _dot(a_hi, b_hi) + _dot(a_lo, b_hi) + _dot(a_hi, b_lo)


def _segsum(x, ones_bd):
    hi, lo = _split_bf16(x)
    width = ones_bd.shape[0]
    outs = []
    for j in range(x.shape[-1] // width):
        sl = slice(j * width, (j + 1) * width)
        outs.append(_dot(hi[:, sl], ones_bd) + _dot(lo[:, sl], ones_bd))
    return jnp.concatenate(outs, axis=-1)


def _rms_scale(x):
    return lax.rsqrt(jnp.mean(x * x, axis=-1, keepdims=True) + RMS_EPS)


def _ffn_body(x_ref, gpre_ref, gpost_ref, wg_ref, wu_ref, wd_ref, o_ref, h_ref, *, n_f):
    f = pl.program_id(1)

    @pl.when(f == 0)
    def _():
        x = x_ref[...]
        h_ref[...] = (x * _rms_scale(x) * gpre_ref[...]).astype(BF16)
        o_ref[...] = jnp.zeros_like(o_ref)

    h = h_ref[...]
    g = _dot(h, wg_ref[...].astype(BF16))
    u = _dot(h, wu_ref[...].astype(BF16))
    act = (g * _sigmoid(g) * u).astype(BF16)
    o_ref[...] += _dot(act, wd_ref[...].astype(BF16))

    @pl.when(f == n_f - 1)
    def _():
        y = o_ref[...]
        o_ref[...] = x_ref[...] + 0.5 * (y * _rms_scale(y) * gpost_ref[...])


def _ffn(x, g_pre, g_post, w_in, w_down):
    m = x.shape[0]
    tm = min(FFN_TM, m)
    n_m, n_f = m // tm, D_FF // FFN_TF
    return pl.pallas_call(
        functools.partial(_ffn_body, n_f=n_f),
        grid=(n_m, n_f),
        in_specs=[
            pl.BlockSpec((tm, D_MODEL), lambda i, f: (i, 0)),
            pl.BlockSpec((1, D_MODEL), lambda i, f: (0, 0)),
            pl.BlockSpec((1, D_MODEL), lambda i, f: (0, 0)),
            pl.BlockSpec((D_MODEL, FFN_TF), lambda i, f: (0, f)),
            pl.BlockSpec((D_MODEL, FFN_TF), lambda i, f: (0, f + n_f)),
            pl.BlockSpec((FFN_TF, D_MODEL), lambda i, f: (f, 0)),
        ],
        out_specs=pl.BlockSpec((tm, D_MODEL), lambda i, f: (i, 0), pipeline_mode=pl.Buffered(1)),
        out_shape=jax.ShapeDtypeStruct((m, D_MODEL), F32),
        scratch_shapes=[pltpu.VMEM((tm, D_MODEL), BF16)],
        compiler_params=pltpu.CompilerParams(
            dimension_semantics=("parallel", "arbitrary"), vmem_limit_bytes=VMEM_LIMIT),
        name="ffn",
    )(x, g_pre, g_post, w_in, w_in, w_down)


def _repack_body(wt_hbm, o_ref, buf_ref, sem, *, n_head):
    j = pl.program_id(0)
    last = pl.num_programs(0) - 1
    o_ga = COL_GA + DECAY_LORA + ICLR_LORA + GATE_LORA

    def row_copy(src_row, n_rows, dst_row, slot):
        return pltpu.make_async_copy(wt_hbm.at[pl.ds(src_row, n_rows), :],
                                     buf_ref.at[pl.ds(dst_row, n_rows), :], sem.at[slot])

    @pl.when(j < last)
    def _():
        src = jnp.where(j < n_head, j * REPACK_ROWS, o_ga + (j - n_head) * REPACK_ROWS)
        copy = row_copy(src, REPACK_ROWS, 0, 0)
        copy.start()
        copy.wait()

    @pl.when(j == last)
    def _():
        n_gate = COL_LORA - (PROJ_COLS - REPACK_ROWS)
        pieces = [(o_ga + COL_LORA - COL_GA - n_gate, n_gate, 0),
                  (COL_GA, DECAY_LORA, n_gate),
                  (COL_GA + DECAY_LORA, ICLR_LORA, n_gate + LORA_PAD),
                  (COL_GA + DECAY_LORA + ICLR_LORA, GATE_LORA, n_gate + 2 * LORA_PAD)]
        for dst, width in ((n_gate, DECAY_LORA), (n_gate + LORA_PAD, ICLR_LORA)):
            buf_ref[dst + width:dst + LORA_PAD, :] = jnp.zeros((LORA_PAD - width, D_MODEL), F32)
        copies = [row_copy(src, n, dst, slot) for slot, (src, n, dst) in enumerate(pieces)]
        for copy in copies:
            copy.start()
        for copy in copies:
            copy.wait()

    o_ref[...] = buf_ref[...].astype(BF16)


def _repack_proj_weight(wt):
    assert COL_GA % REPACK_ROWS == 0 and PROJ_COLS % REPACK_ROWS == 0
    assert PROJ_COLS - REPACK_ROWS <= COL_LORA <= PROJ_COLS - LORA_COLS
    return pl.pallas_call(
        functools.partial(_repack_body, n_head=COL_GA // REPACK_ROWS),
        grid=(PROJ_COLS // REPACK_ROWS,),
        in_specs=[pl.BlockSpec(memory_space=pl.ANY)],
        out_specs=pl.BlockSpec((REPACK_ROWS, D_MODEL), lambda j: (j, 0)),
        out_shape=jax.ShapeDtypeStruct((PROJ_COLS, D_MODEL), BF16),
        scratch_shapes=[pltpu.VMEM((REPACK_ROWS, D_MODEL), F32), pltpu.SemaphoreType.DMA((4,))],
        compiler_params=pltpu.CompilerParams(
            dimension_semantics=("arbitrary",), vmem_limit_bytes=VMEM_LIMIT),
        name="repack_proj_weight",
    )(wt)


def _proj_body(x_ref, g_ref, w_ref, o_ref, h_ref):
    @pl.when(pl.program_id(1) == 0)
    def _():
        x = x_ref[...]
        h_ref[...] = (x * _rms_scale(x) * g_ref[...]).astype(BF16)

    o_ref[...] = _qk(h_ref[...], w_ref[...])


def _proj(x, g, w):
    m = x.shape[0]
    tm = min(PROJ_TM, m)
    return pl.pallas_call(
        _proj_body,
        grid=(m // tm, PROJ_COLS // PROJ_TN),
        in_specs=[
            pl.BlockSpec((tm, D_MODEL), lambda i, n: (i, 0)),
            pl.BlockSpec((1, D_MODEL), lambda i, n: (0, 0)),
            pl.BlockSpec((PROJ_TN, D_MODEL), lambda i, n: (n, 0)),
        ],
        out_specs=pl.BlockSpec((tm, PROJ_TN), lambda i, n: (i, n)),
        out_shape=jax.ShapeDtypeStruct((m, PROJ_COLS), F32),
        scratch_shapes=[pltpu.VMEM((tm, D_MODEL), BF16)],
        compiler_params=pltpu.CompilerParams(
            dimension_semantics=("parallel", "arbitrary"), vmem_limit_bytes=VMEM_LIMIT),
        name="proj",
    )(x, g, w)


def _qk(q, k):
    return lax.dot_general(q, k, (((1,), (1,)), ((), ())), preferred_element_type=F32)


def _attn_prompt_body(q_ref, k_ref, v_ref, diag_ref, o_ref, kbf_ref, vbf_ref, bias_ref):
    c = pl.program_id(1)
    shift = jnp.maximum(ATT_LEAD - c, 0)

    @pl.when(c == 0)
    def _():
        kbf_ref[...] = k_ref[...].astype(BF16)
        vbf_ref[...] = v_ref[...].T.astype(BF16)

    @pl.when((pl.program_id(0) == 0) & (c <= ATT_LEAD))
    def _():
        key = lax.broadcasted_iota(jnp.int32, (ATT_KEYS, ATT_Q), 0) // CHUNK
        qry = lax.broadcasted_iota(jnp.int32, (ATT_KEYS, ATT_Q), 1) // CHUNK
        dist = (LEFT_CHUNKS + qry) - (key + shift * (ATT_Q // CHUNK))
        in_band = (dist >= 0) & (dist <= LEFT_CHUNKS)
        for h in range(HEADS):
            diag = jnp.broadcast_to(diag_ref[shift, pl.ds(h, 1), :], (ATT_Q, ATT_DIAG))
            toep = pltpu.roll(diag, ATT_DIAG - ATT_Q, axis=1, stride=1, stride_axis=0)
            bias_ref[shift, h] = jnp.where(in_band, toep[:, :ATT_KEYS].T, NEG_INF)

    start = pl.multiple_of(jnp.maximum(c - ATT_LEAD, 0) * ATT_Q, ATT_Q)
    lane = lax.broadcasted_iota(jnp.int32, (ATT_Q, 128), 1)
    for pair in range(HEADS // 2):
        sl = slice(pair * 128, (pair + 1) * 128)
        kp = kbf_ref[pl.ds(start, ATT_KEYS), sl]
        vp = vbf_ref[sl, pl.ds(start, ATT_KEYS)]
        qp = q_ref[:, sl] * (HEAD_DIM ** -0.5)
        halves = []
        for half in range(2):
            qm = jnp.where((lane >= HEAD_DIM) == bool(half), qp, 0.0).astype(BF16)
            s = _qk(kp, qm) + bias_ref[shift, 2 * pair + half]
            p = jnp.exp(s - jnp.max(s, axis=0, keepdims=True))
            p = (p * (1.0 / jnp.sum(p, axis=0, keepdims=True))).astype(BF16)
            halves.append(_dot(vp, p))
        chan = lax.broadcasted_iota(jnp.int32, (128, ATT_Q), 0)
        o_ref[:, sl] = jnp.where(chan < HEAD_DIM, halves[0], halves[1]).T.astype(BF16)


def _attn_prompt(proj, diag, batch, seq):
    n_c = seq // ATT_Q
    once = dict(pipeline_mode=pl.Buffered(1))
    return pl.pallas_call(
        _attn_prompt_body,
        grid=(batch, n_c),
        in_specs=[
            pl.BlockSpec((ATT_Q, WIDTH), lambda b, c: (b * n_c + c, COL_Q // WIDTH)),
            pl.BlockSpec((seq, WIDTH), lambda b, c: (b, COL_K // WIDTH), **once),
            pl.BlockSpec((seq, WIDTH), lambda b, c: (b, COL_V // WIDTH), **once),
            pl.BlockSpec(diag.shape, lambda b, c: (0, 0, 0)),
        ],
        out_specs=pl.BlockSpec((ATT_Q, WIDTH), lambda b, c: (b * n_c + c, 0)),
        out_shape=jax.ShapeDtypeStruct((batch * seq, WIDTH), BF16),
        scratch_shapes=[pltpu.VMEM((seq, WIDTH), BF16), pltpu.VMEM((WIDTH, seq), BF16),
                        pltpu.VMEM((ATT_LEAD + 1, HEADS, ATT_KEYS, ATT_Q), F32)],
        compiler_params=pltpu.CompilerParams(
            dimension_semantics=("arbitrary", "arbitrary"), vmem_limit_bytes=VMEM_LIMIT_MAX),
        name="attn_prompt",
    )(proj, proj, proj, diag)


def _attn_sample_body(q_ref, kn_ref, vn_ref, ck_ref, cv_ref, bias_c_ref, bias_n_ref, o_ref):
    t = q_ref.shape[0]
    lane = lax.broadcasted_iota(jnp.int32, (t, 128), 1)
    slabs = [slice(pair * 128, (pair + 1) * 128) for pair in range(HEADS // 2)]

    s_c, s_n = [], []
    for h in range(HEADS):
        sl = slabs[h // 2]
        qm = jnp.where((lane >= HEAD_DIM) == bool(h % 2), q_ref[:, sl] * (HEAD_DIM ** -0.5), 0.0).astype(BF16)
        s_c.append(_dot(qm, ck_ref[0, sl, :].astype(BF16)) + bias_c_ref[h])
        s_n.append(_qk(qm, kn_ref[:, sl].astype(BF16)) + bias_n_ref[h])
    s_c = jnp.concatenate(s_c, axis=0)
    s_n = jnp.concatenate(s_n, axis=0)
    m = jnp.maximum(jnp.max(s_c, axis=-1, keepdims=True), jnp.max(s_n, axis=-1, keepdims=True))
    p_c = jnp.exp(s_c - m)
    p_n = jnp.exp(s_n - m)
    inv = 1.0 / (jnp.sum(p_c, axis=-1, keepdims=True) + jnp.sum(p_n, axis=-1, keepdims=True))
    p_c = (p_c * inv).astype(BF16)
    p_n = (p_n * inv).astype(BF16)
    for pair, sl in enumerate(slabs):
        cv = cv_ref[0, sl, :].astype(BF16)
        vn = vn_ref[:, sl].astype(BF16)
        halves = []
        for h in (2 * pair, 2 * pair + 1):
            rows = slice(h * t, (h + 1) * t)
            halves.append(_qk(p_c[rows], cv) + _dot(p_n[rows], vn))
        o_ref[:, sl] = jnp.where(lane < HEAD_DIM, halves[0], halves[1]).astype(BF16)


def _attn_sample(proj, row0, cache_k, cache_v, bias_c, bias_n, batch, t):
    n_past = cache_k.shape[2]
    blk0 = row0 // t
    return pl.pallas_call(
        _attn_sample_body,
        grid=(batch,),
        in_specs=[
            pl.BlockSpec((t, WIDTH), lambda b: (blk0 + b, COL_Q // WIDTH)),
            pl.BlockSpec((t, WIDTH), lambda b: (blk0 + b, COL_K // WIDTH)),
            pl.BlockSpec((t, WIDTH), lambda b: (blk0 + b, COL_V // WIDTH)),
            pl.BlockSpec((1, WIDTH, n_past), lambda b: (b, 0, 0)),
            pl.BlockSpec((1, WIDTH, n_past), lambda b: (b, 0, 0)),
            pl.BlockSpec((HEADS, t, n_past), lambda b: (0, 0, 0)),
            pl.BlockSpec((HEADS, t, t), lambda b: (0, 0, 0)),
        ],
        out_specs=pl.BlockSpec((t, WIDTH), lambda b: (b, 0)),
        out_shape=jax.ShapeDtypeStruct((batch * t, WIDTH), BF16),
        compiler_params=pltpu.CompilerParams(
            dimension_semantics=("parallel",), vmem_limit_bytes=VMEM_LIMIT),
        name="attn_sample",
    )(proj, proj, proj, cache_k, cache_v, bias_c, bias_n)


def _prep_body(p_ref, pl_ref, bp_ref, bl_ref, mu_ref, mul_ref, w0_ref, a0_ref, kk_ref, ka_ref, rk_ref,
               lnb_ref, wup_hi, wup_lo, aup_hi, aup_lo, gup_hi, gup_lo, bd_ref,
               r_o, w_o, k_o, v_o, kk_o, kka_o, g_o, cb_o):
    def shifted(p, first_row, mu):
        row = lax.broadcasted_iota(jnp.int32, p.shape, 0)
        prev = jnp.where(row == 0, first_row, pltpu.roll(p, 1, axis=0))
        return p + (prev - p) * mu

    xm = shifted(p_ref[...], bp_ref[0], mu_ref[...])
    xl = shifted(pl_ref[...], bl_ref[0], mul_ref[...])
    r = xm[:, 0:WIDTH]
    k = xm[:, WIDTH:2 * WIDTH]
    v = xm[:, 2 * WIDTH:3 * WIDTH]
    wd = xl[:, 0:LORA_PAD]
    ad = xl[:, LORA_PAD:2 * LORA_PAD]
    gd = xl[:, 2 * LORA_PAD:]

    z = -(w0_ref[...] + _dot_f32(jnp.tanh(wd), wup_hi[...], wup_lo[...]))
    softplus = jnp.maximum(z, 0.0) + jnp.log(1.0 + jnp.exp(-jnp.abs(z)))
    decay = jnp.exp(-jnp.exp(-softplus - 0.5))
    a = _sigmoid(a0_ref[...] + _dot_f32(ad, aup_hi[...], aup_lo[...]))
    g = _dot_f32(_sigmoid(gd), gup_hi[...], gup_lo[...])

    bd = bd_ref[...]
    kk = k * kk_ref[...]
    kk = kk / jnp.maximum(jnp.sqrt(_segsum(kk * kk, bd)), 1e-12)
    k2 = k * (1.0 + (a - 1.0) * ka_ref[...])
    bonus = _segsum(r * k2 * rk_ref[...], bd) * v

    r_o[...] = r
    w_o[...] = decay
    k_o[...] = k2
    v_o[...] = v
    kk_o[...] = kk
    kka_o[...] = kk * a
    g_o[...] = g
    cb_o[...] = lnb_ref[...] + bonus


def _prep(proj, row0, n_rows, tm, first_rkv, first_lora, params):
    n_t = n_rows // tm
    blk0 = row0 // tm
    vec = lambda w: pl.BlockSpec((1, w), lambda i: (0, 0))
    mat = lambda r: pl.BlockSpec((r, WIDTH), lambda i: (0, 0))
    out = pl.BlockSpec((tm, WIDTH), lambda i: (i, 0))
    return pl.pallas_call(
        _prep_body,
        grid=(n_t,),
        in_specs=[
            pl.BlockSpec((tm, 3 * WIDTH), lambda i: (blk0 + i, COL_RKV // (3 * WIDTH))),
            pl.BlockSpec((tm, LORA_COLS), lambda i: (blk0 + i, COL_LORA // LORA_COLS)),
            pl.BlockSpec((1, 1, 3 * WIDTH), lambda i: (i, 0, 0)),
            pl.BlockSpec((1, 1, LORA_COLS), lambda i: (i, 0, 0)),
            vec(3 * WIDTH), vec(LORA_COLS), vec(WIDTH), vec(WIDTH), vec(WIDTH), vec(WIDTH), vec(WIDTH),
            vec(WIDTH), mat(LORA_PAD), mat(LORA_PAD), mat(LORA_PAD), mat(LORA_PAD),
            mat(GATE_LORA), mat(GATE_LORA),
            pl.BlockSpec((SEG_HEADS * HEAD_DIM, SEG_HEADS * HEAD_DIM), lambda i: (0, 0)),
        ],
        out_specs=[out] * 8,
        out_shape=[jax.ShapeDtypeStruct((n_rows, WIDTH), F32)] * 8,
        compiler_params=pltpu.CompilerParams(
            dimension_semantics=("parallel",), vmem_limit_bytes=VMEM_LIMIT),
        name="rwkv_prep",
    )(proj, proj, first_rkv, first_lora, *params)


def _scan_steps(s_ref, row, get_v, put_y, n_steps):
    W, KKA, K, R, KK = range(5)
    zero = jnp.zeros((KLO, 128), F32)

    sa0 = [zero, zero]
    for k in range(HEAD_DIM):
        sa0[k % 2] = sa0[k % 2] - s_ref[k] * row(KK, 0, k)

    def step(t, sa):
        v = get_v(t)
        t_next = jnp.minimum(t + 1, n_steps - 1)
        y = [zero, zero]
        sa_next = [zero, zero]
        for k in range(HEAD_DIM):
            s_new = s_ref[k] * row(W, t, k) + sa * row(KKA, t, k) + v * row(K, t, k)
            s_ref[k] = s_new
            y[k % 2] = y[k % 2] + s_new * row(R, t, k)
            sa_next[k % 2] = sa_next[k % 2] - s_new * row(KK, t_next, k)
        put_y(t, y[0] + y[1])
        return sa_next[0] + sa_next[1]

    lax.fori_loop(0, n_steps, step, sa0[0] + sa0[1])


def _scan_body(w_ref, kka_ref, k_ref, r_ref, kk_ref, v_ref, s0_ref, y_ref, st_ref, s_ref, rows_ref, *, tb):
    @pl.when(pl.program_id(1) == 0)
    def _():
        s_ref[...] = s0_ref[0]

    lane = lax.broadcasted_iota(jnp.int32, (tb * KLO, 128), 1)
    for a, ref in enumerate((w_ref, kka_ref, k_ref, r_ref, kk_ref)):
        x = ref[0].reshape(tb * KLO, 128)
        swapped = pltpu.roll(x, HEAD_DIM, axis=1)
        rows_ref[a, :, 0:KLO, :] = jnp.where(lane < HEAD_DIM, x, swapped).reshape(tb, KLO, 128)
        rows_ref[a, :, KLO:, :] = jnp.where(lane < HEAD_DIM, swapped, x).reshape(tb, KLO, 128)

    def row(a, t, k):
        return jnp.broadcast_to(rows_ref[a, t, pl.ds(k, 1), :], (KLO, 128))

    def put_y(t, y):
        y_ref[0, t] = y

    _scan_steps(s_ref, row, lambda t: v_ref[0, t], put_y, tb)

    @pl.when(pl.program_id(1) == pl.num_programs(1) - 1)
    def _():
        st_ref[0] = s_ref[...]


def _scan_tok_body(w_hbm, kka_hbm, k_hbm, r_hbm, kk_hbm, v_hbm, s0_ref, y_ref, st_ref,
                   s_ref, rows_ref, xt_ref, vt_ref, yt_ref, in_ref, in_sem, *, tb):
    step = pl.program_id(0)
    operands = (w_hbm, kka_hbm, k_hbm, r_hbm, kk_hbm, v_hbm)

    def block_copy(a, blk):
        return pltpu.make_async_copy(operands[a].at[:, pl.ds(blk * tb, tb), :], in_ref.at[a], in_sem.at[a])

    @pl.when(step == 0)
    def _():
        for a in range(len(operands)):
            block_copy(a, 0).start()
        s_ref[...] = s0_ref[0]
        xt_ref[...] = jnp.zeros_like(xt_ref)

    def head_rows(c):
        return pl.ds(c, HEADS, stride=XT_PITCH)

    def channel_major(ref):
        for b in range(LANE_GROUP_BATCH):
            xt = ref[b].T
            for h in range(HEADS):
                xt_ref[b, h * XT_PITCH:h * XT_PITCH + HEAD_DIM, 0:tb] = xt[h * HEAD_DIM:(h + 1) * HEAD_DIM]

    def lane_tile(chan):
        parts = [xt_ref[b, head_rows(c), :] for c in chan for b in range(LANE_GROUP_BATCH)]
        return jnp.concatenate(parts, axis=0).T[0:tb]

    for a in range(5):
        block_copy(a, step).wait()
        channel_major(in_ref.at[a])
        for k in range(HEAD_DIM):
            rows_ref[a, k] = lane_tile((k, k))
    block_copy(5, step).wait()
    channel_major(in_ref.at[5])
    for j in range(KLO):
        vt_ref[pl.ds(j, tb, stride=STEP_PITCH), :] = lane_tile((j, KLO + j))

    @pl.when(step + 1 < pl.num_programs(0))
    def _():
        for a in range(len(operands)):
            block_copy(a, step + 1).start()

    def row(a, t, k):
        return jnp.broadcast_to(rows_ref[a, k, pl.ds(t, 1), :], (KLO, 128))

    def step_rows(t):
        return pl.ds(pl.multiple_of(t * STEP_PITCH, 8), KLO)

    def put_y(t, y):
        yt_ref[step_rows(t), :] = y

    _scan_steps(s_ref, row, lambda t: vt_ref[step_rows(t), :], put_y, tb)

    for j in range(KLO):
        steps = yt_ref[pl.ds(j, tb, stride=STEP_PITCH), :]
        if tb < 128:
            steps = jnp.concatenate([steps, jnp.zeros((128 - tb, 128), F32)], axis=0)
        tile = steps.T
        for half in range(2):
            for b in range(LANE_GROUP_BATCH):
                r0 = (half * LANE_GROUP_BATCH + b) * HEADS
                xt_ref[b, head_rows(half * KLO + j), :] = tile[r0:r0 + HEADS]
    for b in range(LANE_GROUP_BATCH):
        heads = [xt_ref[b, h * XT_PITCH:h * XT_PITCH + HEAD_DIM, 0:tb] for h in range(HEADS)]
        y_ref[b] = jnp.concatenate(heads, axis=0).T

    @pl.when(pl.program_id(0) == pl.num_programs(0) - 1)
    def _():
        st_ref[0] = s_ref[...]


def _scan_tok(ops, v, s0, tb):
    nb, t, _ = v.shape
    tok_spec = pl.BlockSpec((nb, tb, WIDTH), lambda j: (0, j, 0))
    st_spec = pl.BlockSpec((1, HEAD_DIM, KLO, 128), lambda j: (0, 0, 0, 0))
    return pl.pallas_call(
        functools.partial(_scan_tok_body, tb=tb),
        grid=(t // tb,),
        in_specs=[pl.BlockSpec(memory_space=pl.ANY)] * 6 + [st_spec],
        out_specs=[tok_spec, st_spec],
        out_shape=[jax.ShapeDtypeStruct((nb, t, WIDTH), F32),
                   jax.ShapeDtypeStruct((1, HEAD_DIM, KLO, 128), F32)],
        scratch_shapes=[pltpu.VMEM((HEAD_DIM, KLO, 128), F32),
                        pltpu.VMEM((5, HEAD_DIM, tb, 128), F32),
                        pltpu.VMEM((nb, HEADS * XT_PITCH, 128), F32),
                        pltpu.VMEM((tb * STEP_PITCH, 128), F32),
                        pltpu.VMEM((tb * STEP_PITCH, 128), F32),
                        pltpu.VMEM((6, nb, tb, WIDTH), F32),
                        pltpu.SemaphoreType.DMA((6,))],
        compiler_params=pltpu.CompilerParams(
            dimension_semantics=("arbitrary",), vmem_limit_bytes=VMEM_LIMIT),
        name="rwkv_scan_tok",
    )(*ops, v, s0)


def _scan(rows, v, s0, tb):
    g, t = v.shape[0], v.shape[1]
    row_spec = pl.BlockSpec((1, tb, KLO, 128), lambda i, j: (i, j, 0, 0))
    st_spec = pl.BlockSpec((1, HEAD_DIM, KLO, 128), lambda i, j: (i, 0, 0, 0))
    return pl.pallas_call(
        functools.partial(_scan_body, tb=tb),
        grid=(g, t // tb),
        in_specs=[row_spec] * 6 + [st_spec],
        out_specs=[row_spec, st_spec],
        out_shape=[jax.ShapeDtypeStruct((g, t, KLO, 128), F32),
                   jax.ShapeDtypeStruct((g, HEAD_DIM, KLO, 128), F32)],
        scratch_shapes=[pltpu.VMEM((HEAD_DIM, KLO, 128), F32),
                        pltpu.VMEM((5, tb, HEAD_DIM, 128), F32)],
        compiler_params=pltpu.CompilerParams(
            dimension_semantics=("parallel", "arbitrary"), vmem_limit_bytes=VMEM_LIMIT),
        name="rwkv_scan",
    )(*rows, v, s0)


def _rows_to_scan(x, n_batch, t):
    g = n_batch // LANE_GROUP_BATCH
    x = x.reshape(g, LANE_GROUP_BATCH, t, HEADS, 2, KLO)
    return x.transpose(0, 2, 5, 4, 1, 3).reshape(g, t, KLO, 128)


def _rows_from_scan(y, n_batch, t):
    g = n_batch // LANE_GROUP_BATCH
    y = y.reshape(g, t, KLO, 2, LANE_GROUP_BATCH, HEADS).transpose(0, 4, 1, 5, 3, 2)
    return y.reshape(n_batch * t, WIDTH)


def _state_to_scan(s, n_batch):
    g = n_batch // LANE_GROUP_BATCH
    s = s.reshape(g, LANE_GROUP_BATCH, HEADS, 2, KLO, HEAD_DIM)
    return s.transpose(0, 5, 4, 3, 1, 2).reshape(g, HEAD_DIM, KLO, 128)


def _state_from_scan(s, n_batch):
    g = n_batch // LANE_GROUP_BATCH
    s = s.reshape(g, HEAD_DIM, KLO, 2, LANE_GROUP_BATCH, HEADS)
    return s.transpose(0, 4, 5, 3, 2, 1).reshape(n_batch, HEADS, HEAD_DIM, HEAD_DIM)


def _merge_body(att_ref, y_ref, g_ref, cb_ref, ga_ref, gb_ref, x_ref, lnw_ref, gpost_ref, bd_ref,
                wa_ref, wr_ref, wo_ref, o_ref):
    bd = bd_ref[...]
    y = y_ref[...]
    d = y - _segsum(y, bd) * (1.0 / HEAD_DIM)
    var = _segsum(d * d, bd) * (1.0 / HEAD_DIM)
    rw = ((d * lax.rsqrt(var + GN_EPS) * lnw_ref[...] + cb_ref[...]) * g_ref[...]).astype(BF16)
    rw_o = _dot(rw, wr_ref[...])
    att_o = _dot(att_ref[...], wa_ref[...])
    mixed = (_sigmoid(ga_ref[...]) * att_o + _sigmoid(gb_ref[...]) * rw_o).astype(BF16)
    z = _dot(mixed, wo_ref[...])
    o_ref[...] = x_ref[...] + z * _rms_scale(z) * gpost_ref[...]


def _merge(att, y, g, cb, proj, x, ln_w, g_post, ones_bd, w_att, w_rwkv, w_out):
    m = x.shape[0]
    tm = min(MERGE_TM, m)
    tok = lambda w: pl.BlockSpec((tm, w), lambda i: (i, 0))
    const = lambda a: pl.BlockSpec(a.shape, lambda i: (0,) * a.ndim)
    return pl.pallas_call(
        _merge_body,
        grid=(m // tm,),
        in_specs=[
            tok(WIDTH), tok(WIDTH), tok(WIDTH), tok(WIDTH),
            pl.BlockSpec((tm, D_MODEL), lambda i: (i, COL_GA // D_MODEL)),
            pl.BlockSpec((tm, D_MODEL), lambda i: (i, COL_GB // D_MODEL)),
            tok(D_MODEL), const(ln_w), const(g_post), const(ones_bd),
            const(w_att), const(w_rwkv), const(w_out),
        ],
        out_specs=tok(D_MODEL),
        out_shape=jax.ShapeDtypeStruct((m, D_MODEL), F32),
        compiler_params=pltpu.CompilerParams(
            dimension_semantics=("parallel",), vmem_limit_bytes=VMEM_LIMIT),
        name="merge",
    )(att, y, g, cb, proj, proj, x, ln_w, g_post, ones_bd, w_att, w_rwkv, w_out)


def _pad_rows(x, height):
    return jnp.pad(x, ((0, height - x.shape[0]), (0, 0)))


def _reorder_shift_cols(x):
    o = 3 * WIDTH
    wd = x[..., o:o + DECAY_LORA]
    ad = x[..., o + DECAY_LORA:o + DECAY_LORA + ICLR_LORA]
    gd = x[..., o + DECAY_LORA + ICLR_LORA:]
    pad = [(0, 0)] * (x.ndim - 1)
    lora = jnp.concatenate([jnp.pad(wd, pad + [(0, LORA_PAD - DECAY_LORA)]),
                            jnp.pad(ad, pad + [(0, LORA_PAD - ICLR_LORA)]), gd], axis=-1)
    return x[..., :o], lora


def _shift_row_from_proj(rows):
    lo = rows[:, COL_LORA:]
    return jnp.concatenate([rows[:, COL_RKV:COL_RKV + 3 * WIDTH], lo[:, :DECAY_LORA],
                            lo[:, LORA_PAD:LORA_PAD + ICLR_LORA], lo[:, 2 * LORA_PAD:]], axis=-1)


def _bias_diagonals(rel_bias):
    u = np.arange(ATT_DIAG)[None, :]
    s = np.arange(ATT_LEAD + 1)[:, None]
    idx = np.clip(ATT_KEYS - u - ATT_Q * s, -MAX_REL, MAX_REL) + MAX_REL
    tab = jnp.take(rel_bias, jnp.asarray(idx.reshape(-1), jnp.int32), axis=1)
    return tab.reshape(HEADS, ATT_LEAD + 1, ATT_DIAG).transpose(1, 0, 2)


def _bias_sample(rel_bias, t, n_past):
    rel = np.arange(t)[:, None] + n_past - np.arange(n_past + t)[None, :]
    idx = np.clip(rel, -MAX_REL, MAX_REL) + MAX_REL
    tab = jnp.take(rel_bias, jnp.asarray(idx.reshape(-1), jnp.int32), axis=1).reshape(HEADS, t, n_past + t)
    return tab[:, :, :n_past], tab[:, :, n_past:]


def _layer(x, n_batch, t, attend, first_rows, prep_tm, wkv0, scan_tb, wts):
    m = n_batch * t
    x = _ffn(x, *wts["ffn1"])
    proj = _proj(x, *wts["proj"])
    att = attend(proj)

    prep = _prep(proj, 0, m, prep_tm, *first_rows(proj), wts["prep"])
    r, wdec, k2, v, kk, kka, g, cb = prep
    ops = (wdec, kka, k2, r, kk)
    if n_batch == LANE_GROUP_BATCH and scan_tb % 64 == 0:
        tok = lambda a: a.reshape(n_batch, t, WIDTH)
        y, s_t = _scan_tok([tok(a) for a in ops], tok(v), _state_to_scan(wkv0, n_batch), scan_tb)
        y = y.reshape(m, WIDTH)
    else:
        rows = [_rows_to_scan(a, n_batch, t) for a in ops]
        y, s_t = _scan(rows, _rows_to_scan(v, n_batch, t), _state_to_scan(wkv0, n_batch), scan_tb)
        y = _rows_from_scan(y, n_batch, t)

    x = _merge(att, y, g, cb, proj, x, *wts["merge"])
    x = _ffn(x, *wts["ffn2"])

    last = jnp.concatenate([proj[(b + 1) * t - 1:(b + 1) * t] for b in range(n_batch)])
    return x, proj, _state_from_scan(s_t, n_batch), _shift_row_from_proj(last)[:, None]


def kernel(x_prompt, x_sample, cache_k, cache_v, state_wkv, state_shift, norm_ffn1_pre, norm_ffn1_post, w_ffn1_in, w_ffn1_down, norm_mix_pre, norm_mix_post, w_in, rel_bias, w_att_out, rwkv_mu, rwkv_w0, rwkv_w_up, rwkv_a0, rwkv_a_up, rwkv_g_up, rwkv_k_k, rwkv_k_a, rwkv_r_k, rwkv_ln_w, rwkv_ln_b, w_rwkv_out, w_out, norm_ffn2_pre, norm_ffn2_post, w_ffn2_in, w_ffn2_down):
    depth = w_in.shape[0]
    batch, seq, _ = x_prompt.shape
    dec_batch, dec_seq, _ = x_sample.shape
    n_past = cache_k.shape[2]
    m_p, m_s = batch * seq, dec_batch * dec_seq
    keep = min(PAST_BAND, seq)
    assert n_past == PAST_BAND and seq % ATT_Q == 0 and seq // ATT_Q > ATT_LEAD
    assert m_p % FFN_TM == 0 and m_p % PROJ_TM == 0 and m_p % MERGE_TM == 0 and seq % PREP_TM == 0
    assert seq % SCAN_TB == 0 and batch == LANE_GROUP_BATCH and dec_batch % LANE_GROUP_BATCH == 0

    x_p = x_prompt.reshape(m_p, D_MODEL)
    x_s = x_sample.reshape(m_s, D_MODEL)
    ones_bd = jnp.asarray(np.kron(np.eye(SEG_HEADS), np.ones((HEAD_DIM, HEAD_DIM))), BF16)
    row = lambda v: v.reshape(1, -1)
    heads = lambda a, n, t: a.reshape(n, t, HEADS, HEAD_DIM)

    outs = {k: [] for k in ("kp", "vp", "wp", "sp", "ks", "vs", "ws", "ss")}
    for l in range(depth):
        w_proj = _repack_proj_weight(w_in[l].T)
        mu_rkv, mu_lora = _reorder_shift_cols(rwkv_mu[l][None])
        wts = {
            "ffn1": (row(norm_ffn1_pre[l]), row(norm_ffn1_post[l]),
                     w_ffn1_in[l], w_ffn1_down[l]),
            "proj": (row(norm_mix_pre[l]), w_proj),
            "prep": (mu_rkv, mu_lora, row(rwkv_w0[l]), row(rwkv_a0[l]), row(rwkv_k_k[l]), row(rwkv_k_a[l]),
                     row(rwkv_r_k[l]), row(rwkv_ln_b[l]),
                     *_split_bf16(_pad_rows(rwkv_w_up[l], LORA_PAD)),
                     *_split_bf16(_pad_rows(rwkv_a_up[l], LORA_PAD)),
                     *_split_bf16(rwkv_g_up[l]), ones_bd),
            "merge": (row(rwkv_ln_w[l]), row(norm_mix_post[l]), ones_bd,
                      w_att_out[l].astype(BF16), w_rwkv_out[l].astype(BF16), w_out[l].astype(BF16)),
            "ffn2": (row(norm_ffn2_pre[l]), row(norm_ffn2_post[l]),
                     w_ffn2_in[l], w_ffn2_down[l]),
        }

        diag = _bias_diagonals(rel_bias[l])
        starts_seq = (jnp.arange(m_p // PREP_TM) % (seq // PREP_TM) == 0)[:, None]

        def first_rows_prompt(proj):
            last_of_tile = proj.reshape(m_p // PREP_TM, PREP_TM, PROJ_COLS)[:, PREP_TM - 1]
            first = jnp.where(starts_seq, 0.0, jnp.roll(last_of_tile, 1, axis=0))[:, None]
            return first[..., COL_RKV:COL_RKV + 3 * WIDTH], first[..., COL_LORA:]

        x_p, proj_p, wkv_p, sh_p = _layer(
            x_p, batch, seq, lambda proj: _attn_prompt(proj, diag, batch, seq), first_rows_prompt, PREP_TM,
            jnp.zeros((batch, HEADS, HEAD_DIM, HEAD_DIM), F32), SCAN_TB, wts)
        tail = lambda col: jnp.stack([proj_p[(b + 1) * seq - keep:(b + 1) * seq, col:col + WIDTH]
                                      for b in range(batch)])
        outs["kp"].append(heads(tail(COL_K), batch, keep))
        outs["vp"].append(heads(tail(COL_V), batch, keep))
        outs["wp"].append(wkv_p.astype(state_wkv.dtype))
        outs["sp"].append(sh_p)

        chan_major = lambda c: c.transpose(0, 2, 3, 1).reshape(dec_batch, WIDTH, n_past)
        ck, cv = chan_major(cache_k[l]), chan_major(cache_v[l])
        bias_c, bias_n = _bias_sample(rel_bias[l], dec_seq, n_past)
        x_s, proj_s, wkv_s, sh_s = _layer(
            x_s, dec_batch, dec_seq,
            lambda proj: _attn_sample(proj, 0, ck, cv, bias_c, bias_n, dec_batch, dec_seq),
            lambda proj: _reorder_shift_cols(state_shift[l]), dec_seq,
            state_wkv[l].astype(F32), dec_seq, wts)
        outs["ks"].append(heads(proj_s[:, COL_K:COL_K + WIDTH], dec_batch, dec_seq))
        outs["vs"].append(heads(proj_s[:, COL_V:COL_V + WIDTH], dec_batch, dec_seq))
        outs["ws"].append(wkv_s.astype(state_wkv.dtype))
        outs["ss"].append(sh_s)

    st = lambda k: jnp.stack(outs[k])
    return (x_p.reshape(batch, seq, D_MODEL), x_s.reshape(dec_batch, dec_seq, D_MODEL),
            st("kp"), st("vp"), st("wp"), st("sp"), st("ks"), st("vs"), st("ws"), st("ss"))
```

```python
import functools

import numpy as np
import jax
import jax.numpy as jnp
from jax import lax
from jax.experimental import pallas as pl
from jax.experimental.pallas import tpu as pltpu

F32 = jnp.float32
BF16 = jnp.bfloat16

D_MODEL = 2048
CHUNK = 64
LEFT_CHUNKS = 8
PAST_BAND = LEFT_CHUNKS * CHUNK
HEAD_DIM = 64
HEADS = 16
WIDTH = HEADS * HEAD_DIM
DECAY_LORA = 96
ICLR_LORA = 96
GATE_LORA = 256
LORA_PAD = 128
LORA_COLS = 2 * LORA_PAD + GATE_LORA
D_FF = 5632
MAX_REL = 128
RMS_EPS = 1e-6
GN_EPS = 64e-5
NEG_INF = -1e30

COL_Q, COL_K, COL_V = 0, WIDTH, 2 * WIDTH
COL_RKV = 3 * WIDTH
COL_GA = 6 * WIDTH
COL_GB = COL_GA + D_MODEL
COL_LORA = COL_GB + D_MODEL
PROJ_COLS = COL_LORA + LORA_COLS

VMEM_LIMIT = 56 * 1024 * 1024
VMEM_LIMIT_MAX = 60 * 1024 * 1024

FFN_TM = 1024
FFN_TF = 256
PROJ_TM = 1024
PROJ_TN = 1536
REPACK_ROWS = 768
SEG_HEADS = 4
MERGE_TM = 256
PREP_TM = 256
SCAN_TB = 128
ATT_Q = 2 * CHUNK
ATT_KEYS = PAST_BAND + ATT_Q
ATT_LEAD = PAST_BAND // ATT_Q
ATT_DIAG = ATT_KEYS + ATT_Q
LANE_GROUP_BATCH = 4
KLO = HEAD_DIM // 2
STEP_PITCH = KLO + 8
XT_PITCH = HEAD_DIM + 8


def _sigmoid(x):
    return 1.0 / (1.0 + jnp.exp(-x))


def _split_bf16(x):
    hi = x.astype(BF16)
    lo = (x - hi.astype(F32)).astype(BF16)
    return hi, lo


def _dot(a, b):
    return jnp.dot(a, b, preferred_element_type=F32)


def _dot_f32(a, b_hi, b_lo):
    a_hi, a_lo = _split_bf16(a)
    return _dot(a_hi, b_hi) + _dot(a_lo, b_hi) + _dot(a_hi, b_lo)


def _segsum(x, ones_bd):
    hi, lo = _split_bf16(x)
    width = ones_bd.shape[0]
    outs = []
    for j in range(x.shape[-1] // width):
        sl = slice(j * width, (j + 1) * width)
        outs.append(_dot(hi[:, sl], ones_bd) + _dot(lo[:, sl], ones_bd))
    return jnp.concatenate(outs, axis=-1)


def _rms_scale(x):
    return lax.rsqrt(jnp.mean(x * x, axis=-1, keepdims=True) + RMS_EPS)


def _ffn_body(x_ref, gpre_ref, gpost_ref, wg_ref, wu_ref, wd_ref, o_ref, h_ref, *, n_f):
    f = pl.program_id(1)

    @pl.when(f == 0)
    def _():
        x = x_ref[...]
        h_ref[...] = (x * _rms_scale(x) * gpre_ref[...]).astype(BF16)
        o_ref[...] = jnp.zeros_like(o_ref)

    h = h_ref[...]
    g = _dot(h, wg_ref[...].astype(BF16))
    u = _dot(h, wu_ref[...].astype(BF16))
    act = (g * _sigmoid(g) * u).astype(BF16)
    o_ref[...] += _dot(act, wd_ref[...].astype(BF16))

    @pl.when(f == n_f - 1)
    def _():
        y = o_ref[...]
        o_ref[...] = x_ref[...] + 0.5 * (y * _rms_scale(y) * gpost_ref[...])


def _ffn(x, g_pre, g_post, w_in, w_down):
    m = x.shape[0]
    tm = min(FFN_TM, m)
    n_m, n_f = m // tm, D_FF // FFN_TF
    return pl.pallas_call(
        functools.partial(_ffn_body, n_f=n_f),
        grid=(n_m, n_f),
        in_specs=[
            pl.BlockSpec((tm, D_MODEL), lambda i, f: (i, 0)),
            pl.BlockSpec((1, D_MODEL), lambda i, f: (0, 0)),
            pl.BlockSpec((1, D_MODEL), lambda i, f: (0, 0)),
            pl.BlockSpec((D_MODEL, FFN_TF), lambda i, f: (0, f)),
            pl.BlockSpec((D_MODEL, FFN_TF), lambda i, f: (0, f + n_f)),
            pl.BlockSpec((FFN_TF, D_MODEL), lambda i, f: (f, 0)),
        ],
        out_specs=pl.BlockSpec((tm, D_MODEL), lambda i, f: (i, 0)),
        out_shape=jax.ShapeDtypeStruct((m, D_MODEL), F32),
        scratch_shapes=[pltpu.VMEM((tm, D_MODEL), BF16)],
        compiler_params=pltpu.CompilerParams(
            dimension_semantics=("parallel", "arbitrary"), vmem_limit_bytes=VMEM_LIMIT_MAX),
        name="ffn",
    )(x, g_pre, g_post, w_in, w_in, w_down)


def _repack_body(wt_hbm, o_ref, buf_ref, sem, *, n_head):
    j = pl.program_id(0)
    last = pl.num_programs(0) - 1
    o_ga = COL_GA + DECAY_LORA + ICLR_LORA + GATE_LORA

    def row_copy(src_row, n_rows, dst_row, slot):
        return pltpu.make_async_copy(wt_hbm.at[pl.ds(src_row, n_rows), :],
                                     buf_ref.at[pl.ds(dst_row, n_rows), :], sem.at[slot])

    @pl.when(j < last)
    def _():
        src = jnp.where(j < n_head, j * REPACK_ROWS, o_ga + (j - n_head) * REPACK_ROWS)
        copy = row_copy(src, REPACK_ROWS, 0, 0)
        copy.start()
        copy.wait()

    @pl.when(j == last)
    def _():
        n_gate = COL_LORA - (PROJ_COLS - REPACK_ROWS)
        pieces = [(o_ga + COL_LORA - COL_GA - n_gate, n_gate, 0),
                  (COL_GA, DECAY_LORA, n_gate),
                  (COL_GA + DECAY_LORA, ICLR_LORA, n_gate + LORA_PAD),
                  (COL_GA + DECAY_LORA + ICLR_LORA, GATE_LORA, n_gate + 2 * LORA_PAD)]
        for dst, width in ((n_gate, DECAY_LORA), (n_gate + LORA_PAD, ICLR_LORA)):
            buf_ref[dst + width:dst + LORA_PAD, :] = jnp.zeros((LORA_PAD - width, D_MODEL), F32)
        copies = [row_copy(src, n, dst, slot) for slot, (src, n, dst) in enumerate(pieces)]
        for copy in copies:
            copy.start()
        for copy in copies:
            copy.wait()

    o_ref[...] = buf_ref[...].astype(BF16)


def _repack_proj_weight(wt):
    assert COL_GA % REPACK_ROWS == 0 and PROJ_COLS % REPACK_ROWS == 0
    assert PROJ_COLS - REPACK_ROWS <= COL_LORA <= PROJ_COLS - LORA_COLS
    return pl.pallas_call(
        functools.partial(_repack_body, n_head=COL_GA // REPACK_ROWS),
        grid=(PROJ_COLS // REPACK_ROWS,),
        in_specs=[pl.BlockSpec(memory_space=pl.ANY)],
        out_specs=pl.BlockSpec((REPACK_ROWS, D_MODEL), lambda j: (j, 0)),
        out_shape=jax.ShapeDtypeStruct((PROJ_COLS, D_MODEL), BF16),
        scratch_shapes=[pltpu.VMEM((REPACK_ROWS, D_MODEL), F32), pltpu.SemaphoreType.DMA((4,))],
        compiler_params=pltpu.CompilerParams(
            dimension_semantics=("arbitrary",), vmem_limit_bytes=VMEM_LIMIT),
        name="repack_proj_weight",
    )(wt)


def _proj_body(x_ref, g_ref, w_ref, o_ref, h_ref):
    @pl.when(pl.program_id(1) == 0)
    def _():
        x = x_ref[...]
        h_ref[...] = (x * _rms_scale(x) * g_ref[...]).astype(BF16)

    o_ref[...] = _qk(h_ref[...], w_ref[...])


def _proj(x, g, w):
    m = x.shape[0]
    tm = min(PROJ_TM, m)
    return pl.pallas_call(
        _proj_body,
        grid=(m // tm, PROJ_COLS // PROJ_TN),
        in_specs=[
            pl.BlockSpec((tm, D_MODEL), lambda i, n: (i, 0)),
            pl.BlockSpec((1, D_MODEL), lambda i, n: (0, 0)),
            pl.BlockSpec((PROJ_TN, D_MODEL), lambda i, n: (n, 0)),
        ],
        out_specs=pl.BlockSpec((tm, PROJ_TN), lambda i, n: (i, n)),
        out_shape=jax.ShapeDtypeStruct((m, PROJ_COLS), F32),
        scratch_shapes=[pltpu.VMEM((tm, D_MODEL), BF16)],
        compiler_params=pltpu.CompilerParams(
            dimension_semantics=("parallel", "arbitrary"), vmem_limit_bytes=VMEM_LIMIT),
        name="proj",
    )(x, g, w)


def _qk(q, k):
    return lax.dot_general(q, k, (((1,), (1,)), ((), ())), preferred_element_type=F32)


def _attn_prompt_body(q_ref, k_ref, v_ref, diag_ref, o_ref, kbf_ref, vbf_ref, bias_ref):
    c = pl.program_id(1)
    shift = jnp.maximum(ATT_LEAD - c, 0)

    @pl.when(c == 0)
    def _():
        kbf_ref[...] = k_ref[...].astype(BF16)
        vbf_ref[...] = v_ref[...].T.astype(BF16)

    @pl.when((pl.program_id(0) == 0) & (c <= ATT_LEAD))
    def _():
        key = lax.broadcasted_iota(jnp.int32, (ATT_KEYS, ATT_Q), 0) // CHUNK
        qry = lax.broadcasted_iota(jnp.int32, (ATT_KEYS, ATT_Q), 1) // CHUNK
        dist = (LEFT_CHUNKS + qry) - (key + shift * (ATT_Q // CHUNK))
        in_band = (dist >= 0) & (dist <= LEFT_CHUNKS)
        for h in range(HEADS):
            diag = jnp.broadcast_to(diag_ref[shift, pl.ds(h, 1), :], (ATT_Q, ATT_DIAG))
            toep = pltpu.roll(diag, ATT_DIAG - ATT_Q, axis=1, stride=1, stride_axis=0)
            bias_ref[shift, h] = jnp.where(in_band, toep[:, :ATT_KEYS].T, NEG_INF)

    start = pl.multiple_of(jnp.maximum(c - ATT_LEAD, 0) * ATT_Q, ATT_Q)
    lane = lax.broadcasted_iota(jnp.int32, (ATT_Q, 128), 1)
    for pair in range(HEADS // 2):
        sl = slice(pair * 128, (pair + 1) * 128)
        kp = kbf_ref[pl.ds(start, ATT_KEYS), sl]
        vp = vbf_ref[sl, pl.ds(start, ATT_KEYS)]
        qp = q_ref[:, sl] * (HEAD_DIM ** -0.5)
        halves = []
        for half in range(2):
            qm = jnp.where((lane >= HEAD_DIM) == bool(half), qp, 0.0).astype(BF16)
            s = _qk(kp, qm) + bias_ref[shift, 2 * pair + half]
            p = jnp.exp(s - jnp.max(s, axis=0, keepdims=True))
            p = (p * (1.0 / jnp.sum(p, axis=0, keepdims=True))).astype(BF16)
            halves.append(_dot(vp, p))
        chan = lax.broadcasted_iota(jnp.int32, (128, ATT_Q), 0)
        o_ref[:, sl] = jnp.where(chan < HEAD_DIM, halves[0], halves[1]).T.astype(BF16)


def _attn_prompt(proj, diag, batch, seq):
    n_c = seq // ATT_Q
    once = dict(pipeline_mode=pl.Buffered(1))
    return pl.pallas_call(
        _attn_prompt_body,
        grid=(batch, n_c),
        in_specs=[
            pl.BlockSpec((ATT_Q, WIDTH), lambda b, c: (b * n_c + c, COL_Q // WIDTH)),
            pl.BlockSpec((seq, WIDTH), lambda b, c: (b, COL_K // WIDTH), **once),
            pl.BlockSpec((seq, WIDTH), lambda b, c: (b, COL_V // WIDTH), **once),
            pl.BlockSpec(diag.shape, lambda b, c: (0, 0, 0)),
        ],
        out_specs=pl.BlockSpec((ATT_Q, WIDTH), lambda b, c: (b * n_c + c, 0)),
        out_shape=jax.ShapeDtypeStruct((batch * seq, WIDTH), BF16),
        scratch_shapes=[pltpu.VMEM((seq, WIDTH), BF16), pltpu.VMEM((WIDTH, seq), BF16),
                        pltpu.VMEM((ATT_LEAD + 1, HEADS, ATT_KEYS, ATT_Q), F32)],
        compiler_params=pltpu.CompilerParams(
            dimension_semantics=("arbitrary", "arbitrary"), vmem_limit_bytes=VMEM_LIMIT_MAX),
        name="attn_prompt",
    )(proj, proj, proj, diag)


def _attn_sample_body(q_ref, kn_ref, vn_ref, ck_ref, cv_ref, bias_c_ref, bias_n_ref, o_ref):
    t = q_ref.shape[0]
    lane = lax.broadcasted_iota(jnp.int32, (t, 128), 1)
    slabs = [slice(pair * 128, (pair + 1) * 128) for pair in range(HEADS // 2)]

    s_c, s_n = [], []
    for h in range(HEADS):
        sl = slabs[h // 2]
        qm = jnp.where((lane >= HEAD_DIM) == bool(h % 2), q_ref[:, sl] * (HEAD_DIM ** -0.5), 0.0).astype(BF16)
        s_c.append(_dot(qm, ck_ref[0, sl, :].astype(BF16)) + bias_c_ref[h])
        s_n.append(_qk(qm, kn_ref[:, sl].astype(BF16)) + bias_n_ref[h])
    s_c = jnp.concatenate(s_c, axis=0)
    s_n = jnp.concatenate(s_n, axis=0)
    m = jnp.maximum(jnp.max(s_c, axis=-1, keepdims=True), jnp.max(s_n, axis=-1, keepdims=True))
    p_c = jnp.exp(s_c - m)
    p_n = jnp.exp(s_n - m)
    inv = 1.0 / (jnp.sum(p_c, axis=-1, keepdims=True) + jnp.sum(p_n, axis=-1, keepdims=True))
    p_c = (p_c * inv).astype(BF16)
    p_n = (p_n * inv).astype(BF16)
    for pair, sl in enumerate(slabs):
        cv = cv_ref[0, sl, :].astype(BF16)
        vn = vn_ref[:, sl].astype(BF16)
        halves = []
        for h in (2 * pair, 2 * pair + 1):
            rows = slice(h * t, (h + 1) * t)
            halves.append(_qk(p_c[rows], cv) + _dot(p_n[rows], vn))
        o_ref[:, sl] = jnp.where(lane < HEAD_DIM, halves[0], halves[1]).astype(BF16)


def _attn_sample(proj, row0, cache_k, cache_v, bias_c, bias_n, batch, t):
    n_past = cache_k.shape[2]
    blk0 = row0 // t
    return pl.pallas_call(
        _attn_sample_body,
        grid=(batch,),
        in_specs=[
            pl.BlockSpec((t, WIDTH), lambda b: (blk0 + b, COL_Q // WIDTH)),
            pl.BlockSpec((t, WIDTH), lambda b: (blk0 + b, COL_K // WIDTH)),
            pl.BlockSpec((t, WIDTH), lambda b: (blk0 + b, COL_V // WIDTH)),
            pl.BlockSpec((1, WIDTH, n_past), lambda b: (b, 0, 0)),
            pl.BlockSpec((1, WIDTH, n_past), lambda b: (b, 0, 0)),
            pl.BlockSpec((HEADS, t, n_past), lambda b: (0, 0, 0)),
            pl.BlockSpec((HEADS, t, t), lambda b: (0, 0, 0)),
        ],
        out_specs=pl.BlockSpec((t, WIDTH), lambda b: (b, 0)),
        out_shape=jax.ShapeDtypeStruct((batch * t, WIDTH), BF16),
        compiler_params=pltpu.CompilerParams(
            dimension_semantics=("parallel",), vmem_limit_bytes=VMEM_LIMIT),
        name="attn_sample",
    )(proj, proj, proj, cache_k, cache_v, bias_c, bias_n)


def _prep_body(p_ref, pl_ref, bp_ref, bl_ref, mu_ref, mul_ref, w0_ref, a0_ref, kk_ref, ka_ref, rk_ref,
               lnb_ref, wup_hi, wup_lo, aup_hi, aup_lo, gup_hi, gup_lo, bd_ref,
               r_o, w_o, k_o, v_o, kk_o, kka_o, g_o, cb_o):
    def shifted(p, first_row, mu):
        row = lax.broadcasted_iota(jnp.int32, p.shape, 0)
        prev = jnp.where(row == 0, first_row, pltpu.roll(p, 1, axis=0))
        return p + (prev - p) * mu

    xm = shifted(p_ref[...], bp_ref[0], mu_ref[...])
    xl = shifted(pl_ref[...], bl_ref[0], mul_ref[...])
    r = xm[:, 0:WIDTH]
    k = xm[:, WIDTH:2 * WIDTH]
    v = xm[:, 2 * WIDTH:3 * WIDTH]
    wd = xl[:, 0:LORA_PAD]
    ad = xl[:, LORA_PAD:2 * LORA_PAD]
    gd = xl[:, 2 * LORA_PAD:]

    z = -(w0_ref[...] + _dot_f32(jnp.tanh(wd), wup_hi[...], wup_lo[...]))
    softplus = jnp.maximum(z, 0.0) + jnp.log(1.0 + jnp.exp(-jnp.abs(z)))
    decay = jnp.exp(-jnp.exp(-softplus - 0.5))
    a = _sigmoid(a0_ref[...] + _dot_f32(ad, aup_hi[...], aup_lo[...]))
    g = _dot_f32(_sigmoid(gd), gup_hi[...], gup_lo[...])

    bd = bd_ref[...]
    kk = k * kk_ref[...]
    kk = kk / jnp.maximum(jnp.sqrt(_segsum(kk * kk, bd)), 1e-12)
    k2 = k * (1.0 + (a - 1.0) * ka_ref[...])
    bonus = _segsum(r * k2 * rk_ref[...], bd) * v

    r_o[...] = r
    w_o[...] = decay
    k_o[...] = k2
    v_o[...] = v
    kk_o[...] = kk
    kka_o[...] = kk * a
    g_o[...] = g
    cb_o[...] = lnb_ref[...] + bonus


def _prep(proj, row0, n_rows, tm, first_rkv, first_lora, params):
    n_t = n_rows // tm
    blk0 = row0 // tm
    vec = lambda w: pl.BlockSpec((1, w), lambda i: (0, 0))
    mat = lambda r: pl.BlockSpec((r, WIDTH), lambda i: (0, 0))
    out = pl.BlockSpec((tm, WIDTH), lambda i: (i, 0))
    return pl.pallas_call(
        _prep_body,
        grid=(n_t,),
        in_specs=[
            pl.BlockSpec((tm, 3 * WIDTH), lambda i: (blk0 + i, COL_RKV // (3 * WIDTH))),
            pl.BlockSpec((tm, LORA_COLS), lambda i: (blk0 + i, COL_LORA // LORA_COLS)),
            pl.BlockSpec((1, 1, 3 * WIDTH), lambda i: (i, 0, 0)),
            pl.BlockSpec((1, 1, LORA_COLS), lambda i: (i, 0, 0)),
            vec(3 * WIDTH), vec(LORA_COLS), vec(WIDTH), vec(WIDTH), vec(WIDTH), vec(WIDTH), vec(WIDTH),
            vec(WIDTH), mat(LORA_PAD), mat(LORA_PAD), mat(LORA_PAD), mat(LORA_PAD),
            mat(GATE_LORA), mat(GATE_LORA),
            pl.BlockSpec((SEG_HEADS * HEAD_DIM, SEG_HEADS * HEAD_DIM), lambda i: (0, 0)),
        ],
        out_specs=[out] * 8,
        out_shape=[jax.ShapeDtypeStruct((n_rows, WIDTH), F32)] * 8,
        compiler_params=pltpu.CompilerParams(
            dimension_semantics=("parallel",), vmem_limit_bytes=VMEM_LIMIT),
        name="rwkv_prep",
    )(proj, proj, first_rkv, first_lora, *params)


def _scan_steps(s_ref, row, get_v, put_y, n_steps):
    W, KKA, K, R, KK = range(5)
    zero = jnp.zeros((KLO, 128), F32)

    sa0 = [zero, zero]
    for k in range(HEAD_DIM):
        sa0[k % 2] = sa0[k % 2] - s_ref[k] * row(KK, 0, k)

    def step(t, sa):
        v = get_v(t)
        t_next = jnp.minimum(t + 1, n_steps - 1)
        y = [zero, zero]
        sa_next = [zero, zero]
        for k in range(HEAD_DIM):
            s_new = s_ref[k] * row(W, t, k) + sa * row(KKA, t, k) + v * row(K, t, k)
            s_ref[k] = s_new
            y[k % 2] = y[k % 2] + s_new * row(R, t, k)
            sa_next[k % 2] = sa_next[k % 2] - s_new * row(KK, t_next, k)
        put_y(t, y[0] + y[1])
        return sa_next[0] + sa_next[1]

    lax.fori_loop(0, n_steps, step, sa0[0] + sa0[1])


def _scan_body(w_ref, kka_ref, k_ref, r_ref, kk_ref, v_ref, s0_ref, y_ref, st_ref, s_ref, rows_ref, *, tb):
    @pl.when(pl.program_id(1) == 0)
    def _():
        s_ref[...] = s0_ref[0]

    lane = lax.broadcasted_iota(jnp.int32, (tb * KLO, 128), 1)
    for a, ref in enumerate((w_ref, kka_ref, k_ref, r_ref, kk_ref)):
        x = ref[0].reshape(tb * KLO, 128)
        swapped = pltpu.roll(x, HEAD_DIM, axis=1)
        rows_ref[a, :, 0:KLO, :] = jnp.where(lane < HEAD_DIM, x, swapped).reshape(tb, KLO, 128)
        rows_ref[a, :, KLO:, :] = jnp.where(lane < HEAD_DIM, swapped, x).reshape(tb, KLO, 128)

    def row(a, t, k):
        return jnp.broadcast_to(rows_ref[a, t, pl.ds(k, 1), :], (KLO, 128))

    def put_y(t, y):
        y_ref[0, t] = y

    _scan_steps(s_ref, row, lambda t: v_ref[0, t], put_y, tb)

    @pl.when(pl.program_id(1) == pl.num_programs(1) - 1)
    def _():
        st_ref[0] = s_ref[...]


def _scan_tok_body(w_hbm, kka_hbm, k_hbm, r_hbm, kk_hbm, v_hbm, s0_ref, y_ref, st_ref,
                   s_ref, rows_ref, xt_ref, vt_ref, yt_ref, in_ref, in_sem, *, tb):
    step = pl.program_id(0)
    operands = (w_hbm, kka_hbm, k_hbm, r_hbm, kk_hbm, v_hbm)

    def block_copy(a, blk):
        return pltpu.make_async_copy(operands[a].at[:, pl.ds(blk * tb, tb), :], in_ref.at[a], in_sem.at[a])

    @pl.when(step == 0)
    def _():
        for a in range(len(operands)):
            block_copy(a, 0).start()
        s_ref[...] = s0_ref[0]
        xt_ref[...] = jnp.zeros_like(xt_ref)

    def head_rows(c):
        return pl.ds(c, HEADS, stride=XT_PITCH)

    def channel_major(ref):
        for b in range(LANE_GROUP_BATCH):
            xt = ref[b].T
            for h in range(HEADS):
                xt_ref[b, h * XT_PITCH:h * XT_PITCH + HEAD_DIM, 0:tb] = xt[h * HEAD_DIM:(h + 1) * HEAD_DIM]

    def lane_tile(chan):
        parts = [xt_ref[b, head_rows(c), :] for c in chan for b in range(LANE_GROUP_BATCH)]
        return jnp.concatenate(parts, axis=0).T[0:tb]

    for a in range(5):
        block_copy(a, step).wait()
        channel_major(in_ref.at[a])
        for k in range(HEAD_DIM):
            rows_ref[a, k] = lane_tile((k, k))
    block_copy(5, step).wait()
    channel_major(in_ref.at[5])
    for j in range(KLO):
        vt_ref[pl.ds(j, tb, stride=STEP_PITCH), :] = lane_tile((j, KLO + j))

    @pl.when(step + 1 < pl.num_programs(0))
    def _():
        for a in range(len(operands)):
            block_copy(a, step + 1).start()

    def row(a, t, k):
        return jnp.broadcast_to(rows_ref[a, k, pl.ds(t, 1), :], (KLO, 128))

    def step_rows(t):
        return pl.ds(pl.multiple_of(t * STEP_PITCH, 8), KLO)

    def put_y(t, y):
        yt_ref[step_rows(t), :] = y

    _scan_steps(s_ref, row, lambda t: vt_ref[step_rows(t), :], put_y, tb)

    for j in range(KLO):
        steps = yt_ref[pl.ds(j, tb, stride=STEP_PITCH), :]
        if tb < 128:
            steps = jnp.concatenate([steps, jnp.zeros((128 - tb, 128), F32)], axis=0)
        tile = steps.T
        for half in range(2):
            for b in range(LANE_GROUP_BATCH):
                r0 = (half * LANE_GROUP_BATCH + b) * HEADS
                xt_ref[b, head_rows(half * KLO + j), :] = tile[r0:r0 + HEADS]
    for b in range(LANE_GROUP_BATCH):
        heads = [xt_ref[b, h * XT_PITCH:h * XT_PITCH + HEAD_DIM, 0:tb] for h in range(HEADS)]
        y_ref[b] = jnp.concatenate(heads, axis=0).T

    @pl.when(pl.program_id(0) == pl.num_programs(0) - 1)
    def _():
        st_ref[0] = s_ref[...]


def _scan_tok(ops, v, s0, tb):
    nb, t, _ = v.shape
    tok_spec = pl.BlockSpec((nb, tb, WIDTH), lambda j: (0, j, 0))
    st_spec = pl.BlockSpec((1, HEAD_DIM, KLO, 128), lambda j: (0, 0, 0, 0))
    return pl.pallas_call(
        functools.partial(_scan_tok_body, tb=tb),
        grid=(t // tb,),
        in_specs=[pl.BlockSpec(memory_space=pl.ANY)] * 6 + [st_spec],
        out_specs=[tok_spec, st_spec],
        out_shape=[jax.ShapeDtypeStruct((nb, t, WIDTH), F32),
                   jax.ShapeDtypeStruct((1, HEAD_DIM, KLO, 128), F32)],
        scratch_shapes=[pltpu.VMEM((HEAD_DIM, KLO, 128), F32),
                        pltpu.VMEM((5, HEAD_DIM, tb, 128), F32),
                        pltpu.VMEM((nb, HEADS * XT_PITCH, 128), F32),
                        pltpu.VMEM((tb * STEP_PITCH, 128), F32),
                        pltpu.VMEM((tb * STEP_PITCH, 128), F32),
                        pltpu.VMEM((6, nb, tb, WIDTH), F32),
                        pltpu.SemaphoreType.DMA((6,))],
        compiler_params=pltpu.CompilerParams(
            dimension_semantics=("arbitrary",), vmem_limit_bytes=VMEM_LIMIT),
        name="rwkv_scan_tok",
    )(*ops, v, s0)


def _scan(rows, v, s0, tb):
    g, t = v.shape[0], v.shape[1]
    row_spec = pl.BlockSpec((1, tb, KLO, 128), lambda i, j: (i, j, 0, 0))
    st_spec = pl.BlockSpec((1, HEAD_DIM, KLO, 128), lambda i, j: (i, 0, 0, 0))
    return pl.pallas_call(
        functools.partial(_scan_body, tb=tb),
        grid=(g, t // tb),
        in_specs=[row_spec] * 6 + [st_spec],
        out_specs=[row_spec, st_spec],
        out_shape=[jax.ShapeDtypeStruct((g, t, KLO, 128), F32),
                   jax.ShapeDtypeStruct((g, HEAD_DIM, KLO, 128), F32)],
        scratch_shapes=[pltpu.VMEM((HEAD_DIM, KLO, 128), F32),
                        pltpu.VMEM((5, tb, HEAD_DIM, 128), F32)],
        compiler_params=pltpu.CompilerParams(
            dimension_semantics=("parallel", "arbitrary"), vmem_limit_bytes=VMEM_LIMIT),
        name="rwkv_scan",
    )(*rows, v, s0)


def _rows_to_scan(x, n_batch, t):
    g = n_batch // LANE_GROUP_BATCH
    x = x.reshape(g, LANE_GROUP_BATCH, t, HEADS, 2, KLO)
    return x.transpose(0, 2, 5, 4, 1, 3).reshape(g, t, KLO, 128)


def _rows_from_scan(y, n_batch, t):
    g = n_batch // LANE_GROUP_BATCH
    y = y.reshape(g, t, KLO, 2, LANE_GROUP_BATCH, HEADS).transpose(0, 4, 1, 5, 3, 2)
    return y.reshape(n_batch * t, WIDTH)


def _state_to_scan(s, n_batch):
    g = n_batch // LANE_GROUP_BATCH
    s = s.reshape(g, LANE_GROUP_BATCH, HEADS, 2, KLO, HEAD_DIM)
    return s.transpose(0, 5, 4, 3, 1, 2).reshape(g, HEAD_DIM, KLO, 128)


def _state_from_scan(s, n_batch):
    g = n_batch // LANE_GROUP_BATCH
    s = s.reshape(g, HEAD_DIM, KLO, 2, LANE_GROUP_BATCH, HEADS)
    return s.transpose(0, 4, 5, 3, 2, 1).reshape(n_batch, HEADS, HEAD_DIM, HEAD_DIM)


def _merge_body(att_ref, y_ref, g_ref, cb_ref, ga_ref, gb_ref, x_ref, lnw_ref, gpost_ref, bd_ref,
                wa_ref, wr_ref, wo_ref, o_ref):
    bd = bd_ref[...]
    y = y_ref[...]
    d = y - _segsum(y, bd) * (1.0 / HEAD_DIM)
    var = _segsum(d * d, bd) * (1.0 / HEAD_DIM)
    rw = ((d * lax.rsqrt(var + GN_EPS) * lnw_ref[...] + cb_ref[...]) * g_ref[...]).astype(BF16)
    rw_o = _dot(rw, wr_ref[...])
    att_o = _dot(att_ref[...], wa_ref[...])
    mixed = (_sigmoid(ga_ref[...]) * att_o + _sigmoid(gb_ref[...]) * rw_o).astype(BF16)
    z = _dot(mixed, wo_ref[...])
    o_ref[...] = x_ref[...] + z * _rms_scale(z) * gpost_ref[...]


def _merge(att, y, g, cb, proj, x, ln_w, g_post, ones_bd, w_att, w_rwkv, w_out):
    m = x.shape[0]
    tm = min(MERGE_TM, m)
    tok = lambda w: pl.BlockSpec((tm, w), lambda i: (i, 0))
    const = lambda a: pl.BlockSpec(a.shape, lambda i: (0,) * a.ndim)
    return pl.pallas_call(
        _merge_body,
        grid=(m // tm,),
        in_specs=[
            tok(WIDTH), tok(WIDTH), tok(WIDTH), tok(WIDTH),
            pl.BlockSpec((tm, D_MODEL), lambda i: (i, COL_GA // D_MODEL)),
            pl.BlockSpec((tm, D_MODEL), lambda i: (i, COL_GB // D_MODEL)),
            tok(D_MODEL), const(ln_w), const(g_post), const(ones_bd),
            const(w_att), const(w_rwkv), const(w_out),
        ],
        out_specs=tok(D_MODEL),
        out_shape=jax.ShapeDtypeStruct((m, D_MODEL), F32),
        compiler_params=pltpu.CompilerParams(
            dimension_semantics=("parallel",), vmem_limit_bytes=VMEM_LIMIT),
        name="merge",
    )(att, y, g, cb, proj, proj, x, ln_w, g_post, ones_bd, w_att, w_rwkv, w_out)


def _pad_rows(x, height):
    return jnp.pad(x, ((0, height - x.shape[0]), (0, 0)))


def _reorder_shift_cols(x):
    o = 3 * WIDTH
    wd = x[..., o:o + DECAY_LORA]
    ad = x[..., o + DECAY_LORA:o + DECAY_LORA + ICLR_LORA]
    gd = x[..., o + DECAY_LORA + ICLR_LORA:]
    pad = [(0, 0)] * (x.ndim - 1)
    lora = jnp.concatenate([jnp.pad(wd, pad + [(0, LORA_PAD - DECAY_LORA)]),
                            jnp.pad(ad, pad + [(0, LORA_PAD - ICLR_LORA)]), gd], axis=-1)
    return x[..., :o], lora


def _shift_row_from_proj(rows):
    lo = rows[:, COL_LORA:]
    return jnp.concatenate([rows[:, COL_RKV:COL_RKV + 3 * WIDTH], lo[:, :DECAY_LORA],
                            lo[:, LORA_PAD:LORA_PAD + ICLR_LORA], lo[:, 2 * LORA_PAD:]], axis=-1)


def _bias_diagonals(rel_bias):
    u = np.arange(ATT_DIAG)[None, :]
    s = np.arange(ATT_LEAD + 1)[:, None]
    idx = np.clip(ATT_KEYS - u - ATT_Q * s, -MAX_REL, MAX_REL) + MAX_REL
    tab = jnp.take(rel_bias, jnp.asarray(idx.reshape(-1), jnp.int32), axis=1)
    return tab.reshape(HEADS, ATT_LEAD + 1, ATT_DIAG).transpose(1, 0, 2)


def _bias_sample(rel_bias, t, n_past):
    rel = np.arange(t)[:, None] + n_past - np.arange(n_past + t)[None, :]
    idx = np.clip(rel, -MAX_REL, MAX_REL) + MAX_REL
    tab = jnp.take(rel_bias, jnp.asarray(idx.reshape(-1), jnp.int32), axis=1).reshape(HEADS, t, n_past + t)
    return tab[:, :, :n_past], tab[:, :, n_past:]


def _layer(x, n_batch, t, attend, first_rows, prep_tm, wkv0, scan_tb, wts):
    m = n_batch * t
    x = _ffn(x, *wts["ffn1"])
    proj = _proj(x, *wts["proj"])
    att = attend(proj)

    prep = _prep(proj, 0, m, prep_tm, *first_rows(proj), wts["prep"])
    r, wdec, k2, v, kk, kka, g, cb = prep
    ops = (wdec, kka, k2, r, kk)
    if n_batch == LANE_GROUP_BATCH and scan_tb % 64 == 0:
        tok = lambda a: a.reshape(n_batch, t, WIDTH)
        y, s_t = _scan_tok([tok(a) for a in ops], tok(v), _state_to_scan(wkv0, n_batch), scan_tb)
        y = y.reshape(m, WIDTH)
    else:
        rows = [_rows_to_scan(a, n_batch, t) for a in ops]
        y, s_t = _scan(rows, _rows_to_scan(v, n_batch, t), _state_to_scan(wkv0, n_batch), scan_tb)
        y = _rows_from_scan(y, n_batch, t)

    x = _merge(att, y, g, cb, proj, x, *wts["merge"])
    x = _ffn(x, *wts["ffn2"])

    last = jnp.concatenate([proj[(b + 1) * t - 1:(b + 1) * t] for b in range(n_batch)])
    return x, proj, _state_from_scan(s_t, n_batch), _shift_row_from_proj(last)[:, None]


def kernel(x_prompt, x_sample, cache_k, cache_v, state_wkv, state_shift, norm_ffn1_pre, norm_ffn1_post, w_ffn1_in, w_ffn1_down, norm_mix_pre, norm_mix_post, w_in, rel_bias, w_att_out, rwkv_mu, rwkv_w0, rwkv_w_up, rwkv_a0, rwkv_a_up, rwkv_g_up, rwkv_k_k, rwkv_k_a, rwkv_r_k, rwkv_ln_w, rwkv_ln_b, w_rwkv_out, w_out, norm_ffn2_pre, norm_ffn2_post, w_ffn2_in, w_ffn2_down):
    depth = w_in.shape[0]
    batch, seq, _ = x_prompt.shape
    dec_batch, dec_seq, _ = x_sample.shape
    n_past = cache_k.shape[2]
    m_p, m_s = batch * seq, dec_batch * dec_seq
    keep = min(PAST_BAND, seq)
    assert n_past == PAST_BAND and seq % ATT_Q == 0 and seq // ATT_Q > ATT_LEAD
    assert m_p % FFN_TM == 0 and m_p % PROJ_TM == 0 and m_p % MERGE_TM == 0 and seq % PREP_TM == 0
    assert seq % SCAN_TB == 0 and batch == LANE_GROUP_BATCH and dec_batch % LANE_GROUP_BATCH == 0

    x_p = x_prompt.reshape(m_p, D_MODEL)
    x_s = x_sample.reshape(m_s, D_MODEL)
    ones_bd = jnp.asarray(np.kron(np.eye(SEG_HEADS), np.ones((HEAD_DIM, HEAD_DIM))), BF16)
    row = lambda v: v.reshape(1, -1)
    heads = lambda a, n, t: a.reshape(n, t, HEADS, HEAD_DIM)

    outs = {k: [] for k in ("kp", "vp", "wp", "sp", "ks", "vs", "ws", "ss")}
    for l in range(depth):
        w_proj = _repack_proj_weight(w_in[l].T)
        mu_rkv, mu_lora = _reorder_shift_cols(rwkv_mu[l][None])
        wts = {
            "ffn1": (row(norm_ffn1_pre[l]), row(norm_ffn1_post[l]),
                     w_ffn1_in[l], w_ffn1_down[l]),
            "proj": (row(norm_mix_pre[l]), w_proj),
            "prep": (mu_rkv, mu_lora, row(rwkv_w0[l]), row(rwkv_a0[l]), row(rwkv_k_k[l]), row(rwkv_k_a[l]),
                     row(rwkv_r_k[l]), row(rwkv_ln_b[l]),
                     *_split_bf16(_pad_rows(rwkv_w_up[l], LORA_PAD)),
                     *_split_bf16(_pad_rows(rwkv_a_up[l], LORA_PAD)),
                     *_split_bf16(rwkv_g_up[l]), ones_bd),
            "merge": (row(rwkv_ln_w[l]), row(norm_mix_post[l]), ones_bd,
                      w_att_out[l].astype(BF16), w_rwkv_out[l].astype(BF16), w_out[l].astype(BF16)),
            "ffn2": (row(norm_ffn2_pre[l]), row(norm_ffn2_post[l]),
                     w_ffn2_in[l], w_ffn2_down[l]),
        }

        diag = _bias_diagonals(rel_bias[l])
        starts_seq = (jnp.arange(m_p // PREP_TM) % (seq // PREP_TM) == 0)[:, None]

        def first_rows_prompt(proj):
            last_of_tile = proj.reshape(m_p // PREP_TM, PREP_TM, PROJ_COLS)[:, PREP_TM - 1]
            first = jnp.where(starts_seq, 0.0, jnp.roll(last_of_tile, 1, axis=0))[:, None]
            return first[..., COL_RKV:COL_RKV + 3 * WIDTH], first[..., COL_LORA:]

        x_p, proj_p, wkv_p, sh_p = _layer(
            x_p, batch, seq, lambda proj: _attn_prompt(proj, diag, batch, seq), first_rows_prompt, PREP_TM,
            jnp.zeros((batch, HEADS, HEAD_DIM, HEAD_DIM), F32), SCAN_TB, wts)
        tail = lambda col: jnp.stack([proj_p[(b + 1) * seq - keep:(b + 1) * seq, col:col + WIDTH]
                                      for b in range(batch)])
        outs["kp"].append(heads(tail(COL_K), batch, keep))
        outs["vp"].append(heads(tail(COL_V), batch, keep))
        outs["wp"].append(wkv_p.astype(state_wkv.dtype))
        outs["sp"].append(sh_p)

        chan_major = lambda c: c.transpose(0, 2, 3, 1).reshape(dec_batch, WIDTH, n_past)
        ck, cv = chan_major(cache_k[l]), chan_major(cache_v[l])
        bias_c, bias_n = _bias_sample(rel_bias[l], dec_seq, n_past)
        x_s, proj_s, wkv_s, sh_s = _layer(
            x_s, dec_batch, dec_seq,
            lambda proj: _attn_sample(proj, 0, ck, cv, bias_c, bias_n, dec_batch, dec_seq),
            lambda proj: _reorder_shift_cols(state_shift[l]), dec_seq,
            state_wkv[l].astype(F32), dec_seq, wts)
        outs["ks"].append(heads(proj_s[:, COL_K:COL_K + WIDTH], dec_batch, dec_seq))
        outs["vs"].append(heads(proj_s[:, COL_V:COL_V + WIDTH], dec_batch, dec_seq))
        outs["ws"].append(wkv_s.astype(state_wkv.dtype))
        outs["ss"].append(sh_s)

    st = lambda k: jnp.stack(outs[k])
    return (x_p.reshape(batch, seq, D_MODEL), x_s.reshape(dec_batch, dec_seq, D_MODEL),
            st("kp"), st("vp"), st("wp"), st("sp"), st("ks"), st("vs"), st("ws"), st("ss"))
```

```python
import functools

import numpy as np
import jax
import jax.numpy as jnp
from jax import lax
from jax.experimental import pallas as pl
from jax.experimental.pallas import tpu as pltpu

F32 = jnp.float32
BF16 = jnp.bfloat16

D_MODEL = 2048
CHUNK = 64
LEFT_CHUNKS = 8
PAST_BAND = LEFT_CHUNKS * CHUNK
HEAD_DIM = 64
HEADS = 16
WIDTH = HEADS * HEAD_DIM
DECAY_LORA = 96
ICLR_LORA = 96
GATE_LORA = 256
LORA_PAD = 128
LORA_COLS = 2 * LORA_PAD + GATE_LORA
D_FF = 5632
MAX_REL = 128
RMS_EPS = 1e-6
GN_EPS = 64e-5
NEG_INF = -1e30

COL_Q, COL_K, COL_V = 0, WIDTH, 2 * WIDTH
COL_RKV = 3 * WIDTH
COL_GA = 6 * WIDTH
COL_GB = COL_GA + D_MODEL
COL_LORA = COL_GB + D_MODEL
PROJ_COLS = COL_LORA + LORA_COLS

VMEM_LIMIT = 56 * 1024 * 1024
VMEM_LIMIT_MAX = 60 * 1024 * 1024

FFN_TM = 1024
FFN_TF = 256
FFN_EDGE_ROWS = 16
PROJ_TM = 1024
PROJ_TN = 1536
REPACK_ROWS = 768
SEG_HEADS = 4
MERGE_TM = 256
PREP_TM = 256
SCAN_TB = 128
ATT_Q = 2 * CHUNK
ATT_KEYS = PAST_BAND + ATT_Q
ATT_LEAD = PAST_BAND // ATT_Q
ATT_DIAG = ATT_KEYS + ATT_Q
LANE_GROUP_BATCH = 4
KLO = HEAD_DIM // 2
STEP_PITCH = KLO + 8
XT_PITCH = HEAD_DIM + 8


def _sigmoid(x):
    return 1.0 / (1.0 + jnp.exp(-x))


def _split_bf16(x):
    hi = x.astype(BF16)
    lo = (x - hi.astype(F32)).astype(BF16)
    return hi, lo


def _dot(a, b):
    return jnp.dot(a, b, preferred_element_type=F32)


def _dot_f32(a, b_hi, b_lo):
    a_hi, a_lo = _split_bf16(a)
    return _dot(a_hi, b_hi) + _dot(a_lo, b_hi) + _dot(a_hi, b_lo)


def _segsum(x, ones_bd):
    hi, lo = _split_bf16(x)
    width = ones_bd.shape[0]
    outs = []
    for j in range(x.shape[-1] // width):
        sl = slice(j * width, (j + 1) * width)
        outs.append(_dot(hi[:, sl], ones_bd) + _dot(lo[:, sl], ones_bd))
    return jnp.concatenate(outs, axis=-1)


def _rms_scale(x):
    return lax.rsqrt(jnp.mean(x * x, axis=-1, keepdims=True) + RMS_EPS)


def _ffn_body(x_ref, gpre_ref, gpost_ref, wg_ref, wu_ref, wd_ref, o_ref, h_ref, *, n_f):
    f = pl.program_id(1)

    chunks = [slice(r, r + FFN_EDGE_ROWS) for r in range(0, x_ref.shape[0], FFN_EDGE_ROWS)]

    @pl.when(f == 0)
    def _():
        for rows in chunks:
            x = x_ref[rows, :]
            h_ref[rows, :] = (x * _rms_scale(x) * gpre_ref[...]).astype(BF16)
            o_ref[rows, :] = jnp.zeros((FFN_EDGE_ROWS, D_MODEL), F32)

    h = h_ref[...]
    g = _dot(h, wg_ref[...].astype(BF16))
    u = _dot(h, wu_ref[...].astype(BF16))
    act = (g * _sigmoid(g) * u).astype(BF16)
    o_ref[...] += _dot(act, wd_ref[...].astype(BF16))

    @pl.when(f == n_f - 1)
    def _():
        for rows in chunks:
            y = o_ref[rows, :]
            o_ref[rows, :] = x_ref[rows, :] + y * _rms_scale(y) * gpost_ref[...]


def _ffn(x, g_pre, g_post, w_in, w_down):
    m = x.shape[0]
    tm = min(FFN_TM, m)
    n_m, n_f = m // tm, D_FF // FFN_TF
    return pl.pallas_call(
        functools.partial(_ffn_body, n_f=n_f),
        grid=(n_m, n_f),
        in_specs=[
            pl.BlockSpec((tm, D_MODEL), lambda i, f: (i, 0)),
            pl.BlockSpec((1, D_MODEL), lambda i, f: (0, 0)),
            pl.BlockSpec((1, D_MODEL), lambda i, f: (0, 0)),
            pl.BlockSpec((D_MODEL, FFN_TF), lambda i, f: (0, f)),
            pl.BlockSpec((D_MODEL, FFN_TF), lambda i, f: (0, f + n_f)),
            pl.BlockSpec((FFN_TF, D_MODEL), lambda i, f: (f, 0)),
        ],
        out_specs=pl.BlockSpec((tm, D_MODEL), lambda i, f: (i, 0)),
        out_shape=jax.ShapeDtypeStruct((m, D_MODEL), F32),
        scratch_shapes=[pltpu.VMEM((tm, D_MODEL), BF16)],
        compiler_params=pltpu.CompilerParams(
            dimension_semantics=("parallel", "arbitrary"), vmem_limit_bytes=VMEM_LIMIT_MAX),
        name="ffn",
    )(x, g_pre, g_post, w_in, w_in, w_down)


def _repack_body(wt_hbm, o_ref, buf_ref, sem, *, n_head):
    j = pl.program_id(0)
    last = pl.num_programs(0) - 1
    o_ga = COL_GA + DECAY_LORA + ICLR_LORA + GATE_LORA

    def row_copy(src_row, n_rows, dst_row, slot):
        return pltpu.make_async_copy(wt_hbm.at[pl.ds(src_row, n_rows), :],
                                     buf_ref.at[pl.ds(dst_row, n_rows), :], sem.at[slot])

    @pl.when(j < last)
    def _():
        src = jnp.where(j < n_head, j * REPACK_ROWS, o_ga + (j - n_head) * REPACK_ROWS)
        copy = row_copy(src, REPACK_ROWS, 0, 0)
        copy.start()
        copy.wait()

    @pl.when(j == last)
    def _():
        n_gate = COL_LORA - (PROJ_COLS - REPACK_ROWS)
        pieces = [(o_ga + COL_LORA - COL_GA - n_gate, n_gate, 0),
                  (COL_GA, DECAY_LORA, n_gate),
                  (COL_GA + DECAY_LORA, ICLR_LORA, n_gate + LORA_PAD),
                  (COL_GA + DECAY_LORA + ICLR_LORA, GATE_LORA, n_gate + 2 * LORA_PAD)]
        for dst, width in ((n_gate, DECAY_LORA), (n_gate + LORA_PAD, ICLR_LORA)):
            buf_ref[dst + width:dst + LORA_PAD, :] = jnp.zeros((LORA_PAD - width, D_MODEL), F32)
        copies = [row_copy(src, n, dst, slot) for slot, (src, n, dst) in enumerate(pieces)]
        for copy in copies:
            copy.start()
        for copy in copies:
            copy.wait()

    o_ref[...] = buf_ref[...].astype(BF16)


def _repack_proj_weight(wt):
    assert COL_GA % REPACK_ROWS == 0 and PROJ_COLS % REPACK_ROWS == 0
    assert PROJ_COLS - REPACK_ROWS <= COL_LORA <= PROJ_COLS - LORA_COLS
    return pl.pallas_call(
        functools.partial(_repack_body, n_head=COL_GA // REPACK_ROWS),
        grid=(PROJ_COLS // REPACK_ROWS,),
        in_specs=[pl.BlockSpec(memory_space=pl.ANY)],
        out_specs=pl.BlockSpec((REPACK_ROWS, D_MODEL), lambda j: (j, 0)),
        out_shape=jax.ShapeDtypeStruct((PROJ_COLS, D_MODEL), BF16),
        scratch_shapes=[pltpu.VMEM((REPACK_ROWS, D_MODEL), F32), pltpu.SemaphoreType.DMA((4,))],
        compiler_params=pltpu.CompilerParams(
            dimension_semantics=("arbitrary",), vmem_limit_bytes=VMEM_LIMIT),
        name="repack_proj_weight",
    )(wt)


def _proj_body(x_ref, g_ref, w_ref, o_ref, h_ref):
    @pl.when(pl.program_id(1) == 0)
    def _():
        x = x_ref[...]
        h_ref[...] = (x * _rms_scale(x) * g_ref[...]).astype(BF16)

    o_ref[...] = _qk(h_ref[...], w_ref[...])


def _proj(x, g, w):
    m = x.shape[0]
    tm = min(PROJ_TM, m)
    return pl.pallas_call(
        _proj_body,
        grid=(m // tm, PROJ_COLS // PROJ_TN),
        in_specs=[
            pl.BlockSpec((tm, D_MODEL), lambda i, n: (i, 0)),
            pl.BlockSpec((1, D_MODEL), lambda i, n: (0, 0)),
            pl.BlockSpec((PROJ_TN, D_MODEL), lambda i, n: (n, 0)),
        ],
        out_specs=pl.BlockSpec((tm, PROJ_TN), lambda i, n: (i, n)),
        out_shape=jax.ShapeDtypeStruct((m, PROJ_COLS), F32),
        scratch_shapes=[pltpu.VMEM((tm, D_MODEL), BF16)],
        compiler_params=pltpu.CompilerParams(
            dimension_semantics=("parallel", "arbitrary"), vmem_limit_bytes=VMEM_LIMIT),
        name="proj",
    )(x, g, w)


def _qk(q, k):
    return lax.dot_general(q, k, (((1,), (1,)), ((), ())), preferred_element_type=F32)


def _attn_prompt_body(q_ref, k_ref, v_ref, diag_ref, o_ref, kbf_ref, vbf_ref, bias_ref):
    c = pl.program_id(1)
    shift = jnp.maximum(ATT_LEAD - c, 0)

    @pl.when(c == 0)
    def _():
        kbf_ref[...] = k_ref[...].astype(BF16)
        vbf_ref[...] = v_ref[...].T.astype(BF16)

    @pl.when((pl.program_id(0) == 0) & (c <= ATT_LEAD))
    def _():
        key = lax.broadcasted_iota(jnp.int32, (ATT_KEYS, ATT_Q), 0) // CHUNK
        qry = lax.broadcasted_iota(jnp.int32, (ATT_KEYS, ATT_Q), 1) // CHUNK
        dist = (LEFT_CHUNKS + qry) - (key + shift * (ATT_Q // CHUNK))
        in_band = (dist >= 0) & (dist <= LEFT_CHUNKS)
        for h in range(HEADS):
            diag = jnp.broadcast_to(diag_ref[shift, pl.ds(h, 1), :], (ATT_Q, ATT_DIAG))
            toep = pltpu.roll(diag, ATT_DIAG - ATT_Q, axis=1, stride=1, stride_axis=0)
            bias_ref[shift, h] = jnp.where(in_band, toep[:, :ATT_KEYS].T, NEG_INF)

    start = pl.multiple_of(jnp.maximum(c - ATT_LEAD, 0) * ATT_Q, ATT_Q)
    lane = lax.broadcasted_iota(jnp.int32, (ATT_Q, 128), 1)
    for pair in range(HEADS // 2):
        sl = slice(pair * 128, (pair + 1) * 128)
        kp = kbf_ref[pl.ds(start, ATT_KEYS), sl]
        vp = vbf_ref[sl, pl.ds(start, ATT_KEYS)]
        qp = q_ref[:, sl] * (HEAD_DIM ** -0.5)
        halves = []
        for half in range(2):
            qm = jnp.where((lane >= HEAD_DIM) == bool(half), qp, 0.0).astype(BF16)
            s = _qk(kp, qm) + bias_ref[shift, 2 * pair + half]
            p = jnp.exp(s - jnp.max(s, axis=0, keepdims=True))
            p = (p * (1.0 / jnp.sum(p, axis=0, keepdims=True))).astype(BF16)
            halves.append(_dot(vp, p))
        chan = lax.broadcasted_iota(jnp.int32, (128, ATT_Q), 0)
        o_ref[:, sl] = jnp.where(chan < HEAD_DIM, halves[0], halves[1]).T.astype(BF16)


def _attn_prompt(proj, diag, batch, seq):
    n_c = seq // ATT_Q
    once = dict(pipeline_mode=pl.Buffered(1))
    return pl.pallas_call(
        _attn_prompt_body,
        grid=(batch, n_c),
        in_specs=[
            pl.BlockSpec((ATT_Q, WIDTH), lambda b, c: (b * n_c + c, COL_Q // WIDTH)),
            pl.BlockSpec((seq, WIDTH), lambda b, c: (b, COL_K // WIDTH), **once),
            pl.BlockSpec((seq, WIDTH), lambda b, c: (b, COL_V // WIDTH), **once),
            pl.BlockSpec(diag.shape, lambda b, c: (0, 0, 0)),
        ],
        out_specs=pl.BlockSpec((ATT_Q, WIDTH), lambda b, c: (b * n_c + c, 0)),
        out_shape=jax.ShapeDtypeStruct((batch * seq, WIDTH), BF16),
        scratch_shapes=[pltpu.VMEM((seq, WIDTH), BF16), pltpu.VMEM((WIDTH, seq), BF16),
                        pltpu.VMEM((ATT_LEAD + 1, HEADS, ATT_KEYS, ATT_Q), F32)],
        compiler_params=pltpu.CompilerParams(
            dimension_semantics=("arbitrary", "arbitrary"), vmem_limit_bytes=VMEM_LIMIT_MAX),
        name="attn_prompt",
    )(proj, proj, proj, diag)


def _attn_sample_body(q_ref, kn_ref, vn_ref, ck_ref, cv_ref, bias_c_ref, bias_n_ref, o_ref):
    t = q_ref.shape[0]
    lane = lax.broadcasted_iota(jnp.int32, (t, 128), 1)
    slabs = [slice(pair * 128, (pair + 1) * 128) for pair in range(HEADS // 2)]

    s_c, s_n = [], []
    for h in range(HEADS):
        sl = slabs[h // 2]
        qm = jnp.where((lane >= HEAD_DIM) == bool(h % 2), q_ref[:, sl] * (HEAD_DIM ** -0.5), 0.0).astype(BF16)
        s_c.append(_dot(qm, ck_ref[0, sl, :].astype(BF16)) + bias_c_ref[h])
        s_n.append(_qk(qm, kn_ref[:, sl].astype(BF16)) + bias_n_ref[h])
    s_c = jnp.concatenate(s_c, axis=0)
    s_n = jnp.concatenate(s_n, axis=0)
    m = jnp.maximum(jnp.max(s_c, axis=-1, keepdims=True), jnp.max(s_n, axis=-1, keepdims=True))
    p_c = jnp.exp(s_c - m)
    p_n = jnp.exp(s_n - m)
    inv = 1.0 / (jnp.sum(p_c, axis=-1, keepdims=True) + jnp.sum(p_n, axis=-1, keepdims=True))
    p_c = (p_c * inv).astype(BF16)
    p_n = (p_n * inv).astype(BF16)
    for pair, sl in enumerate(slabs):
        cv = cv_ref[0, sl, :].astype(BF16)
        vn = vn_ref[:, sl].astype(BF16)
        halves = []
        for h in (2 * pair, 2 * pair + 1):
            rows = slice(h * t, (h + 1) * t)
            halves.append(_qk(p_c[rows], cv) + _dot(p_n[rows], vn))
        o_ref[:, sl] = jnp.where(lane < HEAD_DIM, halves[0], halves[1]).astype(BF16)


def _attn_sample(proj, row0, cache_k, cache_v, bias_c, bias_n, batch, t):
    n_past = cache_k.shape[2]
    blk0 = row0 // t
    return pl.pallas_call(
        _attn_sample_body,
        grid=(batch,),
        in_specs=[
            pl.BlockSpec((t, WIDTH), lambda b: (blk0 + b, COL_Q // WIDTH)),
            pl.BlockSpec((t, WIDTH), lambda b: (blk0 + b, COL_K // WIDTH)),
            pl.BlockSpec((t, WIDTH), lambda b: (blk0 + b, COL_V // WIDTH)),
            pl.BlockSpec((1, WIDTH, n_past), lambda b: (b, 0, 0)),
            pl.BlockSpec((1, WIDTH, n_past), lambda b: (b, 0, 0)),
            pl.BlockSpec((HEADS, t, n_past), lambda b: (0, 0, 0)),
            pl.BlockSpec((HEADS, t, t), lambda b: (0, 0, 0)),
        ],
        out_specs=pl.BlockSpec((t, WIDTH), lambda b: (b, 0)),
        out_shape=jax.ShapeDtypeStruct((batch * t, WIDTH), BF16),
        compiler_params=pltpu.CompilerParams(
            dimension_semantics=("parallel",), vmem_limit_bytes=VMEM_LIMIT),
        name="attn_sample",
    )(proj, proj, proj, cache_k, cache_v, bias_c, bias_n)


def _prep_body(p_ref, pl_ref, bp_ref, bl_ref, mu_ref, mul_ref, w0_ref, a0_ref, kk_ref, ka_ref, rk_ref,
               lnb_ref, wup_hi, wup_lo, aup_hi, aup_lo, gup_hi, gup_lo, bd_ref,
               r_o, w_o, k_o, v_o, kk_o, kka_o, g_o, cb_o):
    def shifted(p, first_row, mu):
        row = lax.broadcasted_iota(jnp.int32, p.shape, 0)
        prev = jnp.where(row == 0, first_row, pltpu.roll(p, 1, axis=0))
        return p + (prev - p) * mu

    xm = shifted(p_ref[...], bp_ref[0], mu_ref[...])
    xl = shifted(pl_ref[...], bl_ref[0], mul_ref[...])
    r = xm[:, 0:WIDTH]
    k = xm[:, WIDTH:2 * WIDTH]
    v = xm[:, 2 * WIDTH:3 * WIDTH]
    wd = xl[:, 0:LORA_PAD]
    ad = xl[:, LORA_PAD:2 * LORA_PAD]
    gd = xl[:, 2 * LORA_PAD:]

    z = -(w0_ref[...] + _dot_f32(jnp.tanh(wd), wup_hi[...], wup_lo[...]))
    softplus = jnp.maximum(z, 0.0) + jnp.log(1.0 + jnp.exp(-jnp.abs(z)))
    decay = jnp.exp(-jnp.exp(-softplus - 0.5))
    a = _sigmoid(a0_ref[...] + _dot_f32(ad, aup_hi[...], aup_lo[...]))
    g = _dot_f32(_sigmoid(gd), gup_hi[...], gup_lo[...])

    bd = bd_ref[...]
    kk = k * kk_ref[...]
    kk = kk / jnp.maximum(jnp.sqrt(_segsum(kk * kk, bd)), 1e-12)
    k2 = k * (1.0 + (a - 1.0) * ka_ref[...])
    bonus = _segsum(r * k2 * rk_ref[...], bd) * v

    r_o[...] = r
    w_o[...] = decay
    k_o[...] = k2
    v_o[...] = v
    kk_o[...] = kk
    kka_o[...] = kk * a
    g_o[...] = g
    cb_o[...] = lnb_ref[...] + bonus


def _prep(proj, row0, n_rows, tm, first_rkv, first_lora, params):
    n_t = n_rows // tm
    blk0 = row0 // tm
    vec = lambda w: pl.BlockSpec((1, w), lambda i: (0, 0))
    mat = lambda r: pl.BlockSpec((r, WIDTH), lambda i: (0, 0))
    out = pl.BlockSpec((tm, WIDTH), lambda i: (i, 0))
    return pl.pallas_call(
        _prep_body,
        grid=(n_t,),
        in_specs=[
            pl.BlockSpec((tm, 3 * WIDTH), lambda i: (blk0 + i, COL_RKV // (3 * WIDTH))),
            pl.BlockSpec((tm, LORA_COLS), lambda i: (blk0 + i, COL_LORA // LORA_COLS)),
            pl.BlockSpec((1, 1, 3 * WIDTH), lambda i: (i, 0, 0)),
            pl.BlockSpec((1, 1, LORA_COLS), lambda i: (i, 0, 0)),
            vec(3 * WIDTH), vec(LORA_COLS), vec(WIDTH), vec(WIDTH), vec(WIDTH), vec(WIDTH), vec(WIDTH),
            vec(WIDTH), mat(LORA_PAD), mat(LORA_PAD), mat(LORA_PAD), mat(LORA_PAD),
            mat(GATE_LORA), mat(GATE_LORA),
            pl.BlockSpec((SEG_HEADS * HEAD_DIM, SEG_HEADS * HEAD_DIM), lambda i: (0, 0)),
        ],
        out_specs=[out] * 8,
        out_shape=[jax.ShapeDtypeStruct((n_rows, WIDTH), F32)] * 8,
        compiler_params=pltpu.CompilerParams(
            dimension_semantics=("parallel",), vmem_limit_bytes=VMEM_LIMIT),
        name="rwkv_prep",
    )(proj, proj, first_rkv, first_lora, *params)


def _scan_steps(s_ref, row, get_v, put_y, n_steps):
    W, KKA, K, R, KK = range(5)
    zero = jnp.zeros((KLO, 128), F32)

    sa0 = [zero, zero]
    for k in range(HEAD_DIM):
        sa0[k % 2] = sa0[k % 2] - s_ref[k] * row(KK, 0, k)

    def step(t, sa):
        v = get_v(t)
        t_next = jnp.minimum(t + 1, n_steps - 1)
        y = [zero, zero]
        sa_next = [zero, zero]
        for k in range(HEAD_DIM):
            s_new = s_ref[k] * row(W, t, k) + sa * row(KKA, t, k) + v * row(K, t, k)
            s_ref[k] = s_new
            y[k % 2] = y[k % 2] + s_new * row(R, t, k)
            sa_next[k % 2] = sa_next[k % 2] - s_new * row(KK, t_next, k)
        put_y(t, y[0] + y[1])
        return sa_next[0] + sa_next[1]

    lax.fori_loop(0, n_steps, step, sa0[0] + sa0[1])


def _scan_body(w_ref, kka_ref, k_ref, r_ref, kk_ref, v_ref, s0_ref, y_ref, st_ref, s_ref, rows_ref, *, tb):
    @pl.when(pl.program_id(1) == 0)
    def _():
        s_ref[...] = s0_ref[0]

    lane = lax.broadcasted_iota(jnp.int32, (tb * KLO, 128), 1)
    for a, ref in enumerate((w_ref, kka_ref, k_ref, r_ref, kk_ref)):
        x = ref[0].reshape(tb * KLO, 128)
        swapped = pltpu.roll(x, HEAD_DIM, axis=1)
        rows_ref[a, :, 0:KLO, :] = jnp.where(lane < HEAD_DIM, x, swapped).reshape(tb, KLO, 128)
        rows_ref[a, :, KLO:, :] = jnp.where(lane < HEAD_DIM, swapped, x).reshape(tb, KLO, 128)

    def row(a, t, k):
        return jnp.broadcast_to(rows_ref[a, t, pl.ds(k, 1), :], (KLO, 128))

    def put_y(t, y):
        y_ref[0, t] = y

    _scan_steps(s_ref, row, lambda t: v_ref[0, t], put_y, tb)

    @pl.when(pl.program_id(1) == pl.num_programs(1) - 1)
    def _():
        st_ref[0] = s_ref[...]


def _scan_tok_body(w_hbm, kka_hbm, k_hbm, r_hbm, kk_hbm, v_hbm, s0_ref, y_ref, st_ref,
                   s_ref, rows_ref, xt_ref, vt_ref, yt_ref, in_ref, in_sem, *, tb):
    step = pl.program_id(0)
    operands = (w_hbm, kka_hbm, k_hbm, r_hbm, kk_hbm, v_hbm)

    def block_copy(a, blk):
        return pltpu.make_async_copy(operands[a].at[:, pl.ds(blk * tb, tb), :], in_ref.at[a], in_sem.at[a])

    @pl.when(step == 0)
    def _():
        for a in range(len(operands)):
            block_copy(a, 0).start()
        s_ref[...] = s0_ref[0]
        xt_ref[...] = jnp.zeros_like(xt_ref)

    def head_rows(c):
        return pl.ds(c, HEADS, stride=XT_PITCH)

    def channel_major(ref):
        for b in range(LANE_GROUP_BATCH):
            xt = ref[b].T
            for h in range(HEADS):
                xt_ref[b, h * XT_PITCH:h * XT_PITCH + HEAD_DIM, 0:tb] = xt[h * HEAD_DIM:(h + 1) * HEAD_DIM]

    def lane_tile(chan):
        parts = [xt_ref[b, head_rows(c), :] for c in chan for b in range(LANE_GROUP_BATCH)]
        return jnp.concatenate(parts, axis=0).T[0:tb]

    for a in range(5):
        block_copy(a, step).wait()
        channel_major(in_ref.at[a])
        for k in range(HEAD_DIM):
            rows_ref[a, k] = lane_tile((k, k))
    block_copy(5, step).wait()
    channel_major(in_ref.at[5])
    for j in range(KLO):
        vt_ref[pl.ds(j, tb, stride=STEP_PITCH), :] = lane_tile((j, KLO + j))

    @pl.when(step + 1 < pl.num_programs(0))
    def _():
        for a in range(len(operands)):
            block_copy(a, step + 1).start()

    def row(a, t, k):
        return jnp.broadcast_to(rows_ref[a, k, pl.ds(t, 1), :], (KLO, 128))

    def step_rows(t):
        return pl.ds(pl.multiple_of(t * STEP_PITCH, 8), KLO)

    def put_y(t, y):
        yt_ref[step_rows(t), :] = y

    _scan_steps(s_ref, row, lambda t: vt_ref[step_rows(t), :], put_y, tb)

    for j in range(KLO):
        steps = yt_ref[pl.ds(j, tb, stride=STEP_PITCH), :]
        if tb < 128:
            steps = jnp.concatenate([steps, jnp.zeros((128 - tb, 128), F32)], axis=0)
        tile = steps.T
        for half in range(2):
            for b in range(LANE_GROUP_BATCH):
                r0 = (half * LANE_GROUP_BATCH + b) * HEADS
                xt_ref[b, head_rows(half * KLO + j), :] = tile[r0:r0 + HEADS]
    for b in range(LANE_GROUP_BATCH):
        heads = [xt_ref[b, h * XT_PITCH:h * XT_PITCH + HEAD_DIM, 0:tb] for h in range(HEADS)]
        y_ref[b] = jnp.concatenate(heads, axis=0).T

    @pl.when(pl.program_id(0) == pl.num_programs(0) - 1)
    def _():
        st_ref[0] = s_ref[...]


def _scan_tok(ops, v, s0, tb):
    nb, t, _ = v.shape
    tok_spec = pl.BlockSpec((nb, tb, WIDTH), lambda j: (0, j, 0))
    st_spec = pl.BlockSpec((1, HEAD_DIM, KLO, 128), lambda j: (0, 0, 0, 0))
    return pl.pallas_call(
        functools.partial(_scan_tok_body, tb=tb),
        grid=(t // tb,),
        in_specs=[pl.BlockSpec(memory_space=pl.ANY)] * 6 + [st_spec],
        out_specs=[tok_spec, st_spec],
        out_shape=[jax.ShapeDtypeStruct((nb, t, WIDTH), F32),
                   jax.ShapeDtypeStruct((1, HEAD_DIM, KLO, 128), F32)],
        scratch_shapes=[pltpu.VMEM((HEAD_DIM, KLO, 128), F32),
                        pltpu.VMEM((5, HEAD_DIM, tb, 128), F32),
                        pltpu.VMEM((nb, HEADS * XT_PITCH, 128), F32),
                        pltpu.VMEM((tb * STEP_PITCH, 128), F32),
                        pltpu.VMEM((tb * STEP_PITCH, 128), F32),
                        pltpu.VMEM((6, nb, tb, WIDTH), F32),
                        pltpu.SemaphoreType.DMA((6,))],
        compiler_params=pltpu.CompilerParams(
            dimension_semantics=("arbitrary",), vmem_limit_bytes=VMEM_LIMIT),
        name="rwkv_scan_tok",
    )(*ops, v, s0)


def _scan(rows, v, s0, tb):
    g, t = v.shape[0], v.shape[1]
    row_spec = pl.BlockSpec((1, tb, KLO, 128), lambda i, j: (i, j, 0, 0))
    st_spec = pl.BlockSpec((1, HEAD_DIM, KLO, 128), lambda i, j: (i, 0, 0, 0))
    return pl.pallas_call(
        functools.partial(_scan_body, tb=tb),
        grid=(g, t // tb),
        in_specs=[row_spec] * 6 + [st_spec],
        out_specs=[row_spec, st_spec],
        out_shape=[jax.ShapeDtypeStruct((g, t, KLO, 128), F32),
                   jax.ShapeDtypeStruct((g, HEAD_DIM, KLO, 128), F32)],
        scratch_shapes=[pltpu.VMEM((HEAD_DIM, KLO, 128), F32),
                        pltpu.VMEM((5, tb, HEAD_DIM, 128), F32)],
        compiler_params=pltpu.CompilerParams(
            dimension_semantics=("parallel", "arbitrary"), vmem_limit_bytes=VMEM_LIMIT),
        name="rwkv_scan",
    )(*rows, v, s0)


def _rows_to_scan(x, n_batch, t):
    g = n_batch // LANE_GROUP_BATCH
    x = x.reshape(g, LANE_GROUP_BATCH, t, HEADS, 2, KLO)
    return x.transpose(0, 2, 5, 4, 1, 3).reshape(g, t, KLO, 128)


def _rows_from_scan(y, n_batch, t):
    g = n_batch // LANE_GROUP_BATCH
    y = y.reshape(g, t, KLO, 2, LANE_GROUP_BATCH, HEADS).transpose(0, 4, 1, 5, 3, 2)
    return y.reshape(n_batch * t, WIDTH)


def _state_to_scan(s, n_batch):
    g = n_batch // LANE_GROUP_BATCH
    s = s.reshape(g, LANE_GROUP_BATCH, HEADS, 2, KLO, HEAD_DIM)
    return s.transpose(0, 5, 4, 3, 1, 2).reshape(g, HEAD_DIM, KLO, 128)


def _state_from_scan(s, n_batch):
    g = n_batch // LANE_GROUP_BATCH
    s = s.reshape(g, HEAD_DIM, KLO, 2, LANE_GROUP_BATCH, HEADS)
    return s.transpose(0, 4, 5, 3, 2, 1).reshape(n_batch, HEADS, HEAD_DIM, HEAD_DIM)


def _merge_body(att_ref, y_ref, g_ref, cb_ref, ga_ref, gb_ref, x_ref, lnw_ref, gpost_ref, bd_ref,
                wa_ref, wr_ref, wo_ref, o_ref):
    bd = bd_ref[...]
    y = y_ref[...]
    d = y - _segsum(y, bd) * (1.0 / HEAD_DIM)
    var = _segsum(d * d, bd) * (1.0 / HEAD_DIM)
    rw = ((d * lax.rsqrt(var + GN_EPS) * lnw_ref[...] + cb_ref[...]) * g_ref[...]).astype(BF16)
    rw_o = _dot(rw, wr_ref[...])
    att_o = _dot(att_ref[...], wa_ref[...])
    mixed = (_sigmoid(ga_ref[...]) * att_o + _sigmoid(gb_ref[...]) * rw_o).astype(BF16)
    z = _dot(mixed, wo_ref[...])
    o_ref[...] = x_ref[...] + z * _rms_scale(z) * gpost_ref[...]


def _merge(att, y, g, cb, proj, x, ln_w, g_post, ones_bd, w_att, w_rwkv, w_out):
    m = x.shape[0]
    tm = min(MERGE_TM, m)
    tok = lambda w: pl.BlockSpec((tm, w), lambda i: (i, 0))
    const = lambda a: pl.BlockSpec(a.shape, lambda i: (0,) * a.ndim)
    return pl.pallas_call(
        _merge_body,
        grid=(m // tm,),
        in_specs=[
            tok(WIDTH), tok(WIDTH), tok(WIDTH), tok(WIDTH),
            pl.BlockSpec((tm, D_MODEL), lambda i: (i, COL_GA // D_MODEL)),
            pl.BlockSpec((tm, D_MODEL), lambda i: (i, COL_GB // D_MODEL)),
            tok(D_MODEL), const(ln_w), const(g_post), const(ones_bd),
            const(w_att), const(w_rwkv), const(w_out),
        ],
        out_specs=tok(D_MODEL),
        out_shape=jax.ShapeDtypeStruct((m, D_MODEL), F32),
        compiler_params=pltpu.CompilerParams(
            dimension_semantics=("parallel",), vmem_limit_bytes=VMEM_LIMIT),
        name="merge",
    )(att, y, g, cb, proj, proj, x, ln_w, g_post, ones_bd, w_att, w_rwkv, w_out)


def _pad_rows(x, height):
    return jnp.pad(x, ((0, height - x.shape[0]), (0, 0)))


def _reorder_shift_cols(x):
    o = 3 * WIDTH
    wd = x[..., o:o + DECAY_LORA]
    ad = x[..., o + DECAY_LORA:o + DECAY_LORA + ICLR_LORA]
    gd = x[..., o + DECAY_LORA + ICLR_LORA:]
    pad = [(0, 0)] * (x.ndim - 1)
    lora = jnp.concatenate([jnp.pad(wd, pad + [(0, LORA_PAD - DECAY_LORA)]),
                            jnp.pad(ad, pad + [(0, LORA_PAD - ICLR_LORA)]), gd], axis=-1)
    return x[..., :o], lora


def _shift_row_from_proj(rows):
    lo = rows[:, COL_LORA:]
    return jnp.concatenate([rows[:, COL_RKV:COL_RKV + 3 * WIDTH], lo[:, :DECAY_LORA],
                            lo[:, LORA_PAD:LORA_PAD + ICLR_LORA], lo[:, 2 * LORA_PAD:]], axis=-1)


def _bias_diagonals(rel_bias):
    u = np.arange(ATT_DIAG)[None, :]
    s = np.arange(ATT_LEAD + 1)[:, None]
    idx = np.clip(ATT_KEYS - u - ATT_Q * s, -MAX_REL, MAX_REL) + MAX_REL
    tab = jnp.take(rel_bias, jnp.asarray(idx.reshape(-1), jnp.int32), axis=1)
    return tab.reshape(HEADS, ATT_LEAD + 1, ATT_DIAG).transpose(1, 0, 2)


def _bias_sample(rel_bias, t, n_past):
    rel = np.arange(t)[:, None] + n_past - np.arange(n_past + t)[None, :]
    idx = np.clip(rel, -MAX_REL, MAX_REL) + MAX_REL
    tab = jnp.take(rel_bias, jnp.asarray(idx.reshape(-1), jnp.int32), axis=1).reshape(HEADS, t, n_past + t)
    return tab[:, :, :n_past], tab[:, :, n_past:]


def _layer(x, n_batch, t, attend, first_rows, prep_tm, wkv0, scan_tb, wts):
    m = n_batch * t
    x = _ffn(x, *wts["ffn1"])
    proj = _proj(x, *wts["proj"])
    att = attend(proj)

    prep = _prep(proj, 0, m, prep_tm, *first_rows(proj), wts["prep"])
    r, wdec, k2, v, kk, kka, g, cb = prep
    ops = (wdec, kka, k2, r, kk)
    if n_batch == LANE_GROUP_BATCH and scan_tb % 64 == 0:
        tok = lambda a: a.reshape(n_batch, t, WIDTH)
        y, s_t = _scan_tok([tok(a) for a in ops], tok(v), _state_to_scan(wkv0, n_batch), scan_tb)
        y = y.reshape(m, WIDTH)
    else:
        rows = [_rows_to_scan(a, n_batch, t) for a in ops]
        y, s_t = _scan(rows, _rows_to_scan(v, n_batch, t), _state_to_scan(wkv0, n_batch), scan_tb)
        y = _rows_from_scan(y, n_batch, t)

    x = _merge(att, y, g, cb, proj, x, *wts["merge"])
    x = _ffn(x, *wts["ffn2"])

    last = jnp.concatenate([proj[(b + 1) * t - 1:(b + 1) * t] for b in range(n_batch)])
    return x, proj, _state_from_scan(s_t, n_batch), _shift_row_from_proj(last)[:, None]


def kernel(x_prompt, x_sample, cache_k, cache_v, state_wkv, state_shift, norm_ffn1_pre, norm_ffn1_post, w_ffn1_in, w_ffn1_down, norm_mix_pre, norm_mix_post, w_in, rel_bias, w_att_out, rwkv_mu, rwkv_w0, rwkv_w_up, rwkv_a0, rwkv_a_up, rwkv_g_up, rwkv_k_k, rwkv_k_a, rwkv_r_k, rwkv_ln_w, rwkv_ln_b, w_rwkv_out, w_out, norm_ffn2_pre, norm_ffn2_post, w_ffn2_in, w_ffn2_down):
    depth = w_in.shape[0]
    batch, seq, _ = x_prompt.shape
    dec_batch, dec_seq, _ = x_sample.shape
    n_past = cache_k.shape[2]
    m_p, m_s = batch * seq, dec_batch * dec_seq
    keep = min(PAST_BAND, seq)
    assert n_past == PAST_BAND and seq % ATT_Q == 0 and seq // ATT_Q > ATT_LEAD
    assert m_p % FFN_TM == 0 and m_p % PROJ_TM == 0 and m_p % MERGE_TM == 0 and seq % PREP_TM == 0
    assert seq % SCAN_TB == 0 and batch == LANE_GROUP_BATCH and dec_batch % LANE_GROUP_BATCH == 0

    x_p = x_prompt.reshape(m_p, D_MODEL)
    x_s = x_sample.reshape(m_s, D_MODEL)
    ones_bd = jnp.asarray(np.kron(np.eye(SEG_HEADS), np.ones((HEAD_DIM, HEAD_DIM))), BF16)
    row = lambda v: v.reshape(1, -1)
    heads = lambda a, n, t: a.reshape(n, t, HEADS, HEAD_DIM)

    outs = {k: [] for k in ("kp", "vp", "wp", "sp", "ks", "vs", "ws", "ss")}
    for l in range(depth):
        w_proj = _repack_proj_weight(w_in[l].T)
        mu_rkv, mu_lora = _reorder_shift_cols(rwkv_mu[l][None])
        wts = {
            "ffn1": (row(norm_ffn1_pre[l]), 0.5 * row(norm_ffn1_post[l]),
                     w_ffn1_in[l], w_ffn1_down[l]),
            "proj": (row(norm_mix_pre[l]), w_proj),
            "prep": (mu_rkv, mu_lora, row(rwkv_w0[l]), row(rwkv_a0[l]), row(rwkv_k_k[l]), row(rwkv_k_a[l]),
                     row(rwkv_r_k[l]), row(rwkv_ln_b[l]),
                     *_split_bf16(_pad_rows(rwkv_w_up[l], LORA_PAD)),
                     *_split_bf16(_pad_rows(rwkv_a_up[l], LORA_PAD)),
                     *_split_bf16(rwkv_g_up[l]), ones_bd),
            "merge": (row(rwkv_ln_w[l]), row(norm_mix_post[l]), ones_bd,
                      w_att_out[l].astype(BF16), w_rwkv_out[l].astype(BF16), w_out[l].astype(BF16)),
            "ffn2": (row(norm_ffn2_pre[l]), 0.5 * row(norm_ffn2_post[l]),
                     w_ffn2_in[l], w_ffn2_down[l]),
        }

        diag = _bias_diagonals(rel_bias[l])
        starts_seq = (jnp.arange(m_p // PREP_TM) % (seq // PREP_TM) == 0)[:, None]

        def first_rows_prompt(proj):
            last_of_tile = proj.reshape(m_p // PREP_TM, PREP_TM, PROJ_COLS)[:, PREP_TM - 1]
            first = jnp.where(starts_seq, 0.0, jnp.roll(last_of_tile, 1, axis=0))[:, None]
            return first[..., COL_RKV:COL_RKV + 3 * WIDTH], first[..., COL_LORA:]

        x_p, proj_p, wkv_p, sh_p = _layer(
            x_p, batch, seq, lambda proj: _attn_prompt(proj, diag, batch, seq), first_rows_prompt, PREP_TM,
            jnp.zeros((batch, HEADS, HEAD_DIM, HEAD_DIM), F32), SCAN_TB, wts)
        tail = lambda col: jnp.stack([proj_p[(b + 1) * seq - keep:(b + 1) * seq, col:col + WIDTH]
                                      for b in range(batch)])
        outs["kp"].append(heads(tail(COL_K), batch, keep))
        outs["vp"].append(heads(tail(COL_V), batch, keep))
        outs["wp"].append(wkv_p.astype(state_wkv.dtype))
        outs["sp"].append(sh_p)

        chan_major = lambda c: c.transpose(0, 2, 3, 1).reshape(dec_batch, WIDTH, n_past)
        ck, cv = chan_major(cache_k[l]), chan_major(cache_v[l])
        bias_c, bias_n = _bias_sample(rel_bias[l], dec_seq, n_past)
        x_s, proj_s, wkv_s, sh_s = _layer(
            x_s, dec_batch, dec_seq,
            lambda proj: _attn_sample(proj, 0, ck, cv, bias_c, bias_n, dec_batch, dec_seq),
            lambda proj: _reorder_shift_cols(state_shift[l]), dec_seq,
            state_wkv[l].astype(F32), dec_seq, wts)
        outs["ks"].append(heads(proj_s[:, COL_K:COL_K + WIDTH], dec_batch, dec_seq))
        outs["vs"].append(heads(proj_s[:, COL_V:COL_V + WIDTH], dec_batch, dec_seq))
        outs["ws"].append(wkv_s.astype(state_wkv.dtype))
        outs["ss"].append(sh_s)

    st = lambda k: jnp.stack(outs[k])
    return (x_p.reshape(batch, seq, D_MODEL), x_s.reshape(dec_batch, dec_seq, D_MODEL),
            st("kp"), st("vp"), st("wp"), st("sp"), st("ks"), st("vs"), st("ws"), st("ss"))
```
